```python
import jax, jax.numpy as jnp
from jax import lax
import numpy as np

D_MODEL = 1024
BATCH = 32
SEQ = 2048
DEPTH = 2

D_MIX = D_MODEL
HEAD_DIM = 64
N_FOX_HEADS = 4
N_GMLP_GROUPS = 4
N_NSA_HEADS = 4
N_NSA_KV = 1
N_POOL_GROUPS = 4
W_FOX = N_FOX_HEADS * HEAD_DIM
W_GMLP = N_GMLP_GROUPS * HEAD_DIM
W_NSA = N_NSA_HEADS * HEAD_DIM
W_POOL = N_POOL_GROUPS * HEAD_DIM
W_NSA_KV = N_NSA_KV * HEAD_DIM
IN_SPLITS = (W_FOX, W_FOX, W_FOX, N_FOX_HEADS, W_GMLP, W_GMLP, W_NSA, W_NSA_KV, W_NSA_KV, W_NSA_KV, W_NSA_KV, W_NSA_KV, W_NSA_KV, 3 * N_NSA_HEADS, W_POOL)
N_IN = 3 * W_FOX + N_FOX_HEADS + 2 * W_GMLP + W_NSA + 6 * W_NSA_KV + 3 * N_NSA_HEADS + W_POOL
D_FF = 2816
ROPE_THETA = 500000.0
ROPE_DIM = HEAD_DIM // 4
Q_BLOCK = 128
NSA_Q_BLOCK = 64
GMLP_CHUNK = 128
CMP_LEN = 32
CMP_STRIDE = 16
CMP_HIDDEN = 256
SEL_LEN = 64
SEL_TOP = 16
WINDOW = 512
POOL_SIZES = (2, 4, 8, 16)
FFN_RES_WEIGHT = 0.5
EPS = 1e-6
NEG_INF = -1e30
SEL_FORCE = 1e3

kernel_name = 'hybrid_fox_gmlp_nsa_pool_macaron'

F32 = jnp.float32


def rms_norm(x, g):
    xf = x.astype(F32)
    y = xf * lax.rsqrt(jnp.mean(xf * xf, axis=-1, keepdims=True) + EPS)
    return (y * g.astype(F32)).astype(x.dtype)


def swiglu_ffn(x, g, w1, w3, w2):
    h = rms_norm(x, g)
    return (jax.nn.silu(h @ w1) * (h @ w3)) @ w2


def rope_partial(x, pos):
    half = ROPE_DIM // 2
    inv = ROPE_THETA ** (-jnp.arange(half, dtype=F32) * 2.0 / ROPE_DIM)
    ang = pos.astype(F32)[:, None] * inv[None, :]
    cos = jnp.cos(ang)[:, None, :].astype(x.dtype)
    sin = jnp.sin(ang)[:, None, :].astype(x.dtype)
    x1 = x[..., :half]
    x2 = x[..., half:ROPE_DIM]
    return jnp.concatenate([x1 * cos - x2 * sin, x2 * cos + x1 * sin, x[..., ROPE_DIM:]], axis=-1)


def split_mixing_columns(p):
    offs = []
    acc = 0
    for s in IN_SPLITS[:-1]:
        acc += s
        offs.append(acc)
    return jnp.split(p, offs, axis=-1)


def fox_mixer(q, k, v, f_logit, f_bias, g_q, g_k):
    B, T, H, Dh = q.shape
    q = rms_norm(q, g_q)
    k = rms_norm(k, g_k)
    log_f = jax.nn.log_sigmoid((f_logit + f_bias).astype(F32))
    c = jnp.cumsum(log_f, axis=1).transpose(0, 2, 1)
    scale = Dh ** -0.5
    kpos = jnp.arange(T)

    def block(i):
        t0 = i * Q_BLOCK
        q_i = lax.dynamic_slice_in_dim(q, t0, Q_BLOCK, axis=1)
        c_i = lax.dynamic_slice_in_dim(c, t0, Q_BLOCK, axis=2)
        s = jnp.einsum('bqhd,bshd->bhqs', q_i, k).astype(F32) * scale
        s = s + c_i[..., :, None] - c[:, :, None, :]
        qpos = t0 + jnp.arange(Q_BLOCK)
        s = jnp.where(kpos[None, :] <= qpos[:, None], s, NEG_INF)
        p = jax.nn.softmax(s, axis=-1).astype(v.dtype)
        return jnp.einsum('bhqs,bshd->bqhd', p, v)

    o = lax.map(block, jnp.arange(T // Q_BLOCK))
    return o.transpose(1, 0, 2, 3, 4).reshape(B, T, H * Dh)


def gmlp_mixer(u, v, g_v, w_s, b_s):
    B, T, W = u.shape
    G = N_GMLP_GROUPS
    Dg = W // G
    C = GMLP_CHUNK
    u = jax.nn.gelu(u)
    v = rms_norm(jax.nn.gelu(v).reshape(B, T, G, Dg), g_v.reshape(G, Dg))
    vc = v.reshape(B, T // C, C, G, Dg)
    w = w_s * jnp.tril(jnp.ones((C, C), w_s.dtype))
    s = jnp.einsum('gts,bcsgd->bctgd', w, vc) + b_s.T[:, :, None]
    return u * s.reshape(B, T, W)


def compress_kv(kv, pos_emb, w1, w2):
    B, T, Hk, Dh = kv.shape
    nc = (T - CMP_LEN) // CMP_STRIDE + 1
    idx = jnp.arange(nc)[:, None] * CMP_STRIDE + jnp.arange(CMP_LEN)[None, :]
    blocks = kv[:, idx] + pos_emb[None, None, :, None, :]
    flat = blocks.transpose(0, 1, 3, 2, 4).reshape(B, nc, Hk, CMP_LEN * Dh)
    return jax.nn.gelu(flat @ w1) @ w2


def cmp_to_sel_overlap(nc, nsel):
    cs = np.arange(nc) * CMP_STRIDE
    ce = cs + CMP_LEN
    ss = np.arange(nsel) * SEL_LEN
    se = ss + SEL_LEN
    ov = np.clip(np.minimum(ce[:, None], se[None, :]) - np.maximum(cs[:, None], ss[None, :]), 0, None)
    return jnp.asarray(ov / CMP_LEN, dtype=F32)


def nsa_mixer(q, kc, vc, ks, vs, kw, vw, gate_logit, gate_b, g_q, g_kc, g_ks, g_kw,
              pos_k, k_w1, k_w2, pos_v, v_w1, v_w2):
    B, T = q.shape[:2]
    H, Hk, Dh = N_NSA_HEADS, N_NSA_KV, HEAD_DIM
    Hg = H // Hk
    Qb = NSA_Q_BLOCK
    scale = Dh ** -0.5
    pos = jnp.arange(T)
    q = rope_partial(rms_norm(q.reshape(B, T, H, Dh), g_q), pos).reshape(B, T, Hk, Hg, Dh)
    kc, vc, ks, vs, kw, vw = [a.reshape(B, T, Hk, Dh) for a in (kc, vc, ks, vs, kw, vw)]

    nc = (T - CMP_LEN) // CMP_STRIDE + 1
    cmp_end = jnp.arange(nc) * CMP_STRIDE + CMP_LEN - 1
    k_cmp = rope_partial(rms_norm(compress_kv(kc, pos_k, k_w1, k_w2), g_kc), cmp_end)
    v_cmp = compress_kv(vc, pos_v, v_w1, v_w2)
    s = jnp.einsum('btghd,bngd->bghtn', q, k_cmp).astype(F32) * scale
    m_cmp = cmp_end[None, :] <= pos[:, None]
    p_cmp = jnp.where(m_cmp, jax.nn.softmax(jnp.where(m_cmp, s, NEG_INF), axis=-1), 0.0)
    o_cmp = jnp.einsum('bghtn,bngd->btghd', p_cmp.astype(v_cmp.dtype), v_cmp)

    nsel = T // SEL_LEN
    n_top = min(SEL_TOP, nsel)
    imp = jnp.einsum('bghtn,nj->bgtj', p_cmp, cmp_to_sel_overlap(nc, nsel))
    blk = jnp.arange(nsel)[None, :]
    cur = (pos // SEL_LEN)[:, None]
    forced = ((blk == 0) | (blk == cur) | (blk == cur - 1)).astype(F32)
    imp = jnp.where(blk <= cur, imp + SEL_FORCE * forced, NEG_INF)
    top_val, top_idx = lax.top_k(imp, n_top)
    top_ok = top_val > NEG_INF * 0.5

    ks = rope_partial(rms_norm(ks, g_ks), pos)
    ks_blocks = ks.reshape(B, nsel, SEL_LEN, Hk, Dh).transpose(0, 3, 1, 2, 4)
    vs_blocks = vs.reshape(B, nsel, SEL_LEN, Hk, Dh).transpose(0, 3, 1, 2, 4)
    kw_pad = jnp.pad(rope_partial(rms_norm(kw, g_kw), pos), ((0, 0), (WINDOW, 0), (0, 0), (0, 0)))
    vw_pad = jnp.pad(vw, ((0, 0), (WINDOW, 0), (0, 0), (0, 0)))
    bi = jnp.arange(B)[:, None, None, None]
    gi = jnp.arange(Hk)[None, :, None, None]
    m_len = n_top * SEL_LEN

    def block(i):
        t0 = i * Qb
        q_i = lax.dynamic_slice_in_dim(q, t0, Qb, axis=1)
        t_i = t0 + jnp.arange(Qb)
        idx = lax.dynamic_slice_in_dim(top_idx, t0, Qb, axis=2)
        ok = lax.dynamic_slice_in_dim(top_ok, t0, Qb, axis=2)
        k_g = ks_blocks[bi, gi, idx].reshape(B, Hk, Qb, m_len, Dh)
        v_g = vs_blocks[bi, gi, idx].reshape(B, Hk, Qb, m_len, Dh)
        kpos = (idx[..., None] * SEL_LEN + jnp.arange(SEL_LEN)).reshape(B, Hk, Qb, m_len)
        m_s = jnp.repeat(ok, SEL_LEN, axis=-1) & (kpos <= t_i[None, None, :, None])
        s = jnp.einsum('bqghd,bgqmd->bghqm', q_i, k_g).astype(F32) * scale
        p = jax.nn.softmax(jnp.where(m_s[:, :, None], s, NEG_INF), axis=-1)
        o_s = jnp.einsum('bghqm,bgqmd->bqghd', p.astype(v_g.dtype), v_g)
        kw_i = lax.dynamic_slice_in_dim(kw_pad, t0, WINDOW + Qb, axis=1)
        vw_i = lax.dynamic_slice_in_dim(vw_pad, t0, WINDOW + Qb, axis=1)
        wpos = t0 - WINDOW + jnp.arange(WINDOW + Qb)
        d = t_i[:, None] - wpos[None, :]
        m_w = (d >= 0) & (d < WINDOW) & (wpos[None, :] >= 0)
        s = jnp.einsum('bqghd,bsgd->bghqs', q_i, kw_i).astype(F32) * scale
        p = jax.nn.softmax(jnp.where(m_w, s, NEG_INF), axis=-1)
        o_w = jnp.einsum('bghqs,bsgd->bqghd', p.astype(vw_i.dtype), vw_i)
        return o_s, o_w

    o_s, o_w = lax.map(block, jnp.arange(T // Qb))
    o_s = o_s.transpose(1, 0, 2, 3, 4, 5).reshape(B, T, Hk, Hg, Dh)
    o_w = o_w.transpose(1, 0, 2, 3, 4, 5).reshape(B, T, Hk, Hg, Dh)
    g = jax.nn.sigmoid((gate_logit + gate_b).astype(F32)).astype(q.dtype).reshape(B, T, Hk, Hg, 3)
    o = g[..., 0:1] * o_cmp + g[..., 1:2] * o_s + g[..., 2:3] * o_w
    return o.reshape(B, T, H * Dh)


def pool_mixer(z, w_p, scale):
    B, T, W = z.shape
    G = N_POOL_GROUPS
    Dg = W // G
    zf = z.astype(F32)
    cs = jnp.cumsum(zf, axis=1)
    cs = jnp.concatenate([jnp.zeros_like(cs[:, :1]), cs], axis=1)
    win = jnp.repeat(jnp.array(POOL_SIZES, jnp.int32), Dg)
    t = jnp.arange(T)[:, None]
    lo = jnp.maximum(t + 1 - win[None, :], 0)
    lo_sum = jnp.take_along_axis(cs, jnp.broadcast_to(lo[None], (B, T, W)), axis=1)
    cnt = jnp.minimum(t + 1, win[None, :]).astype(F32)
    pooled = ((cs[:, 1:] - lo_sum) / cnt - zf).astype(z.dtype).reshape(B, T, G, Dg)
    y = jnp.einsum('btgd,gde->btge', pooled, w_p)
    return y.reshape(B, T, W) * scale


def setup_inputs(seed: int = 0) -> dict:
    key = jax.random.key(seed)
    ks = list(jax.random.split(key, 40))
    L = DEPTH

    def nrm(shape, scale):
        return jax.random.normal(ks.pop(), shape, F32) * scale

    def gain(shape):
        return 1.0 + 0.02 * jax.random.normal(ks.pop(), shape, F32)

    return {
        'x': nrm((BATCH, SEQ, D_MODEL), 1.0),
        'ffn1_norm': gain((L, D_MODEL)),
        'ffn1_w1': nrm((L, D_MODEL, D_FF), D_MODEL ** -0.5),
        'ffn1_w3': nrm((L, D_MODEL, D_FF), D_MODEL ** -0.5),
        'ffn1_w2': nrm((L, D_FF, D_MODEL), D_FF ** -0.5),
        'mix_norm': gain((L, D_MODEL)),
        'w_in': nrm((L, D_MODEL, N_IN), D_MODEL ** -0.5),
        'w_out': nrm((L, D_MIX, D_MODEL), D_MIX ** -0.5),
        'fox_f_bias': jax.random.uniform(ks.pop(), (L, N_FOX_HEADS), F32, 1.0, 4.0),
        'fox_q_norm': gain((L, HEAD_DIM)),
        'fox_k_norm': gain((L, HEAD_DIM)),
        'gmlp_v_norm': gain((L, W_GMLP)),
        'gmlp_w_s': nrm((L, N_GMLP_GROUPS, GMLP_CHUNK, GMLP_CHUNK), GMLP_CHUNK ** -0.5),
        'gmlp_b_s': gain((L, N_GMLP_GROUPS, GMLP_CHUNK)),
        'nsa_q_norm': gain((L, HEAD_DIM)),
        'nsa_kc_norm': gain((L, HEAD_DIM)),
        'nsa_ks_norm': gain((L, HEAD_DIM)),
        'nsa_kw_norm': gain((L, HEAD_DIM)),
        'nsa_cmp_pos_k': nrm((L, CMP_LEN, HEAD_DIM), 0.1),
        'nsa_cmp_k_w1': nrm((L, CMP_LEN * HEAD_DIM, CMP_HIDDEN), (CMP_LEN * HEAD_DIM) ** -0.5),
        'nsa_cmp_k_w2': nrm((L, CMP_HIDDEN, HEAD_DIM), CMP_HIDDEN ** -0.5),
        'nsa_cmp_pos_v': nrm((L, CMP_LEN, HEAD_DIM), 0.1),
        'nsa_cmp_v_w1': nrm((L, CMP_LEN * HEAD_DIM, CMP_HIDDEN), (CMP_LEN * HEAD_DIM) ** -0.5),
        'nsa_cmp_v_w2': nrm((L, CMP_HIDDEN, HEAD_DIM), CMP_HIDDEN ** -0.5),
        'nsa_gate_bias': nrm((L, 3 * N_NSA_HEADS), 0.02),
        'pool_w': nrm((L, N_POOL_GROUPS, W_POOL // N_POOL_GROUPS, W_POOL // N_POOL_GROUPS), (W_POOL // N_POOL_GROUPS) ** -0.5),
        'pool_scale': gain((L, W_POOL)),
        'ffn2_norm': gain((L, D_MODEL)),
        'ffn2_w1': nrm((L, D_MODEL, D_FF), D_MODEL ** -0.5),
        'ffn2_w3': nrm((L, D_MODEL, D_FF), D_MODEL ** -0.5),
        'ffn2_w2': nrm((L, D_FF, D_MODEL), D_FF ** -0.5),
    }


def reference(x, ffn1_norm, ffn1_w1, ffn1_w3, ffn1_w2, mix_norm, w_in, w_out,
              fox_f_bias, fox_q_norm, fox_k_norm, gmlp_v_norm, gmlp_w_s, gmlp_b_s,
              nsa_q_norm, nsa_kc_norm, nsa_ks_norm, nsa_kw_norm,
              nsa_cmp_pos_k, nsa_cmp_k_w1, nsa_cmp_k_w2, nsa_cmp_pos_v, nsa_cmp_v_w1, nsa_cmp_v_w2,
              nsa_gate_bias, pool_w, pool_scale, ffn2_norm, ffn2_w1, ffn2_w3, ffn2_w2):
    B, T, _ = x.shape
    for l in range(DEPTH):
        x = x + FFN_RES_WEIGHT * swiglu_ffn(x, ffn1_norm[l], ffn1_w1[l], ffn1_w3[l], ffn1_w2[l])
        h = rms_norm(x, mix_norm[l])
        (fq, fk, fv, ff, gu, gv, nq, nkc, nvc, nks, nvs, nkw, nvw, ng, pz) = split_mixing_columns(h @ w_in[l])
        fshape = (B, T, N_FOX_HEADS, HEAD_DIM)
        o_a = fox_mixer(fq.reshape(fshape), fk.reshape(fshape), fv.reshape(fshape), ff,
                        fox_f_bias[l], fox_q_norm[l], fox_k_norm[l])
        o_b = gmlp_mixer(gu, gv, gmlp_v_norm[l], gmlp_w_s[l], gmlp_b_s[l])
        o_c = nsa_mixer(nq, nkc, nvc, nks, nvs, nkw, nvw, ng, nsa_gate_bias[l],
                        nsa_q_norm[l], nsa_kc_norm[l], nsa_ks_norm[l], nsa_kw_norm[l],
                        nsa_cmp_pos_k[l], nsa_cmp_k_w1[l], nsa_cmp_k_w2[l],
                        nsa_cmp_pos_v[l], nsa_cmp_v_w1[l], nsa_cmp_v_w2[l])
        o_d = pool_mixer(pz, pool_w[l], pool_scale[l])
        x = x + jnp.concatenate([o_a, o_b, o_c, o_d], axis=-1) @ w_out[l]
        x = x + FFN_RES_WEIGHT * swiglu_ffn(x, ffn2_norm[l], ffn2_w1[l], ffn2_w3[l], ffn2_w2[l])
    return x
```

```python
import functools

import numpy as np
import jax
import jax.numpy as jnp
from jax import lax
from jax.experimental import pallas as pl
from jax.experimental.pallas import tpu as pltpu

F32 = jnp.float32
BF16 = jnp.bfloat16
HIGHEST = lax.Precision.HIGHEST

HEAD_DIM = 64
N_HEADS = 4
W_MIX = N_HEADS * HEAD_DIM
ROPE_THETA = 500000.0
ROPE_DIM = HEAD_DIM // 4
ROPE_HALF = ROPE_DIM // 2
GMLP_CHUNK = 128
CMP_LEN = 32
CMP_STRIDE = 16
SEL_LEN = 64
SEL_TOP = 16
WINDOW = 512
POOL_SIZES = (2, 4, 8, 16)
FFN_RES_WEIGHT = 0.5
EPS = 1e-6
NEG_INF = -1e30
SEL_FORCE = 1e3
ATTN_SCALE = HEAD_DIM ** -0.5

LANES = 128
VMEM_LIMIT = 52 * 1024 * 1024

FFN_TOKENS = 1024
FFN_HIDDEN = 256
ATTN_TILE = 128


def _params(*sem):
    return pltpu.CompilerParams(dimension_semantics=sem, vmem_limit_bytes=VMEM_LIMIT)


def _dot(a, b, **kw):
    return jnp.dot(a, b, preferred_element_type=F32, **kw)


def _dot_nt(a, b):
    return lax.dot_general(a, b, (((1,), (1,)), ((), ())), preferred_element_type=F32)


def _rms(x):
    return x * lax.rsqrt(jnp.mean(x * x, axis=-1, keepdims=True) + EPS)


def _group_rms(x, gmean):
    ms = _dot(x * x, gmean, precision=HIGHEST)
    return x * lax.rsqrt(ms + EPS)


def _rope(x, c, s_up, s_dn):
    w = x.shape[-1]
    return x * c + pltpu.roll(x, w - ROPE_HALF, axis=1) * s_up + pltpu.roll(x, ROPE_HALF, axis=1) * s_dn


def _gelu(x):
    return 0.5 * x * (1.0 + jnp.tanh(0.7978845608028654 * (x + 0.044715 * (x * x * x))))


def _log_sigmoid(x):
    return jnp.minimum(x, 0.0) - jnp.log1p(jnp.exp(-jnp.abs(x)))


def _ffn_kernel(x_ref, g_ref, w1_ref, w3_ref, w2_ref, o_ref, h_ref, acc_ref):
    j = pl.program_id(1)

    @pl.when(j == 0)
    def _():
        h_ref[...] = (_rms(x_ref[...]) * g_ref[...]).astype(BF16)
        acc_ref[...] = jnp.zeros_like(acc_ref)

    h = h_ref[...]
    a = _dot(h, w1_ref[...])
    b = _dot(h, w3_ref[...])
    gated = (a * jax.nn.sigmoid(a) * b).astype(BF16)
    acc_ref[...] += _dot(gated, w2_ref[...])

    @pl.when(j == pl.num_programs(1) - 1)
    def _():
        o_ref[...] = x_ref[...] + FFN_RES_WEIGHT * acc_ref[...]


def _ffn(x, g, w1, w3, w2):
    n, d = x.shape
    dff = w1.shape[1]
    tm = min(FFN_TOKENS, n)
    tf = FFN_HIDDEN
    return pl.pallas_call(
        _ffn_kernel,
        grid=(n // tm, dff // tf),
        in_specs=[
            pl.BlockSpec((tm, d), lambda i, j: (i, 0)),
            pl.BlockSpec((1, d), lambda i, j: (0, 0)),
            pl.BlockSpec((d, tf), lambda i, j: (0, j)),
            pl.BlockSpec((d, tf), lambda i, j: (0, j)),
            pl.BlockSpec((tf, d), lambda i, j: (j, 0)),
        ],
        out_specs=pl.BlockSpec((tm, d), lambda i, j: (i, 0)),
        out_shape=jax.ShapeDtypeStruct((n, d), F32),
        scratch_shapes=[pltpu.VMEM((tm, d), BF16), pltpu.VMEM((tm, d), F32)],
        compiler_params=_params("parallel", "arbitrary"),
        name="ffn",
    )(x, g.reshape(1, d), w1.astype(BF16), w3.astype(BF16), w2.astype(BF16))


_IN_GROUPS = (768, 512, 256, 256, 256, 128, 128)
_MISC_FORGET = 0
_MISC_GATE = 4


def _inproj_kernel(x_ref, g_ref, w_ref, fqkv_ref, guv_ref, nq_ref, nkv_ref, pz_ref, kc_ref, vc_ref, misc_ref):
    h = (_rms(x_ref[...]) * g_ref[...]).astype(BF16)
    off = 0
    for ref, width in zip((fqkv_ref, guv_ref, nq_ref, nkv_ref, pz_ref), _IN_GROUPS[:5]):
        ref[...] = _dot(h, w_ref[:, off:off + width])
        off += width
    kcvc = _dot(h, w_ref[:, off:off + LANES])
    kc_ref[...] = kcvc[:, :HEAD_DIM]
    vc_ref[...] = kcvc[:, HEAD_DIM:]
    off += LANES
    misc_ref[...] = _dot(h, w_ref[:, off:off + LANES])


def _relayout_w_in(w_in):
    o = np.cumsum((0, 256, 256, 256, 4, 256, 256, 256, 64, 64, 64, 64, 64, 64, 12, 256))
    fq, ff, gu, nq, nkc, nks, ng, pz, end = o[0], o[3], o[4], o[6], o[7], o[9], o[13], o[14], o[15]
    d = w_in.shape[0]
    pad = jnp.zeros((d, LANES - 16), w_in.dtype)
    return jnp.concatenate([
        w_in[:, fq:ff], w_in[:, gu:nq], w_in[:, nq:nkc], w_in[:, nks:ng], w_in[:, pz:end],
        w_in[:, nkc:nks], w_in[:, ff:gu], w_in[:, ng:pz], pad], axis=1)


def _inproj(x, g, w_in):
    n, d = x.shape
    tm = min(FFN_TOKENS, n)
    w = _relayout_w_in(w_in).astype(BF16)
    widths = (768, 512, 256, 256, 256, HEAD_DIM, HEAD_DIM, LANES)
    return pl.pallas_call(
        _inproj_kernel,
        grid=(n // tm,),
        in_specs=[
            pl.BlockSpec((tm, d), lambda i: (i, 0)),
            pl.BlockSpec((1, d), lambda i: (0, 0)),
            pl.BlockSpec(w.shape, lambda i: (0, 0)),
        ],
        out_specs=[pl.BlockSpec((tm, wd), lambda i: (i, 0)) for wd in widths],
        out_shape=[jax.ShapeDtypeStruct((n, wd), F32) for wd in widths],
        compiler_params=_params("parallel"),
        name="inproj",
    )(x, g.reshape(1, d), w)


def _outproj_kernel(x_ref, oa_ref, ob_ref, oc_ref, od_ref, w_ref, o_ref):
    acc = x_ref[...]
    for k, ref in enumerate((oa_ref, ob_ref, oc_ref, od_ref)):
        acc = acc + _dot(ref[...].astype(BF16), w_ref[k * W_MIX:(k + 1) * W_MIX, :])
    o_ref[...] = acc


def _outproj(x, o_a, o_b, o_c, o_d, w_out):
    n, d = x.shape
    tm = min(FFN_TOKENS, n)
    mix = pl.BlockSpec((tm, W_MIX), lambda i: (i, 0))
    return pl.pallas_call(
        _outproj_kernel,
        grid=(n // tm,),
        in_specs=[pl.BlockSpec((tm, d), lambda i: (i, 0)), mix, mix, mix, mix,
                  pl.BlockSpec(w_out.shape, lambda i: (0, 0))],
        out_specs=pl.BlockSpec((tm, d), lambda i: (i, 0)),
        out_shape=jax.ShapeDtypeStruct((n, d), F32),
        compiler_params=_params("parallel"),
        name="outproj",
    )(x, o_a, o_b, o_c, o_d, w_out.astype(BF16))


def _softmax_step(s, v, m, l, acc):
    m_new = jnp.maximum(m, jnp.max(s, axis=-1, keepdims=True))
    alpha = jnp.exp(m - m_new)
    p = jnp.exp(s - m_new)
    l = alpha * l + jnp.sum(p, axis=-1, keepdims=True)
    acc = alpha * acc + _dot(p.astype(BF16), v)
    return m_new, l, acc


def _softmax_init(rows):
    return (jnp.full((rows, 1), NEG_INF, F32), jnp.zeros((rows, 1), F32), jnp.zeros((rows, HEAD_DIM), F32))


def _fox_kernel(qkv_ref, misc_ref, fb_ref, gq_ref, gk_ref, gmean_ref, o_ref, q_s, k_s, v_s, ccol_s, crow_s):
    t = qkv_ref.shape[0]
    tile = ATTN_TILE
    nt = t // tile
    gmean = gmean_ref[...]
    rows = lax.broadcasted_iota(jnp.int32, (tile, tile), 0)
    cols = lax.broadcasted_iota(jnp.int32, (tile, tile), 1)
    causal = cols <= rows
    tri = causal.astype(F32)

    carry = jnp.zeros((1, LANES), F32)
    for r in range(nt):
        sl = slice(r * tile, (r + 1) * tile)
        q_s[sl, :] = (_group_rms(qkv_ref[sl, 0:W_MIX], gmean) * gq_ref[...] * ATTN_SCALE).astype(BF16)
        k_s[sl, :] = (_group_rms(qkv_ref[sl, W_MIX:2 * W_MIX], gmean) * gk_ref[...]).astype(BF16)
        v_s[sl, :] = qkv_ref[sl, 2 * W_MIX:3 * W_MIX].astype(BF16)
        log_f = _log_sigmoid(misc_ref[sl, :] + fb_ref[...])
        c = _dot(tri, log_f, precision=HIGHEST) + carry
        carry = c[tile - 1:tile, :]
        ccol_s[sl, :] = c
        crow_s[r] = c.T[0:8, :]

    for h in range(N_HEADS):
        hs = slice(h * HEAD_DIM, (h + 1) * HEAD_DIM)

        def q_body(qi, _, h=h, hs=hs):
            qsl = pl.ds(pl.multiple_of(qi * tile, tile), tile)
            q = q_s[qsl, hs]
            cq = ccol_s[qsl, h:h + 1]

            def scores(kj):
                ksl = pl.ds(pl.multiple_of(kj * tile, tile), tile)
                s = _dot_nt(q, k_s[ksl, hs]) + cq - crow_s[kj, h:h + 1, :]
                return s, v_s[ksl, hs]

            def k_body(kj, st):
                s, v = scores(kj)
                return _softmax_step(s, v, *st)

            st = lax.fori_loop(0, qi, k_body, _softmax_init(tile))
            s, v = scores(qi)
            m, l, acc = _softmax_step(jnp.where(causal, s, NEG_INF), v, *st)
            o_ref[qsl, hs] = acc / l
            return 0

        lax.fori_loop(0, nt, q_body, 0)


def _group_mean_matrix(width):
    g = np.kron(np.eye(width // HEAD_DIM, dtype=np.float32), np.full((HEAD_DIM, HEAD_DIM), 1.0 / HEAD_DIM, np.float32))
    return jnp.asarray(g)


def _fox(fqkv, misc, f_bias, g_q, g_k, b, t):
    fb = jnp.zeros((1, LANES), F32).at[0, _MISC_FORGET:_MISC_FORGET + N_HEADS].set(f_bias)
    gq = jnp.tile(g_q, N_HEADS).reshape(1, W_MIX)
    gk = jnp.tile(g_k, N_HEADS).reshape(1, W_MIX)
    const = lambda shape: pl.BlockSpec(shape, lambda i: (0,) * len(shape))
    return pl.pallas_call(
        _fox_kernel,
        grid=(b,),
        in_specs=[pl.BlockSpec((t, 3 * W_MIX), lambda i: (i, 0)), pl.BlockSpec((t, LANES), lambda i: (i, 0)),
                  const((1, LANES)), const((1, W_MIX)), const((1, W_MIX)), const((W_MIX, W_MIX))],
        out_specs=pl.BlockSpec((t, W_MIX), lambda i: (i, 0)),
        out_shape=jax.ShapeDtypeStruct((b * t, W_MIX), F32),
        scratch_shapes=[pltpu.VMEM((t, W_MIX), BF16), pltpu.VMEM((t, W_MIX), BF16), pltpu.VMEM((t, W_MIX), BF16),
                        pltpu.VMEM((t, LANES), F32), pltpu.VMEM((t // ATTN_TILE, 8, ATTN_TILE), F32)],
        compiler_params=_params("parallel"),
        name="fox",
    )(fqkv, misc, fb, gq, gk, _group_mean_matrix(W_MIX))


def _gmlp_kernel(uv_ref, gv_ref, gmean_ref, w_ref, bias_ref, o_ref):
    c = GMLP_CHUNK
    rows = lax.broadcasted_iota(jnp.int32, (c, c), 0)
    cols = lax.broadcasted_iota(jnp.int32, (c, c), 1)
    lane_group = lax.broadcasted_iota(jnp.int32, (c, W_MIX), 1) // HEAD_DIM
    w_tril = [jnp.where(cols <= rows, w_ref[g], 0.0).astype(BF16) for g in range(N_HEADS)]
    for r in range(uv_ref.shape[0] // c):
        sl = slice(r * c, (r + 1) * c)
        u = _gelu(uv_ref[sl, 0:W_MIX])
        v = _group_rms(_gelu(uv_ref[sl, W_MIX:2 * W_MIX]), gmean_ref[...]) * gv_ref[...]
        s = bias_ref[...]
        for g in range(N_HEADS):
            s = s + _dot(w_tril[g], jnp.where(lane_group == g, v, 0.0).astype(BF16))
        o_ref[sl, :] = u * s


def _gmlp(guv, g_v, w_s, b_s, n):
    tt = min(1024, n)
    bias = jnp.repeat(b_s.T, HEAD_DIM, axis=1)
    const = lambda shape: pl.BlockSpec(shape, lambda i: (0,) * len(shape))
    return pl.pallas_call(
        _gmlp_kernel,
        grid=(n // tt,),
        in_specs=[pl.BlockSpec((tt, 2 * W_MIX), lambda i: (i, 0)), const((1, W_MIX)), const((W_MIX, W_MIX)),
                  const(w_s.shape), const(bias.shape)],
        out_specs=pl.BlockSpec((tt, W_MIX), lambda i: (i, 0)),
        out_shape=jax.ShapeDtypeStruct((n, W_MIX), F32),
        compiler_params=_params("parallel"),
        name="gmlp",
    )(guv, g_v.reshape(1, W_MIX), _group_mean_matrix(W_MIX), w_s, bias)


def _pool_kernel(z_ref, w_ref, scale_ref, o_ref):
    t = z_ref.shape[0]
    z = z_ref[...]
    row = lax.broadcasted_iota(jnp.int32, (t, W_MIX), 0)
    lane_group = lax.broadcasted_iota(jnp.int32, (t, W_MIX), 1) // HEAD_DIM
    sums = []
    s = z
    k = 1
    while k < max(POOL_SIZES):
        s = s + jnp.where(row >= k, pltpu.roll(s, k, axis=0), 0.0)
        k *= 2
        sums.append((k, s))
    win_sum = jnp.zeros_like(z)
    win = jnp.zeros((t, W_MIX), jnp.int32)
    for g, size in enumerate(POOL_SIZES):
        win_sum = jnp.where(lane_group == g, dict(sums)[size], win_sum)
        win = jnp.where(lane_group == g, size, win)
    cnt = jnp.minimum(row + 1, win).astype(F32)
    pooled = win_sum / cnt - z
    o_ref[...] = _dot(pooled.astype(BF16), w_ref[...]) * scale_ref[...]


def _pool(pz, w_p, scale, b, t):
    w_bd = jax.scipy.linalg.block_diag(*[w_p[g] for g in range(N_HEADS)]).astype(BF16)
    const = lambda shape: pl.BlockSpec(shape, lambda i: (0,) * len(shape))
    return pl.pallas_call(
        _pool_kernel,
        grid=(b,),
        in_specs=[pl.BlockSpec((t, W_MIX), lambda i: (i, 0)), const((W_MIX, W_MIX)), const((1, W_MIX))],
        out_specs=pl.BlockSpec((t, W_MIX), lambda i: (i, 0)),
        out_shape=jax.ShapeDtypeStruct((b * t, W_MIX), F32),
        compiler_params=_params("parallel"),
        name="pool",
    )(pz, w_bd, scale.reshape(1, W_MIX))


def _nsa_prep_kernel(kv_ref, kc_ref, vc_ref, posk_ref, posv_ref, kw1_ref, vw1_ref, kw2_ref, vw2_ref,
                     gkv_ref, nmask_ref, gmean_ref, c_ref, su_ref, sd_ref, gc_ref, cc_ref, csu_ref, csd_ref,
                     ks_ref, vs_ref, kw_ref, vw_ref, kcmp_ref, vcmp_ref):
    x = kv_ref[...]
    normed = _group_rms(x, gmean_ref[...]) * gkv_ref[...]
    y = _rope(jnp.where(nmask_ref[...] > 0.5, normed, x), c_ref[...], su_ref[...], sd_ref[...]).astype(BF16)
    for k, ref in enumerate((ks_ref, vs_ref, kw_ref, vw_ref)):
        ref[...] = y[:, k * HEAD_DIM:(k + 1) * HEAD_DIM]

    half = kc_ref.shape[1]
    nb = kc_ref.shape[0]

    def hidden(x_ref, pos_ref, w1_ref):
        top = _dot((x_ref[...] + pos_ref[:, 0:half]).astype(BF16), w1_ref[0:half, :])
        bot = _dot((x_ref[...] + pos_ref[:, half:2 * half]).astype(BF16), w1_ref[half:2 * half, :])
        return _gelu(top + pltpu.roll(bot, nb - 1, axis=0)).astype(BF16)

    kv_cmp = _dot(hidden(kc_ref, posk_ref, kw1_ref), kw2_ref[...]) + _dot(hidden(vc_ref, posv_ref, vw1_ref), vw2_ref[...])
    lane = lax.broadcasted_iota(jnp.int32, kv_cmp.shape, 1)
    normed = _group_rms(kv_cmp, gmean_ref[0:LANES, 0:LANES]) * gc_ref[...]
    y = _rope(jnp.where(lane < HEAD_DIM, normed, kv_cmp), cc_ref[...], csu_ref[...], csd_ref[...]).astype(BF16)
    kcmp_ref[...] = y[:, :HEAD_DIM]
    vcmp_ref[...] = y[:, HEAD_DIM:]


def _rope_tables(pos):
    inv = ROPE_THETA ** (-jnp.arange(ROPE_HALF, dtype=F32) * 2.0 / ROPE_DIM)
    ang = pos.astype(F32)[:, None] * inv[None, :]
    cos, sin = jnp.cos(ang), jnp.sin(ang)
    n = pos.shape[0]
    zero = jnp.zeros((n, ROPE_HALF), F32)
    rest0 = jnp.zeros((n, HEAD_DIM - ROPE_DIM), F32)
    c = jnp.concatenate([cos, cos, rest0 + 1.0], axis=1)
    s_up = jnp.concatenate([-sin, zero, rest0], axis=1)
    s_dn = jnp.concatenate([zero, sin, rest0], axis=1)
    return c, s_up, s_dn


def _identity_tables(n):
    return jnp.ones((n, HEAD_DIM), F32), jnp.zeros((n, HEAD_DIM), F32), jnp.zeros((n, HEAD_DIM), F32)


def _nsa_prep(nkv, kc, vc, g_kc, g_ks, g_kw, pos_k, k_w1, k_w2, pos_v, v_w1, v_w2, b, t):
    nb = t // CMP_STRIDE
    flat = CMP_STRIDE * HEAD_DIM
    kc2 = kc.reshape(b * nb, flat)
    vc2 = vc.reshape(b * nb, flat)
    rope_t = _rope_tables(jnp.arange(t))
    iden_t = _identity_tables(t)
    tabs = [jnp.concatenate([r, i, r, i], axis=1) for r, i in zip(rope_t, iden_t)]
    rope_c = _rope_tables(jnp.arange(nb) * CMP_STRIDE + CMP_LEN - 1)
    iden_c = _identity_tables(nb)
    tabs_c = [jnp.concatenate([r, i], axis=1) for r, i in zip(rope_c, iden_c)]
    one = jnp.ones((HEAD_DIM,), F32)
    gkv = jnp.concatenate([g_ks, one, g_kw, one]).reshape(1, W_MIX)
    nmask = jnp.concatenate([one, 0 * one, one, 0 * one]).reshape(1, W_MIX)
    gc = jnp.concatenate([g_kc, one]).reshape(1, LANES)
    zpad = jnp.zeros_like(k_w2)
    kw2 = jnp.concatenate([k_w2, zpad], axis=1).astype(BF16)
    vw2 = jnp.concatenate([zpad, v_w2], axis=1).astype(BF16)
    const = lambda shape: pl.BlockSpec(shape, lambda i: (0,) * len(shape))
    seq = lambda width: pl.BlockSpec((t, width), lambda i: (i, 0))
    cmp_in = pl.BlockSpec((nb, flat), lambda i: (i, 0))
    cmp_out = pl.BlockSpec((nb, HEAD_DIM), lambda i: (i, 0))
    return pl.pallas_call(
        _nsa_prep_kernel,
        grid=(b,),
        in_specs=[seq(W_MIX), cmp_in, cmp_in, const((1, 2 * flat)), const((1, 2 * flat)),
                  const(k_w1.shape), const(v_w1.shape), const(kw2.shape), const(vw2.shape),
                  const((1, W_MIX)), const((1, W_MIX)), const((W_MIX, W_MIX)),
                  const((t, W_MIX)), const((t, W_MIX)), const((t, W_MIX)),
                  const((1, LANES)), const((nb, LANES)), const((nb, LANES)), const((nb, LANES))],
        out_specs=[seq(HEAD_DIM)] * 4 + [cmp_out] * 2,
        out_shape=[jax.ShapeDtypeStruct((b * t, HEAD_DIM), BF16)] * 4 + [jax.ShapeDtypeStruct((b * nb, HEAD_DIM), BF16)] * 2,
        compiler_params=_params("parallel"),
        name="nsa_prep",
    )(nkv, kc2, vc2, pos_k.reshape(1, 2 * flat), pos_v.reshape(1, 2 * flat),
      k_w1.astype(BF16), v_w1.astype(BF16), kw2, vw2, gkv, nmask, _group_mean_matrix(W_MIX),
      *tabs, gc, *tabs_c)


def _nsa_kernel(q_ref, misc_ref, gb_ref, gq_ref, gmean_ref, c_ref, su_ref, sd_ref, ovt_ref,
                kcmp_ref, vcmp_ref, ks_ref, vs_ref, kw_ref, vw_ref, o_ref, *, n_top):
    tile = ATTN_TILE
    qi = pl.program_id(1)
    t0 = qi * tile
    nb = kcmp_ref.shape[0]

    qn = _rope(_group_rms(q_ref[...], gmean_ref[...]) * gq_ref[...], c_ref[...], su_ref[...], sd_ref[...])
    qn = (qn * ATTN_SCALE).astype(BF16)
    q = jnp.concatenate([qn[:, h * HEAD_DIM:(h + 1) * HEAD_DIM] for h in range(N_HEADS)], axis=0)
    rows = N_HEADS * tile

    trow = lax.broadcasted_iota(jnp.int32, (rows, tile), 0) % tile
    tcol = lax.broadcasted_iota(jnp.int32, (rows, tile), 1)

    cmp_end = lax.broadcasted_iota(jnp.int32, (rows, nb), 1) * CMP_STRIDE + (CMP_LEN - 1)
    qpos_c = lax.broadcasted_iota(jnp.int32, (rows, nb), 0) % tile + t0
    m_cmp = cmp_end <= qpos_c
    s = jnp.where(m_cmp, _dot_nt(q, kcmp_ref[...]), NEG_INF)
    e = jnp.where(m_cmp, jnp.exp(s - jnp.max(s, axis=-1, keepdims=True)), 0.0)
    denom = jnp.sum(e, axis=-1, keepdims=True)
    p_cmp = e / jnp.where(denom > 0.0, denom, 1.0)
    o_cmp = _dot(p_cmp.astype(BF16), vcmp_ref[...])

    p_sum = p_cmp[0:tile] + p_cmp[tile:2 * tile] + p_cmp[2 * tile:3 * tile] + p_cmp[3 * tile:4 * tile]
    imp = lax.dot_general(ovt_ref[...], p_sum, (((1,), (1,)), ((), ())), precision=HIGHEST,
                          preferred_element_type=F32)
    blk = lax.broadcasted_iota(jnp.int32, (LANES, tile), 0)
    cur = (lax.broadcasted_iota(jnp.int32, (LANES, tile), 1) + t0) // SEL_LEN
    valid = blk <= cur
    forced = ((blk == 0) | (blk == cur) | (blk == cur - 1)).astype(F32)
    imp = jnp.where(valid, imp + SEL_FORCE * forced, NEG_INF)
    sel_t = jnp.zeros((LANES, tile), F32)
    nsel = ks_ref.shape[0] // SEL_LEN
    cur_row = (lax.broadcasted_iota(jnp.int32, (1, tile), 1) + t0) // SEL_LEN
    for j in range(nsel):
        mine = imp[j:j + 1, :]
        beats = (imp > mine) | ((imp == mine) & (blk < j))
        count = jnp.sum(jnp.where(beats & valid, 1.0, 0.0), axis=0, keepdims=True)
        keep = (count < n_top) & (cur_row >= j)
        sel_t = jnp.where(blk == j, jnp.where(keep, 1.0, 0.0), sel_t)
    sel = sel_t.T.astype(BF16)

    def sel_scores(kj):
        ksl = pl.ds(pl.multiple_of(kj * tile, tile), tile)
        expand = (lax.broadcasted_iota(jnp.int32, (LANES, tile), 0)
                  == (lax.broadcasted_iota(jnp.int32, (LANES, tile), 1) + kj * tile) // SEL_LEN)
        picked = _dot(sel, jnp.where(expand, 1.0, 0.0).astype(BF16))
        picked = jnp.concatenate([picked] * N_HEADS, axis=0) > 0.5
        return _dot_nt(q, ks_ref[ksl, :]), picked, vs_ref[ksl, :]

    def sel_body(kj, st):
        s, picked, v = sel_scores(kj)
        return _softmax_step(jnp.where(picked, s, NEG_INF), v, *st)

    st = lax.fori_loop(0, qi, sel_body, _softmax_init(rows))
    s, picked, v = sel_scores(qi)
    _, l, acc = _softmax_step(jnp.where(picked & (tcol <= trow), s, NEG_INF), v, *st)
    o_sel = acc / l

    def win_scores(kj):
        ksl = pl.ds(pl.multiple_of(kj * tile, tile), tile)
        return _dot_nt(q, kw_ref[ksl, :]), vw_ref[ksl, :]

    span = WINDOW // tile
    st = _softmax_init(rows)
    first = qi - span

    def win_first(st):
        s, v = win_scores(first)
        return _softmax_step(jnp.where(trow - tcol + span * tile < WINDOW, s, NEG_INF), v, *st)

    st = lax.cond(first >= 0, win_first, lambda st: st, st)

    def win_body(kj, st):
        s, v = win_scores(kj)
        return _softmax_step(s, v, *st)

    st = lax.fori_loop(jnp.maximum(first + 1, 0), qi, win_body, st)
    s, v = win_scores(qi)
    _, l, acc = _softmax_step(jnp.where(tcol <= trow, s, NEG_INF), v, *st)
    o_win = acc / l

    gate = jax.nn.sigmoid(misc_ref[...] + gb_ref[...])
    for h in range(N_HEADS):
        hr = slice(h * tile, (h + 1) * tile)
        g0 = _MISC_GATE + 3 * h
        o_ref[:, h * HEAD_DIM:(h + 1) * HEAD_DIM] = (
            gate[:, g0:g0 + 1] * o_cmp[hr] + gate[:, g0 + 1:g0 + 2] * o_sel[hr] + gate[:, g0 + 2:g0 + 3] * o_win[hr])


def _cmp_to_sel_overlap_t(nb, nsel):
    nc = nb - 1
    cs = np.arange(nc) * CMP_STRIDE
    ss = np.arange(nsel) * SEL_LEN
    ov = np.clip(np.minimum(cs[:, None] + CMP_LEN, ss[None, :] + SEL_LEN) - np.maximum(cs[:, None], ss[None, :]), 0, None)
    out = np.zeros((LANES, nb), np.float32)
    out[:nsel, :nc] = (ov / CMP_LEN).T
    return jnp.asarray(out)


def _nsa(nq, misc, gate_b, g_q, ks, vs, kw, vw, kcmp, vcmp, b, t):
    tile = ATTN_TILE
    nb = t // CMP_STRIDE
    nsel = t // SEL_LEN
    assert nsel <= LANES and nb % 8 == 0 and WINDOW % tile == 0
    gb = jnp.zeros((1, LANES), F32).at[0, _MISC_GATE:_MISC_GATE + 3 * N_HEADS].set(gate_b)
    gq = jnp.tile(g_q, N_HEADS).reshape(1, W_MIX)
    tabs = [jnp.tile(x, (1, N_HEADS)) for x in _rope_tables(jnp.arange(t))]
    nq_tiles = t // tile
    const = lambda shape: pl.BlockSpec(shape, lambda i, j: (0,) * len(shape))
    qtile = lambda width: pl.BlockSpec((tile, width), lambda i, j: (i * nq_tiles + j, 0))
    ptile = pl.BlockSpec((tile, W_MIX), lambda i, j: (j, 0))
    seq = pl.BlockSpec((t, HEAD_DIM), lambda i, j: (i, 0))
    cmp = pl.BlockSpec((nb, HEAD_DIM), lambda i, j: (i, 0))
    return pl.pallas_call(
        functools.partial(_nsa_kernel, n_top=min(SEL_TOP, nsel)),
        grid=(b, nq_tiles),
        in_specs=[qtile(W_MIX), qtile(LANES), const((1, LANES)), const((1, W_MIX)), const((W_MIX, W_MIX)),
                  ptile, ptile, ptile, const((LANES, nb)), cmp, cmp, seq, seq, seq, seq],
        out_specs=qtile(W_MIX),
        out_shape=jax.ShapeDtypeStruct((b * t, W_MIX), F32),
        compiler_params=_params("parallel", "arbitrary"),
        name="nsa",
    )(nq, misc, gb, gq, _group_mean_matrix(W_MIX), *tabs, _cmp_to_sel_overlap_t(nb, nsel),
      kcmp, vcmp, ks, vs, kw, vw)


def kernel(x, ffn1_norm, ffn1_w1, ffn1_w3, ffn1_w2, mix_norm, w_in, w_out, fox_f_bias, fox_q_norm, fox_k_norm, gmlp_v_norm, gmlp_w_s, gmlp_b_s, nsa_q_norm, nsa_kc_norm, nsa_ks_norm, nsa_kw_norm, nsa_cmp_pos_k, nsa_cmp_k_w1, nsa_cmp_k_w2, nsa_cmp_pos_v, nsa_cmp_v_w1, nsa_cmp_v_w2, nsa_gate_bias, pool_w, pool_scale, ffn2_norm, ffn2_w1, ffn2_w3, ffn2_w2):
    b, t, d = x.shape
    assert t % ATTN_TILE == 0 and t % GMLP_CHUNK == 0 and t >= CMP_LEN
    n = b * t
    xf = x.reshape(n, d)
    for l in range(ffn1_norm.shape[0]):
        xf = _ffn(xf, ffn1_norm[l], ffn1_w1[l], ffn1_w3[l], ffn1_w2[l])
        fqkv, guv, nq, nkv, pz, kc, vc, misc = _inproj(xf, mix_norm[l], w_in[l])
        o_a = _fox(fqkv, misc, fox_f_bias[l], fox_q_norm[l], fox_k_norm[l], b, t)
        o_b = _gmlp(guv, gmlp_v_norm[l], gmlp_w_s[l], gmlp_b_s[l], n)
        ks, vs, kw, vw, kcmp, vcmp = _nsa_prep(
            nkv, kc, vc, nsa_kc_norm[l], nsa_ks_norm[l], nsa_kw_norm[l],
            nsa_cmp_pos_k[l], nsa_cmp_k_w1[l], nsa_cmp_k_w2[l],
            nsa_cmp_pos_v[l], nsa_cmp_v_w1[l], nsa_cmp_v_w2[l], b, t)
        o_c = _nsa(nq, misc, nsa_gate_bias[l], nsa_q_norm[l], ks, vs, kw, vw, kcmp, vcmp, b, t)
        o_d = _pool(pz, pool_w[l], pool_scale[l], b, t)
        xf = _outproj(xf, o_a, o_b, o_c, o_d, w_out[l])
        xf = _ffn(xf, ffn2_norm[l], ffn2_w1[l], ffn2_w3[l], ffn2_w2[l])
    return xf.reshape(b, t, d)
```

```python
import functools

import numpy as np
import jax
import jax.numpy as jnp
from jax import lax
from jax.experimental import pallas as pl
from jax.experimental.pallas import tpu as pltpu

F32 = jnp.float32
BF16 = jnp.bfloat16
HIGHEST = lax.Precision.HIGHEST

HEAD_DIM = 64
N_HEADS = 4
W_MIX = N_HEADS * HEAD_DIM
ROPE_THETA = 500000.0
ROPE_DIM = HEAD_DIM // 4
ROPE_HALF = ROPE_DIM // 2
GMLP_CHUNK = 128
CMP_LEN = 32
CMP_STRIDE = 16
SEL_LEN = 64
SEL_TOP = 16
WINDOW = 512
POOL_SIZES = (2, 4, 8, 16)
FFN_RES_WEIGHT = 0.5
EPS = 1e-6
NEG_INF = -1e30
SEL_FORCE = 1e3
ATTN_SCALE = HEAD_DIM ** -0.5

LANES = 128
VMEM_LIMIT = 52 * 1024 * 1024

FFN_TOKENS = 1024
FFN_HIDDEN = 256
FOX_PREP_ROWS = 256
FOX_TILE = 512
NSA_TILE = 256
ONE_LANE = HEAD_DIM


def _params(*sem):
    return pltpu.CompilerParams(dimension_semantics=sem, vmem_limit_bytes=VMEM_LIMIT)


def _dot(a, b, **kw):
    return jnp.dot(a, b, preferred_element_type=F32, **kw)


def _dot_nt(a, b):
    return lax.dot_general(a, b, (((1,), (1,)), ((), ())), preferred_element_type=F32)


def _rms(x):
    return x * lax.rsqrt(jnp.mean(x * x, axis=-1, keepdims=True) + EPS)


def _group_rms(x, gmean):
    sq = x * x
    hi = sq.astype(BF16)
    lo = (sq - hi.astype(F32)).astype(BF16)
    ms = _dot(hi, gmean) + _dot(lo, gmean)
    return x * lax.rsqrt(ms + EPS)


def _rope(x, c, s_up, s_dn):
    w = x.shape[-1]
    return x * c + pltpu.roll(x, w - ROPE_HALF, axis=1) * s_up + pltpu.roll(x, ROPE_HALF, axis=1) * s_dn


def _gelu(x):
    return 0.5 * x * (1.0 + jnp.tanh(0.7978845608028654 * (x + 0.044715 * (x * x * x))))


def _log_sigmoid(x):
    return jnp.minimum(x, 0.0) - jnp.log1p(jnp.exp(-jnp.abs(x)))


def _head_column(x, h):
    col = x[:, (h // 2) * LANES:(h // 2 + 1) * LANES]
    return pltpu.roll(col, HEAD_DIM, axis=1) if h % 2 else col


def _pair_heads(even, odd):
    lane = lax.broadcasted_iota(jnp.int32, even.shape, 1)
    return jnp.where(lane < HEAD_DIM, even, pltpu.roll(odd, HEAD_DIM, axis=1))


def _ones_lane(shape):
    return jnp.where(lax.broadcasted_iota(jnp.int32, shape, 1) == ONE_LANE, 1.0, 0.0)


def _ffn_kernel(x_ref, g_ref, w1_ref, w3_ref, w2_ref, o_ref, h_ref, acc_ref):
    j = pl.program_id(1)

    @pl.when(j == 0)
    def _():
        h_ref[...] = (_rms(x_ref[...]) * g_ref[...]).astype(BF16)
        acc_ref[...] = jnp.zeros_like(acc_ref)

    h = h_ref[...]
    a = _dot(h, w1_ref[...])
    b = _dot(h, w3_ref[...])
    gated = (a * jax.nn.sigmoid(a) * b).astype(BF16)
    acc_ref[...] += _dot(gated, w2_ref[...])

    @pl.when(j == pl.num_programs(1) - 1)
    def _():
        o_ref[...] = x_ref[...] + FFN_RES_WEIGHT * acc_ref[...]


def _ffn(x, g, w1, w3, w2):
    n, d = x.shape
    dff = w1.shape[1]
    tm = min(FFN_TOKENS, n)
    tf = FFN_HIDDEN
    return pl.pallas_call(
        _ffn_kernel,
        grid=(n // tm, dff // tf),
        in_specs=[
            pl.BlockSpec((tm, d), lambda i, j: (i, 0)),
            pl.BlockSpec((1, d), lambda i, j: (0, 0)),
            pl.BlockSpec((d, tf), lambda i, j: (0, j)),
            pl.BlockSpec((d, tf), lambda i, j: (0, j)),
            pl.BlockSpec((tf, d), lambda i, j: (j, 0)),
        ],
        out_specs=pl.BlockSpec((tm, d), lambda i, j: (i, 0)),
        out_shape=jax.ShapeDtypeStruct((n, d), F32),
        scratch_shapes=[pltpu.VMEM((tm, d), BF16), pltpu.VMEM((tm, d), F32)],
        compiler_params=_params("parallel", "arbitrary"),
        name="ffn",
    )(x, g.reshape(1, d), w1.astype(BF16), w3.astype(BF16), w2.astype(BF16))


_IN_GROUPS = (768, 512, 256, 256, 256, 128, 128)
_MISC_FORGET = 0
_MISC_GATE = 4


def _inproj_kernel(x_ref, g_ref, w_ref, fqkv_ref, guv_ref, nq_ref, nkv_ref, pz_ref, kc_ref, vc_ref, misc_ref):
    h = (_rms(x_ref[...]) * g_ref[...]).astype(BF16)
    off = 0
    for ref, width in zip((fqkv_ref, guv_ref, nq_ref, nkv_ref, pz_ref), _IN_GROUPS[:5]):
        ref[...] = _dot(h, w_ref[:, off:off + width])
        off += width
    kcvc = _dot(h, w_ref[:, off:off + LANES])
    kc_ref[...] = kcvc[:, :HEAD_DIM]
    vc_ref[...] = kcvc[:, HEAD_DIM:]
    off += LANES
    misc_ref[...] = _dot(h, w_ref[:, off:off + LANES])


def _relayout_w_in(w_in):
    o = np.cumsum((0, 256, 256, 256, 4, 256, 256, 256, 64, 64, 64, 64, 64, 64, 12, 256))
    fq, ff, gu, nq, nkc, nks, ng, pz, end = o[0], o[3], o[4], o[6], o[7], o[9], o[13], o[14], o[15]
    d = w_in.shape[0]
    pad = jnp.zeros((d, LANES - 16), w_in.dtype)
    return jnp.concatenate([
        w_in[:, fq:ff], w_in[:, gu:nq], w_in[:, nq:nkc], w_in[:, nks:ng], w_in[:, pz:end],
        w_in[:, nkc:nks], w_in[:, ff:gu], w_in[:, ng:pz], pad], axis=1)


def _inproj(x, g, w_in):
    n, d = x.shape
    tm = min(FFN_TOKENS, n)
    w = _relayout_w_in(w_in).astype(BF16)
    widths = (768, 512, 256, 256, 256, HEAD_DIM, HEAD_DIM, LANES)
    return pl.pallas_call(
        _inproj_kernel,
        grid=(n // tm,),
        in_specs=[
            pl.BlockSpec((tm, d), lambda i: (i, 0)),
            pl.BlockSpec((1, d), lambda i: (0, 0)),
            pl.BlockSpec(w.shape, lambda i: (0, 0)),
        ],
        out_specs=[pl.BlockSpec((tm, wd), lambda i: (i, 0)) for wd in widths],
        out_shape=[jax.ShapeDtypeStruct((n, wd), F32) for wd in widths],
        compiler_params=_params("parallel"),
        name="inproj",
    )(x, g.reshape(1, d), w)


def _outproj_kernel(x_ref, oa_ref, ob_ref, oc_ref, od_ref, w_ref, o_ref):
    acc = x_ref[...]
    for k, ref in enumerate((oa_ref, ob_ref, oc_ref, od_ref)):
        acc = acc + _dot(ref[...].astype(BF16), w_ref[k * W_MIX:(k + 1) * W_MIX, :])
    o_ref[...] = acc


def _outproj(x, o_a, o_b, o_c, o_d, w_out):
    n, d = x.shape
    tm = min(FFN_TOKENS, n)
    mix = pl.BlockSpec((tm, W_MIX), lambda i: (i, 0))
    return pl.pallas_call(
        _outproj_kernel,
        grid=(n // tm,),
        in_specs=[pl.BlockSpec((tm, d), lambda i: (i, 0)), mix, mix, mix, mix,
                  pl.BlockSpec(w_out.shape, lambda i: (0, 0))],
        out_specs=pl.BlockSpec((tm, d), lambda i: (i, 0)),
        out_shape=jax.ShapeDtypeStruct((n, d), F32),
        compiler_params=_params("parallel"),
        name="outproj",
    )(x, o_a, o_b, o_c, o_d, w_out.astype(BF16))


def _bf16_pieces(x):
    p1 = x.astype(BF16).astype(F32)
    r1 = x - p1
    p2 = r1.astype(BF16).astype(F32)
    p3 = (r1 - p2).astype(BF16).astype(F32)
    return p1, p2, p3


_N_PIECES = 3
_N_EXTRA = _N_PIECES * N_HEADS
_PACK_ONE = _N_EXTRA


def _fox_placement():
    pq = np.zeros((LANES, LANES), np.float32)
    pk = np.zeros((LANES, LANES), np.float32)
    for h in range(N_HEADS):
        for p in range(_N_PIECES):
            pq[p * N_HEADS + h, HEAD_DIM + _N_PIECES * h + p] = 1.0
            pk[p * N_HEADS + h, HEAD_DIM + _N_EXTRA + _N_PIECES * h + p] = -1.0
    pq[_PACK_ONE, HEAD_DIM + _N_EXTRA:HEAD_DIM + 2 * _N_EXTRA] = 1.0
    pk[_PACK_ONE, HEAD_DIM:HEAD_DIM + _N_EXTRA] = 1.0
    return jnp.asarray(pq, BF16), jnp.asarray(pk, BF16)


def _fox_kernel(qkv_ref, misc_ref, fb_ref, gq_ref, gk_ref, gmean_ref, pq_ref, pk_ref, o_ref, q_s, k_s, v_s, o_s):
    t = qkv_ref.shape[0]
    pr = min(FOX_PREP_ROWS, t)
    gmean = gmean_ref[...]
    tri = (lax.broadcasted_iota(jnp.int32, (pr, pr), 1) <= lax.broadcasted_iota(jnp.int32, (pr, pr), 0)).astype(BF16)
    lane = lax.broadcasted_iota(jnp.int32, (pr, LANES), 1)
    extra = lane - HEAD_DIM
    own = [((extra >= _N_PIECES * h) & (extra < _N_PIECES * (h + 1)))
           | ((extra >= _N_EXTRA + _N_PIECES * h) & (extra < _N_EXTRA + _N_PIECES * (h + 1))) for h in range(N_HEADS)]
    ones_v = jnp.where(extra == 0, 1.0, 0.0)
    carry = jnp.zeros((1, LANES), F32)
    for r in range(t // pr):
        sl = slice(r * pr, (r + 1) * pr)
        log_f = _log_sigmoid(misc_ref[sl, :] + fb_ref[...])
        c = sum(_dot(tri, piece.astype(BF16)) for piece in _bf16_pieces(log_f)) + carry
        carry = c[pr - 1:pr, :]
        p1, p2, p3 = _bf16_pieces(c)
        packed = jnp.where(lane < N_HEADS, p1, jnp.where(lane < 2 * N_HEADS, pltpu.roll(p2, N_HEADS, axis=1), jnp.where(
            lane < _N_EXTRA, pltpu.roll(p3, 2 * N_HEADS, axis=1), jnp.where(lane == _PACK_ONE, 1.0, 0.0)))).astype(BF16)
        q_extra = _dot(packed, pq_ref[...])
        k_extra = _dot(packed, pk_ref[...])
        qn = _group_rms(qkv_ref[sl, 0:W_MIX], gmean) * gq_ref[...] * ATTN_SCALE
        kn = _group_rms(qkv_ref[sl, W_MIX:2 * W_MIX], gmean) * gk_ref[...]
        v = qkv_ref[sl, 2 * W_MIX:3 * W_MIX]
        for h in range(N_HEADS):
            q_s[h, sl, :] = jnp.where(extra < 0, _head_column(qn, h), q_extra).astype(BF16)
            k_s[h, sl, :] = jnp.where(extra < 0, _head_column(kn, h), jnp.where(own[h], k_extra, 0.0)).astype(BF16)
            v_s[h, sl, :] = jnp.where(extra < 0, _head_column(v, h), ones_v).astype(BF16)

    tq = min(FOX_TILE, t)
    causal = lax.broadcasted_iota(jnp.int32, (tq, tq), 1) <= lax.broadcasted_iota(jnp.int32, (tq, tq), 0)

    def head_body(h, _):
        for qi in range(t // tq):
            dsl = slice(qi * tq, (qi + 1) * tq)
            q = q_s[h, dsl, :]
            s_d = jnp.where(causal, _dot_nt(q, k_s[h, dsl, :]), NEG_INF)
            m = jnp.max(s_d, axis=-1, keepdims=True)
            if qi:
                s_o = _dot_nt(q, k_s[h, 0:qi * tq, :])
                m = jnp.maximum(m, jnp.max(s_o, axis=-1, keepdims=True))
            acc = _dot(jnp.exp(s_d - m).astype(BF16), v_s[h, dsl, :])
            if qi:
                acc = acc + _dot(jnp.exp(s_o - m).astype(BF16), v_s[h, 0:qi * tq, :])
            o_s[h, dsl, :] = acc / acc[:, ONE_LANE:ONE_LANE + 1]
        return 0

    lax.fori_loop(0, N_HEADS, head_body, 0)
    for hp in range(N_HEADS // 2):
        o_ref[:, hp * LANES:(hp + 1) * LANES] = _pair_heads(o_s[2 * hp], o_s[2 * hp + 1])


def _group_mean_matrix(width):
    g = np.kron(np.eye(width // HEAD_DIM, dtype=np.float32), np.full((HEAD_DIM, HEAD_DIM), 1.0 / HEAD_DIM, np.float32))
    return jnp.asarray(g, BF16)


def _fox(fqkv, misc, f_bias, g_q, g_k, b, t):
    assert t % min(FOX_TILE, t) == 0 and t % min(FOX_PREP_ROWS, t) == 0
    fb = jnp.zeros((1, LANES), F32).at[0, _MISC_FORGET:_MISC_FORGET + N_HEADS].set(f_bias)
    gq = jnp.tile(g_q, N_HEADS).reshape(1, W_MIX)
    gk = jnp.tile(g_k, N_HEADS).reshape(1, W_MIX)
    const = lambda shape: pl.BlockSpec(shape, lambda i: (0,) * len(shape))
    head_scratch = lambda dtype: pltpu.VMEM((N_HEADS, t, LANES), dtype)
    return pl.pallas_call(
        _fox_kernel,
        grid=(b,),
        in_specs=[pl.BlockSpec((t, 3 * W_MIX), lambda i: (i, 0)), pl.BlockSpec((t, LANES), lambda i: (i, 0)),
                  const((1, LANES)), const((1, W_MIX)), const((1, W_MIX)), const((W_MIX, W_MIX)),
                  const((LANES, LANES)), const((LANES, LANES))],
        out_specs=pl.BlockSpec((t, W_MIX), lambda i: (i, 0)),
        out_shape=jax.ShapeDtypeStruct((b * t, W_MIX), F32),
        scratch_shapes=[head_scratch(BF16), head_scratch(BF16), head_scratch(BF16), head_scratch(F32)],
        compiler_params=_params("parallel"),
        name="fox",
    )(fqkv, misc, fb, gq, gk, _group_mean_matrix(W_MIX), *_fox_placement())


def _gmlp_kernel(uv_ref, gv_ref, gmean_ref, w_ref, bias_ref, o_ref):
    c = GMLP_CHUNK
    rows = lax.broadcasted_iota(jnp.int32, (c, c), 0)
    cols = lax.broadcasted_iota(jnp.int32, (c, c), 1)
    lane_group = lax.broadcasted_iota(jnp.int32, (c, W_MIX), 1) // HEAD_DIM
    w_tril = [jnp.where(cols <= rows, w_ref[g], 0.0).astype(BF16) for g in range(N_HEADS)]
    for r in range(uv_ref.shape[0] // c):
        sl = slice(r * c, (r + 1) * c)
        u = _gelu(uv_ref[sl, 0:W_MIX])
        v = _group_rms(_gelu(uv_ref[sl, W_MIX:2 * W_MIX]), gmean_ref[...]) * gv_ref[...]
        s = bias_ref[...]
        for g in range(N_HEADS):
            s = s + _dot(w_tril[g], jnp.where(lane_group == g, v, 0.0).astype(BF16))
        o_ref[sl, :] = u * s


def _gmlp(guv, g_v, w_s, b_s, n):
    tt = min(1024, n)
    bias = jnp.repeat(b_s.T, HEAD_DIM, axis=1)
    const = lambda shape: pl.BlockSpec(shape, lambda i: (0,) * len(shape))
    return pl.pallas_call(
        _gmlp_kernel,
        grid=(n // tt,),
        in_specs=[pl.BlockSpec((tt, 2 * W_MIX), lambda i: (i, 0)), const((1, W_MIX)), const((W_MIX, W_MIX)),
                  const(w_s.shape), const(bias.shape)],
        out_specs=pl.BlockSpec((tt, W_MIX), lambda i: (i, 0)),
        out_shape=jax.ShapeDtypeStruct((n, W_MIX), F32),
        compiler_params=_params("parallel"),
        name="gmlp",
    )(guv, g_v.reshape(1, W_MIX), _group_mean_matrix(W_MIX), w_s, bias)


def _pool_kernel(z_ref, w_ref, scale_ref, o_ref):
    t = z_ref.shape[0]
    z = z_ref[...]
    row = lax.broadcasted_iota(jnp.int32, (t, W_MIX), 0)
    lane_group = lax.broadcasted_iota(jnp.int32, (t, W_MIX), 1) // HEAD_DIM
    sums = []
    s = z
    k = 1
    while k < max(POOL_SIZES):
        s = s + jnp.where(row >= k, pltpu.roll(s, k, axis=0), 0.0)
        k *= 2
        sums.append((k, s))
    win_sum = jnp.zeros_like(z)
    win = jnp.zeros((t, W_MIX), jnp.int32)
    for g, size in enumerate(POOL_SIZES):
        win_sum = jnp.where(lane_group == g, dict(sums)[size], win_sum)
        win = jnp.where(lane_group == g, size, win)
    cnt = jnp.minimum(row + 1, win).astype(F32)
    pooled = win_sum / cnt - z
    o_ref[...] = _dot(pooled.astype(BF16), w_ref[...]) * scale_ref[...]


def _pool(pz, w_p, scale, b, t):
    w_bd = jax.scipy.linalg.block_diag(*[w_p[g] for g in range(N_HEADS)]).astype(BF16)
    const = lambda shape: pl.BlockSpec(shape, lambda i: (0,) * len(shape))
    return pl.pallas_call(
        _pool_kernel,
        grid=(b,),
        in_specs=[pl.BlockSpec((t, W_MIX), lambda i: (i, 0)), const((W_MIX, W_MIX)), const((1, W_MIX))],
        out_specs=pl.BlockSpec((t, W_MIX), lambda i: (i, 0)),
        out_shape=jax.ShapeDtypeStruct((b * t, W_MIX), F32),
        compiler_params=_params("parallel"),
        name="pool",
    )(pz, w_bd, scale.reshape(1, W_MIX))


def _nsa_prep_kernel(kv_ref, kc_ref, vc_ref, posk_ref, posv_ref, kw1_ref, vw1_ref, kw2_ref, vw2_ref,
                     gkv_ref, nmask_ref, gmean_ref, c_ref, su_ref, sd_ref, gc_ref, cc_ref, csu_ref, csd_ref,
                     ks_ref, vs_ref, kw_ref, vw_ref, kcmp_ref, vcmp_ref):
    t = kv_ref.shape[0]
    x = kv_ref[...]
    normed = _group_rms(x, gmean_ref[...]) * gkv_ref[...]
    y = _rope(jnp.where(nmask_ref[...] > 0.5, normed, x), c_ref[...], su_ref[...], sd_ref[...])
    lane = lax.broadcasted_iota(jnp.int32, (t, LANES), 1)
    block = lax.broadcasted_iota(jnp.int32, (t, LANES), 0) // SEL_LEN
    key = lane < HEAD_DIM
    ks_v, kw_v = y[:, 0:LANES], y[:, LANES:2 * LANES]
    ks_ref[...] = jnp.where(key, ks_v, jnp.where(lane - HEAD_DIM == block, 1.0, 0.0)).astype(BF16)
    vs_ref[...] = jnp.where(key, pltpu.roll(ks_v, HEAD_DIM, axis=1), _ones_lane((t, LANES))).astype(BF16)
    kw_ref[...] = jnp.where(key, kw_v, 0.0).astype(BF16)
    vw_ref[...] = jnp.where(key, pltpu.roll(kw_v, HEAD_DIM, axis=1), _ones_lane((t, LANES))).astype(BF16)

    half = kc_ref.shape[1]
    nb = kc_ref.shape[0]

    def hidden(x_ref, pos_ref, w1_ref):
        top = _dot((x_ref[...] + pos_ref[:, 0:half]).astype(BF16), w1_ref[0:half, :])
        bot = _dot((x_ref[...] + pos_ref[:, half:2 * half]).astype(BF16), w1_ref[half:2 * half, :])
        return _gelu(top + pltpu.roll(bot, nb - 1, axis=0)).astype(BF16)

    kv_cmp = _dot(hidden(kc_ref, posk_ref, kw1_ref), kw2_ref[...]) + _dot(hidden(vc_ref, posv_ref, vw1_ref), vw2_ref[...])
    key = lax.broadcasted_iota(jnp.int32, kv_cmp.shape, 1) < HEAD_DIM
    normed = _group_rms(kv_cmp, gmean_ref[0:LANES, 0:LANES]) * gc_ref[...]
    y = _rope(jnp.where(key, normed, kv_cmp), cc_ref[...], csu_ref[...], csd_ref[...])
    kcmp_ref[...] = jnp.where(key, y, 0.0).astype(BF16)
    vcmp_ref[...] = jnp.where(key, pltpu.roll(y, HEAD_DIM, axis=1), _ones_lane(y.shape)).astype(BF16)


def _rope_tables(pos):
    inv = ROPE_THETA ** (-jnp.arange(ROPE_HALF, dtype=F32) * 2.0 / ROPE_DIM)
    ang = pos.astype(F32)[:, None] * inv[None, :]
    cos, sin = jnp.cos(ang), jnp.sin(ang)
    n = pos.shape[0]
    zero = jnp.zeros((n, ROPE_HALF), F32)
    rest0 = jnp.zeros((n, HEAD_DIM - ROPE_DIM), F32)
    c = jnp.concatenate([cos, cos, rest0 + 1.0], axis=1)
    s_up = jnp.concatenate([-sin, zero, rest0], axis=1)
    s_dn = jnp.concatenate([zero, sin, rest0], axis=1)
    return c, s_up, s_dn


def _identity_tables(n):
    return jnp.ones((n, HEAD_DIM), F32), jnp.zeros((n, HEAD_DIM), F32), jnp.zeros((n, HEAD_DIM), F32)


def _nsa_prep(nkv, kc, vc, g_kc, g_ks, g_kw, pos_k, k_w1, k_w2, pos_v, v_w1, v_w2, b, t):
    nb = t // CMP_STRIDE
    assert HEAD_DIM + t // SEL_LEN <= LANES
    flat = CMP_STRIDE * HEAD_DIM
    kc2 = kc.reshape(b * nb, flat)
    vc2 = vc.reshape(b * nb, flat)
    rope_t = _rope_tables(jnp.arange(t))
    iden_t = _identity_tables(t)
    tabs = [jnp.concatenate([r, i, r, i], axis=1) for r, i in zip(rope_t, iden_t)]
    rope_c = _rope_tables(jnp.arange(nb) * CMP_STRIDE + CMP_LEN - 1)
    iden_c = _identity_tables(nb)
    tabs_c = [jnp.concatenate([r, i], axis=1) for r, i in zip(rope_c, iden_c)]
    one = jnp.ones((HEAD_DIM,), F32)
    gkv = jnp.concatenate([g_ks, one, g_kw, one]).reshape(1, W_MIX)
    nmask = jnp.concatenate([one, 0 * one, one, 0 * one]).reshape(1, W_MIX)
    gc = jnp.concatenate([g_kc, one]).reshape(1, LANES)
    zpad = jnp.zeros_like(k_w2)
    kw2 = jnp.concatenate([k_w2, zpad], axis=1).astype(BF16)
    vw2 = jnp.concatenate([zpad, v_w2], axis=1).astype(BF16)
    const = lambda shape: pl.BlockSpec(shape, lambda i: (0,) * len(shape))
    seq = lambda width: pl.BlockSpec((t, width), lambda i: (i, 0))
    cmp_in = pl.BlockSpec((nb, flat), lambda i: (i, 0))
    cmp_out = pl.BlockSpec((nb, LANES), lambda i: (i, 0))
    return pl.pallas_call(
        _nsa_prep_kernel,
        grid=(b,),
        in_specs=[seq(W_MIX), cmp_in, cmp_in, const((1, 2 * flat)), const((1, 2 * flat)),
                  const(k_w1.shape), const(v_w1.shape), const(kw2.shape), const(vw2.shape),
                  const((1, W_MIX)), const((1, W_MIX)), const((W_MIX, W_MIX)),
                  const((t, W_MIX)), const((t, W_MIX)), const((t, W_MIX)),
                  const((1, LANES)), const((nb, LANES)), const((nb, LANES)), const((nb, LANES))],
        out_specs=[seq(LANES)] * 4 + [cmp_out] * 2,
        out_shape=[jax.ShapeDtypeStruct((b * t, LANES), BF16)] * 4 + [jax.ShapeDtypeStruct((b * nb, LANES), BF16)] * 2,
        compiler_params=_params("parallel"),
        name="nsa_prep",
    )(nkv, kc2, vc2, pos_k.reshape(1, 2 * flat), pos_v.reshape(1, 2 * flat),
      k_w1.astype(BF16), v_w1.astype(BF16), kw2, vw2, gkv, nmask, _group_mean_matrix(W_MIX),
      *tabs, gc, *tabs_c)


def _nsa_kernel(q_ref, misc_ref, gb_ref, gq_ref, gmean_ref, c_ref, su_ref, sd_ref, ovt_ref,
                kcmp_ref, vcmp_ref, ks_ref, vs_ref, kw_ref, vw_ref, o_ref, *, n_top):
    tq = q_ref.shape[0]
    qi = pl.program_id(1)
    t0 = qi * tq
    nb = kcmp_ref.shape[0]
    nsel = ks_ref.shape[0] // SEL_LEN
    rows = N_HEADS * tq
    stack = lambda x: jnp.concatenate([x] * N_HEADS, axis=0)

    qn = _rope(_group_rms(q_ref[...], gmean_ref[...]) * gq_ref[...], c_ref[...], su_ref[...], sd_ref[...]) * ATTN_SCALE
    q_cols = [_head_column(qn, h) for h in range(N_HEADS)]
    lane_q = lax.broadcasted_iota(jnp.int32, (tq, LANES), 1)

    q0 = jnp.concatenate([jnp.where(lane_q < HEAD_DIM, col, 0.0) for col in q_cols], axis=0).astype(BF16)
    cmp_end = lax.broadcasted_iota(jnp.int32, (rows, nb), 1) * CMP_STRIDE + (CMP_LEN - 1)
    qpos_c = (lax.broadcasted_iota(jnp.int32, (rows, nb), 0) & (tq - 1)) + t0
    m_cmp = cmp_end <= qpos_c
    s = jnp.where(m_cmp, _dot_nt(q0, kcmp_ref[...]), NEG_INF)
    e = jnp.where(m_cmp, jnp.exp(s - jnp.max(s, axis=-1, keepdims=True)), 0.0)
    denom = jnp.sum(e, axis=-1, keepdims=True)
    p_cmp = e / jnp.where(denom > 0.0, denom, 1.0)
    o_cmp = _dot(p_cmp.astype(BF16), vcmp_ref[...])

    p_sum = p_cmp[0:tq] + p_cmp[tq:2 * tq] + p_cmp[2 * tq:3 * tq] + p_cmp[3 * tq:4 * tq]
    nsp = -(-nsel // 8) * 8
    imp = lax.dot_general(ovt_ref[0:nsp, :], p_sum, (((1,), (1,)), ((), ())), precision=HIGHEST,
                          preferred_element_type=F32)
    blk = lax.broadcasted_iota(jnp.int32, (nsp, tq), 0)
    cur = (lax.broadcasted_iota(jnp.int32, (nsp, tq), 1) + t0) // SEL_LEN
    valid = blk <= cur
    forced = ((blk == 0) | (blk == cur) | (blk == cur - 1)).astype(F32)
    imp = jnp.where(valid, imp + SEL_FORCE * forced, NEG_INF)
    cur_row = (lax.broadcasted_iota(jnp.int32, (1, tq), 1) + t0) // SEL_LEN
    bias_t = jnp.where(blk < nsel, NEG_INF, 0.0)
    for j in range(nsel):
        mine = imp[j:j + 1, :]
        beats = (imp > mine) | ((imp == mine) & (blk < j))
        count = jnp.sum(jnp.where(beats & valid, 1.0, 0.0), axis=0, keepdims=True)
        keep = (count < n_top) & (cur_row >= j)
        bias_t = jnp.where(blk == j, jnp.where(keep, 0.0, NEG_INF), bias_t)
    if nsp < LANES:
        bias_t = jnp.concatenate([bias_t, jnp.zeros((LANES - nsp, tq), F32)], axis=0)
    bias = jnp.concatenate([bias_t[:, c * LANES:(c + 1) * LANES].T for c in range(tq // LANES)], axis=0)
    bias = pltpu.roll(bias, HEAD_DIM, axis=1)
    q = jnp.concatenate([jnp.where(lane_q < HEAD_DIM, col, bias) for col in q_cols], axis=0).astype(BF16)

    trow = lax.broadcasted_iota(jnp.int32, (rows, tq), 0) & (tq - 1)
    tcol = lax.broadcasted_iota(jnp.int32, (rows, tq), 1)
    causal_bias = jnp.where(tcol <= trow, 0.0, NEG_INF)

    def tile(j):
        return pl.ds(pl.multiple_of(j * tq, tq), tq)

    def sel_step(kj, extra_bias, st):
        m, acc = st
        s = _dot_nt(q, ks_ref[tile(kj), :])
        if extra_bias is not None:
            s = s + extra_bias
        m_new = jnp.maximum(m, jnp.max(s, axis=-1, keepdims=True))
        acc = jnp.exp(m - m_new) * acc + _dot(jnp.exp(s - m_new).astype(BF16), vs_ref[tile(kj), :])
        return m_new, acc

    st = (jnp.full((rows, 1), NEG_INF, F32), jnp.zeros((rows, LANES), F32))
    st = lax.fori_loop(0, qi, lambda kj, st: sel_step(kj, None, st), st)
    _, acc = sel_step(qi, causal_bias, st)
    o_sel = acc / acc[:, ONE_LANE:ONE_LANE + 1]

    span = WINDOW // tq
    parts = []
    for d in range(span, -1, -1):
        kj = jnp.maximum(qi - d, 0)
        if d == 0:
            bias_w = causal_bias
        elif d == span:
            bias_w = jnp.where((tcol > trow) & (qi >= d), 0.0, NEG_INF)
        else:
            bias_w = jnp.where(qi >= d, 0.0, NEG_INF)
        parts.append((_dot_nt(q, kw_ref[tile(kj), :]) + bias_w, kj))
    m = functools.reduce(jnp.maximum, [jnp.max(s, axis=-1, keepdims=True) for s, _ in parts])
    acc = sum(_dot(jnp.exp(s - m).astype(BF16), vw_ref[tile(kj), :]) for s, kj in parts)
    o_win = acc / acc[:, ONE_LANE:ONE_LANE + 1]

    gate = jax.nn.sigmoid(misc_ref[...] + gb_ref[...])
    outs = []
    for h in range(N_HEADS):
        hr = slice(h * tq, (h + 1) * tq)
        g0 = _MISC_GATE + 3 * h
        outs.append(gate[:, g0:g0 + 1] * o_cmp[hr] + gate[:, g0 + 1:g0 + 2] * o_sel[hr] + gate[:, g0 + 2:g0 + 3] * o_win[hr])
    for hp in range(N_HEADS // 2):
        o_ref[:, hp * LANES:(hp + 1) * LANES] = _pair_heads(outs[2 * hp], outs[2 * hp + 1])


def _cmp_to_sel_overlap_t(nb, nsel):
    nc = nb - 1
    cs = np.arange(nc) * CMP_STRIDE
    ss = np.arange(nsel) * SEL_LEN
    ov = np.clip(np.minimum(cs[:, None] + CMP_LEN, ss[None, :] + SEL_LEN) - np.maximum(cs[:, None], ss[None, :]), 0, None)
    out = np.zeros((LANES, nb), np.float32)
    out[:nsel, :nc] = (ov / CMP_LEN).T
    return jnp.asarray(out)


def _nsa(nq, misc, gate_b, g_q, ks, vs, kw, vw, kcmp, vcmp, b, t):
    tq = min(NSA_TILE, t)
    nb = t // CMP_STRIDE
    nsel = t // SEL_LEN
    assert HEAD_DIM + nsel <= LANES and nb % 8 == 0 and WINDOW % tq == 0 and tq & (tq - 1) == 0 and tq % LANES == 0
    gb = jnp.zeros((1, LANES), F32).at[0, _MISC_GATE:_MISC_GATE + 3 * N_HEADS].set(gate_b)
    gq = jnp.tile(g_q, N_HEADS).reshape(1, W_MIX)
    tabs = [jnp.tile(x, (1, N_HEADS)) for x in _rope_tables(jnp.arange(t))]
    nq_tiles = t // tq
    const = lambda shape: pl.BlockSpec(shape, lambda i, j: (0,) * len(shape))
    qtile = lambda width: pl.BlockSpec((tq, width), lambda i, j: (i * nq_tiles + j, 0))
    ptile = pl.BlockSpec((tq, W_MIX), lambda i, j: (j, 0))
    seq = pl.BlockSpec((t, LANES), lambda i, j: (i, 0))
    cmp = pl.BlockSpec((nb, LANES), lambda i, j: (i, 0))
    return pl.pallas_call(
        functools.partial(_nsa_kernel, n_top=min(SEL_TOP, nsel)),
        grid=(b, nq_tiles),
        in_specs=[qtile(W_MIX), qtile(LANES), const((1, LANES)), const((1, W_MIX)), const((W_MIX, W_MIX)),
                  ptile, ptile, ptile, const((LANES, nb)), cmp, cmp, seq, seq, seq, seq],
        out_specs=qtile(W_MIX),
        out_shape=jax.ShapeDtypeStruct((b * t, W_MIX), F32),
        compiler_params=_params("parallel", "arbitrary"),
        name="nsa",
    )(nq, misc, gb, gq, _group_mean_matrix(W_MIX), *tabs, _cmp_to_sel_overlap_t(nb, nsel),
      kcmp, vcmp, ks, vs, kw, vw)


def kernel(x, ffn1_norm, ffn1_w1, ffn1_w3, ffn1_w2, mix_norm, w_in, w_out, fox_f_bias, fox_q_norm, fox_k_norm, gmlp_v_norm, gmlp_w_s, gmlp_b_s, nsa_q_norm, nsa_kc_norm, nsa_ks_norm, nsa_kw_norm, nsa_cmp_pos_k, nsa_cmp_k_w1, nsa_cmp_k_w2, nsa_cmp_pos_v, nsa_cmp_v_w1, nsa_cmp_v_w2, nsa_gate_bias, pool_w, pool_scale, ffn2_norm, ffn2_w1, ffn2_w3, ffn2_w2):
    b, t, d = x.shape
    assert t % GMLP_CHUNK == 0 and t >= CMP_LEN
    n = b * t
    xf = x.reshape(n, d)
    for l in range(ffn1_norm.shape[0]):
        xf = _ffn(xf, ffn1_norm[l], ffn1_w1[l], ffn1_w3[l], ffn1_w2[l])
        fqkv, guv, nq, nkv, pz, kc, vc, misc = _inproj(xf, mix_norm[l], w_in[l])
        o_a = _fox(fqkv, misc, fox_f_bias[l], fox_q_norm[l], fox_k_norm[l], b, t)
        o_b = _gmlp(guv, gmlp_v_norm[l], gmlp_w_s[l], gmlp_b_s[l], n)
        ks, vs, kw, vw, kcmp, vcmp = _nsa_prep(
            nkv, kc, vc, nsa_kc_norm[l], nsa_ks_norm[l], nsa_kw_norm[l],
            nsa_cmp_pos_k[l], nsa_cmp_k_w1[l], nsa_cmp_k_w2[l],
            nsa_cmp_pos_v[l], nsa_cmp_v_w1[l], nsa_cmp_v_w2[l], b, t)
        o_c = _nsa(nq, misc, nsa_gate_bias[l], nsa_q_norm[l], ks, vs, kw, vw, kcmp, vcmp, b, t)
        o_d = _pool(pz, pool_w[l], pool_scale[l], b, t)
        xf = _outproj(xf, o_a, o_b, o_c, o_d, w_out[l])
        xf = _ffn(xf, ffn2_norm[l], ffn2_w1[l], ffn2_w3[l], ffn2_w2[l])
    return xf.reshape(b, t, d)
```

```python
import functools

import numpy as np
import jax
import jax.numpy as jnp
from jax import lax
from jax.experimental import pallas as pl
from jax.experimental.pallas import tpu as pltpu

F32 = jnp.float32
BF16 = jnp.bfloat16
HIGHEST = lax.Precision.HIGHEST

HEAD_DIM = 64
N_HEADS = 4
W_MIX = N_HEADS * HEAD_DIM
ROPE_THETA = 500000.0
ROPE_DIM = HEAD_DIM // 4
ROPE_HALF = ROPE_DIM // 2
GMLP_CHUNK = 128
CMP_LEN = 32
CMP_STRIDE = 16
SEL_LEN = 64
SEL_TOP = 16
WINDOW = 512
POOL_SIZES = (2, 4, 8, 16)
FFN_RES_WEIGHT = 0.5
EPS = 1e-6
NEG_INF = -1e30
SEL_FORCE = 1e3
ATTN_SCALE = HEAD_DIM ** -0.5

LANES = 128
VMEM_LIMIT = 52 * 1024 * 1024

FFN_TOKENS = 1024
FFN_HIDDEN = 256
FOX_PREP_ROWS = 256
FOX_TILE = 512
NSA_TILE = 512
ONE_LANE = HEAD_DIM


def _params(*sem):
    return pltpu.CompilerParams(dimension_semantics=sem, vmem_limit_bytes=VMEM_LIMIT)


def _dot(a, b, **kw):
    return jnp.dot(a, b, preferred_element_type=F32, **kw)


def _dot_nt(a, b):
    return lax.dot_general(a, b, (((1,), (1,)), ((), ())), preferred_element_type=F32)


def _rms(x):
    return x * lax.rsqrt(jnp.mean(x * x, axis=-1, keepdims=True) + EPS)


def _group_rms(x, gmean):
    sq = x * x
    hi = sq.astype(BF16)
    lo = (sq - hi.astype(F32)).astype(BF16)
    ms = _dot(hi, gmean) + _dot(lo, gmean)
    return x * lax.rsqrt(ms + EPS)


def _rope(x, c, s_up, s_dn):
    w = x.shape[-1]
    return x * c + pltpu.roll(x, w - ROPE_HALF, axis=1) * s_up + pltpu.roll(x, ROPE_HALF, axis=1) * s_dn


def _gelu(x):
    return 0.5 * x * (1.0 + jnp.tanh(0.7978845608028654 * (x + 0.044715 * (x * x * x))))


def _log_sigmoid(x):
    return jnp.minimum(x, 0.0) - jnp.log1p(jnp.exp(-jnp.abs(x)))


def _head_column(x, h):
    col = x[:, (h // 2) * LANES:(h // 2 + 1) * LANES]
    return pltpu.roll(col, HEAD_DIM, axis=1) if h % 2 else col


def _pair_heads(even, odd):
    lane = lax.broadcasted_iota(jnp.int32, even.shape, 1)
    return jnp.where(lane < HEAD_DIM, even, pltpu.roll(odd, HEAD_DIM, axis=1))


def _ones_lane(shape):
    return jnp.where(lax.broadcasted_iota(jnp.int32, shape, 1) == ONE_LANE, 1.0, 0.0)


def _ffn_kernel(x_ref, g_ref, w1_ref, w3_ref, w2_ref, o_ref):
    x = x_ref[...]
    h = (_rms(x) * g_ref[...]).astype(BF16)
    acc = jnp.zeros_like(x)
    for c in range(w1_ref.shape[1] // FFN_HIDDEN):
        cs = slice(c * FFN_HIDDEN, (c + 1) * FFN_HIDDEN)
        a = _dot(h, w1_ref[:, cs])
        b = _dot(h, w3_ref[:, cs])
        acc = acc + _dot((a * jax.nn.sigmoid(a) * b).astype(BF16), w2_ref[cs, :])
    o_ref[...] = x + FFN_RES_WEIGHT * acc


def _ffn(x, g, w1, w3, w2):
    n, d = x.shape
    tm = min(FFN_TOKENS, n)
    resident = lambda shape: pl.BlockSpec(shape, lambda i: (0, 0), pipeline_mode=pl.Buffered(1))
    return pl.pallas_call(
        _ffn_kernel,
        grid=(n // tm,),
        in_specs=[pl.BlockSpec((tm, d), lambda i: (i, 0)), resident((1, d)),
                  resident(w1.shape), resident(w3.shape), resident(w2.shape)],
        out_specs=pl.BlockSpec((tm, d), lambda i: (i, 0)),
        out_shape=jax.ShapeDtypeStruct((n, d), F32),
        compiler_params=_params("parallel"),
        name="ffn",
    )(x, g.reshape(1, d), w1.astype(BF16), w3.astype(BF16), w2.astype(BF16))


_IN_GROUPS = (768, 512, 256, 256, 256, 128, 128)
_MISC_FORGET = 0
_MISC_GATE = 4


def _inproj_kernel(x_ref, g_ref, w_ref, fqkv_ref, guv_ref, nq_ref, nkv_ref, pz_ref, kc_ref, vc_ref, misc_ref):
    h = (_rms(x_ref[...]) * g_ref[...]).astype(BF16)
    off = 0
    for ref, width in zip((fqkv_ref, guv_ref, nq_ref, nkv_ref, pz_ref), _IN_GROUPS[:5]):
        ref[...] = _dot(h, w_ref[:, off:off + width])
        off += width
    kcvc = _dot(h, w_ref[:, off:off + LANES])
    kc_ref[...] = kcvc[:, :HEAD_DIM]
    vc_ref[...] = kcvc[:, HEAD_DIM:]
    off += LANES
    misc_ref[...] = _dot(h, w_ref[:, off:off + LANES])


def _relayout_w_in(w_in):
    o = np.cumsum((0, 256, 256, 256, 4, 256, 256, 256, 64, 64, 64, 64, 64, 64, 12, 256))
    fq, ff, gu, nq, nkc, nks, ng, pz, end = o[0], o[3], o[4], o[6], o[7], o[9], o[13], o[14], o[15]
    d = w_in.shape[0]
    pad = jnp.zeros((d, LANES - 16), w_in.dtype)
    return jnp.concatenate([
        w_in[:, fq:ff], w_in[:, gu:nq], w_in[:, nq:nkc], w_in[:, nks:ng], w_in[:, pz:end],
        w_in[:, nkc:nks], w_in[:, ff:gu], w_in[:, ng:pz], pad], axis=1)


def _inproj(x, g, w_in):
    n, d = x.shape
    tm = min(FFN_TOKENS, n)
    w = _relayout_w_in(w_in).astype(BF16)
    widths = (768, 512, 256, 256, 256, HEAD_DIM, HEAD_DIM, LANES)
    return pl.pallas_call(
        _inproj_kernel,
        grid=(n // tm,),
        in_specs=[
            pl.BlockSpec((tm, d), lambda i: (i, 0)),
            pl.BlockSpec((1, d), lambda i: (0, 0)),
            pl.BlockSpec(w.shape, lambda i: (0, 0)),
        ],
        out_specs=[pl.BlockSpec((tm, wd), lambda i: (i, 0)) for wd in widths],
        out_shape=[jax.ShapeDtypeStruct((n, wd), F32) for wd in widths],
        compiler_params=_params("parallel"),
        name="inproj",
    )(x, g.reshape(1, d), w)


def _outproj_kernel(x_ref, oa_ref, ob_ref, oc_ref, od_ref, w_ref, o_ref):
    acc = x_ref[...]
    for k, ref in enumerate((oa_ref, ob_ref, oc_ref, od_ref)):
        acc = acc + _dot(ref[...].astype(BF16), w_ref[k * W_MIX:(k + 1) * W_MIX, :])
    o_ref[...] = acc


def _outproj(x, o_a, o_b, o_c, o_d, w_out):
    n, d = x.shape
    tm = min(FFN_TOKENS, n)
    mix = pl.BlockSpec((tm, W_MIX), lambda i: (i, 0))
    return pl.pallas_call(
        _outproj_kernel,
        grid=(n // tm,),
        in_specs=[pl.BlockSpec((tm, d), lambda i: (i, 0)), mix, mix, mix, mix,
                  pl.BlockSpec(w_out.shape, lambda i: (0, 0))],
        out_specs=pl.BlockSpec((tm, d), lambda i: (i, 0)),
        out_shape=jax.ShapeDtypeStruct((n, d), F32),
        compiler_params=_params("parallel"),
        name="outproj",
    )(x, o_a, o_b, o_c, o_d, w_out.astype(BF16))


def _bf16_pieces(x):
    p1 = x.astype(BF16).astype(F32)
    r1 = x - p1
    p2 = r1.astype(BF16).astype(F32)
    p3 = (r1 - p2).astype(BF16).astype(F32)
    return p1, p2, p3


_N_PIECES = 3
_N_EXTRA = _N_PIECES * N_HEADS
_PACK_ONE = _N_EXTRA


def _fox_placement():
    pq = np.zeros((LANES, LANES), np.float32)
    pk = np.zeros((LANES, LANES), np.float32)
    for h in range(N_HEADS):
        for p in range(_N_PIECES):
            pq[p * N_HEADS + h, HEAD_DIM + _N_PIECES * h + p] = 1.0
            pk[p * N_HEADS + h, HEAD_DIM + _N_EXTRA + _N_PIECES * h + p] = -1.0
    pq[_PACK_ONE, HEAD_DIM + _N_EXTRA:HEAD_DIM + 2 * _N_EXTRA] = 1.0
    pk[_PACK_ONE, HEAD_DIM:HEAD_DIM + _N_EXTRA] = 1.0
    return jnp.asarray(pq, BF16), jnp.asarray(pk, BF16)


def _fox_kernel(qkv_ref, misc_ref, fb_ref, gq_ref, gk_ref, gmean_ref, pq_ref, pk_ref, o_ref, q_s, k_s, v_s, o_s):
    t = qkv_ref.shape[0]
    pr = min(FOX_PREP_ROWS, t)
    gmean = gmean_ref[...]
    tri = (lax.broadcasted_iota(jnp.int32, (pr, pr), 1) <= lax.broadcasted_iota(jnp.int32, (pr, pr), 0)).astype(BF16)
    lane = lax.broadcasted_iota(jnp.int32, (pr, LANES), 1)
    extra = lane - HEAD_DIM
    own = [((extra >= _N_PIECES * h) & (extra < _N_PIECES * (h + 1)))
           | ((extra >= _N_EXTRA + _N_PIECES * h) & (extra < _N_EXTRA + _N_PIECES * (h + 1))) for h in range(N_HEADS)]
    ones_v = jnp.where(extra == 0, 1.0, 0.0)
    carry = jnp.zeros((1, LANES), F32)
    for r in range(t // pr):
        sl = slice(r * pr, (r + 1) * pr)
        log_f = _log_sigmoid(misc_ref[sl, :] + fb_ref[...])
        c = sum(_dot(tri, piece.astype(BF16)) for piece in _bf16_pieces(log_f)) + carry
        carry = c[pr - 1:pr, :]
        p1, p2, p3 = _bf16_pieces(c)
        packed = jnp.where(lane < N_HEADS, p1, jnp.where(lane < 2 * N_HEADS, pltpu.roll(p2, N_HEADS, axis=1), jnp.where(
            lane < _N_EXTRA, pltpu.roll(p3, 2 * N_HEADS, axis=1), jnp.where(lane == _PACK_ONE, 1.0, 0.0)))).astype(BF16)
        q_extra = _dot(packed, pq_ref[...])
        k_extra = _dot(packed, pk_ref[...])
        qn = _group_rms(qkv_ref[sl, 0:W_MIX], gmean) * gq_ref[...] * ATTN_SCALE
        kn = _group_rms(qkv_ref[sl, W_MIX:2 * W_MIX], gmean) * gk_ref[...]
        v = qkv_ref[sl, 2 * W_MIX:3 * W_MIX]
        for h in range(N_HEADS):
            q_s[h, sl, :] = jnp.where(extra < 0, _head_column(qn, h), q_extra).astype(BF16)
            k_s[h, sl, :] = jnp.where(extra < 0, _head_column(kn, h), jnp.where(own[h], k_extra, 0.0)).astype(BF16)
            v_s[h, sl, :] = jnp.where(extra < 0, _head_column(v, h), ones_v).astype(BF16)

    tq = min(FOX_TILE, t)
    causal = lax.broadcasted_iota(jnp.int32, (tq, tq), 1) <= lax.broadcasted_iota(jnp.int32, (tq, tq), 0)

    def head_body(h, _):
        for qi in range(t // tq):
            dsl = slice(qi * tq, (qi + 1) * tq)
            q = q_s[h, dsl, :]
            s_d = jnp.where(causal, _dot_nt(q, k_s[h, dsl, :]), NEG_INF)
            m = jnp.max(s_d, axis=-1, keepdims=True)
            if qi:
                s_o = _dot_nt(q, k_s[h, 0:qi * tq, :])
                m = jnp.maximum(m, jnp.max(s_o, axis=-1, keepdims=True))
            acc = _dot(jnp.exp(s_d - m).astype(BF16), v_s[h, dsl, :])
            if qi:
                acc = acc + _dot(jnp.exp(s_o - m).astype(BF16), v_s[h, 0:qi * tq, :])
            o_s[h, dsl, :] = acc / acc[:, ONE_LANE:ONE_LANE + 1]
        return 0

    lax.fori_loop(0, N_HEADS, head_body, 0)
    for hp in range(N_HEADS // 2):
        o_ref[:, hp * LANES:(hp + 1) * LANES] = _pair_heads(o_s[2 * hp], o_s[2 * hp + 1])


def _group_mean_matrix(width):
    g = np.kron(np.eye(width // HEAD_DIM, dtype=np.float32), np.full((HEAD_DIM, HEAD_DIM), 1.0 / HEAD_DIM, np.float32))
    return jnp.asarray(g, BF16)


def _fox(fqkv, misc, f_bias, g_q, g_k, b, t):
    assert t % min(FOX_TILE, t) == 0 and t % min(FOX_PREP_ROWS, t) == 0
    fb = jnp.zeros((1, LANES), F32).at[0, _MISC_FORGET:_MISC_FORGET + N_HEADS].set(f_bias)
    gq = jnp.tile(g_q, N_HEADS).reshape(1, W_MIX)
    gk = jnp.tile(g_k, N_HEADS).reshape(1, W_MIX)
    const = lambda shape: pl.BlockSpec(shape, lambda i: (0,) * len(shape))
    head_scratch = lambda dtype: pltpu.VMEM((N_HEADS, t, LANES), dtype)
    return pl.pallas_call(
        _fox_kernel,
        grid=(b,),
        in_specs=[pl.BlockSpec((t, 3 * W_MIX), lambda i: (i, 0)), pl.BlockSpec((t, LANES), lambda i: (i, 0)),
                  const((1, LANES)), const((1, W_MIX)), const((1, W_MIX)), const((W_MIX, W_MIX)),
                  const((LANES, LANES)), const((LANES, LANES))],
        out_specs=pl.BlockSpec((t, W_MIX), lambda i: (i, 0)),
        out_shape=jax.ShapeDtypeStruct((b * t, W_MIX), F32),
        scratch_shapes=[head_scratch(BF16), head_scratch(BF16), head_scratch(BF16), head_scratch(F32)],
        compiler_params=_params("parallel"),
        name="fox",
    )(fqkv, misc, fb, gq, gk, _group_mean_matrix(W_MIX), *_fox_placement())


def _gmlp_kernel(uv_ref, gv_ref, gmean_ref, w_ref, bias_ref, o_ref):
    c = GMLP_CHUNK
    rows = lax.broadcasted_iota(jnp.int32, (c, c), 0)
    cols = lax.broadcasted_iota(jnp.int32, (c, c), 1)
    lane_group = lax.broadcasted_iota(jnp.int32, (c, W_MIX), 1) // HEAD_DIM
    w_tril = [jnp.where(cols <= rows, w_ref[g], 0.0).astype(BF16) for g in range(N_HEADS)]
    for r in range(uv_ref.shape[0] // c):
        sl = slice(r * c, (r + 1) * c)
        u = _gelu(uv_ref[sl, 0:W_MIX])
        v = _group_rms(_gelu(uv_ref[sl, W_MIX:2 * W_MIX]), gmean_ref[...]) * gv_ref[...]
        s = bias_ref[...]
        for g in range(N_HEADS):
            s = s + _dot(w_tril[g], jnp.where(lane_group == g, v, 0.0).astype(BF16))
        o_ref[sl, :] = u * s


def _gmlp(guv, g_v, w_s, b_s, n):
    tt = min(1024, n)
    bias = jnp.repeat(b_s.T, HEAD_DIM, axis=1)
    const = lambda shape: pl.BlockSpec(shape, lambda i: (0,) * len(shape))
    return pl.pallas_call(
        _gmlp_kernel,
        grid=(n // tt,),
        in_specs=[pl.BlockSpec((tt, 2 * W_MIX), lambda i: (i, 0)), const((1, W_MIX)), const((W_MIX, W_MIX)),
                  const(w_s.shape), const(bias.shape)],
        out_specs=pl.BlockSpec((tt, W_MIX), lambda i: (i, 0)),
        out_shape=jax.ShapeDtypeStruct((n, W_MIX), F32),
        compiler_params=_params("parallel"),
        name="gmlp",
    )(guv, g_v.reshape(1, W_MIX), _group_mean_matrix(W_MIX), w_s, bias)


def _pool_kernel(z_ref, w_ref, scale_ref, o_ref):
    t = z_ref.shape[0]
    z = z_ref[...]
    row = lax.broadcasted_iota(jnp.int32, (t, W_MIX), 0)
    lane_group = lax.broadcasted_iota(jnp.int32, (t, W_MIX), 1) // HEAD_DIM
    sums = []
    s = z
    k = 1
    while k < max(POOL_SIZES):
        s = s + jnp.where(row >= k, pltpu.roll(s, k, axis=0), 0.0)
        k *= 2
        sums.append((k, s))
    win_sum = jnp.zeros_like(z)
    win = jnp.zeros((t, W_MIX), jnp.int32)
    for g, size in enumerate(POOL_SIZES):
        win_sum = jnp.where(lane_group == g, dict(sums)[size], win_sum)
        win = jnp.where(lane_group == g, size, win)
    cnt = jnp.minimum(row + 1, win).astype(F32)
    pooled = win_sum / cnt - z
    o_ref[...] = _dot(pooled.astype(BF16), w_ref[...]) * scale_ref[...]


def _pool(pz, w_p, scale, b, t):
    w_bd = jax.scipy.linalg.block_diag(*[w_p[g] for g in range(N_HEADS)]).astype(BF16)
    const = lambda shape: pl.BlockSpec(shape, lambda i: (0,) * len(shape))
    return pl.pallas_call(
        _pool_kernel,
        grid=(b,),
        in_specs=[pl.BlockSpec((t, W_MIX), lambda i: (i, 0)), const((W_MIX, W_MIX)), const((1, W_MIX))],
        out_specs=pl.BlockSpec((t, W_MIX), lambda i: (i, 0)),
        out_shape=jax.ShapeDtypeStruct((b * t, W_MIX), F32),
        compiler_params=_params("parallel"),
        name="pool",
    )(pz, w_bd, scale.reshape(1, W_MIX))


def _nsa_prep_kernel(kv_ref, kc_ref, vc_ref, posk_ref, posv_ref, kw1_ref, vw1_ref, kw2_ref, vw2_ref,
                     gkv_ref, nmask_ref, gmean_ref, c_ref, su_ref, sd_ref, gc_ref, cc_ref, csu_ref, csd_ref,
                     ks_ref, vs_ref, kw_ref, vw_ref, kcmp_ref, vcmp_ref):
    t = kv_ref.shape[0]
    x = kv_ref[...]
    normed = _group_rms(x, gmean_ref[...]) * gkv_ref[...]
    y = _rope(jnp.where(nmask_ref[...] > 0.5, normed, x), c_ref[...], su_ref[...], sd_ref[...])
    lane = lax.broadcasted_iota(jnp.int32, (t, LANES), 1)
    block = lax.broadcasted_iota(jnp.int32, (t, LANES), 0) // SEL_LEN
    key = lane < HEAD_DIM
    ks_v, kw_v = y[:, 0:LANES], y[:, LANES:2 * LANES]
    ks_ref[...] = jnp.where(key, ks_v, jnp.where(lane - HEAD_DIM == block, 1.0, 0.0)).astype(BF16)
    vs_ref[...] = jnp.where(key, pltpu.roll(ks_v, HEAD_DIM, axis=1), _ones_lane((t, LANES))).astype(BF16)
    kw_ref[...] = jnp.where(key, kw_v, 0.0).astype(BF16)
    vw_ref[...] = jnp.where(key, pltpu.roll(kw_v, HEAD_DIM, axis=1), _ones_lane((t, LANES))).astype(BF16)

    half = kc_ref.shape[1]
    nb = kc_ref.shape[0]

    def hidden(x_ref, pos_ref, w1_ref):
        top = _dot((x_ref[...] + pos_ref[:, 0:half]).astype(BF16), w1_ref[0:half, :])
        bot = _dot((x_ref[...] + pos_ref[:, half:2 * half]).astype(BF16), w1_ref[half:2 * half, :])
        return _gelu(top + pltpu.roll(bot, nb - 1, axis=0)).astype(BF16)

    kv_cmp = _dot(hidden(kc_ref, posk_ref, kw1_ref), kw2_ref[...]) + _dot(hidden(vc_ref, posv_ref, vw1_ref), vw2_ref[...])
    key = lax.broadcasted_iota(jnp.int32, kv_cmp.shape, 1) < HEAD_DIM
    normed = _group_rms(kv_cmp, gmean_ref[0:LANES, 0:LANES]) * gc_ref[...]
    y = _rope(jnp.where(key, normed, kv_cmp), cc_ref[...], csu_ref[...], csd_ref[...])
    kcmp_ref[...] = jnp.where(key, y, 0.0).astype(BF16)
    vcmp_ref[...] = jnp.where(key, pltpu.roll(y, HEAD_DIM, axis=1), _ones_lane(y.shape)).astype(BF16)


def _rope_tables(pos):
    inv = ROPE_THETA ** (-jnp.arange(ROPE_HALF, dtype=F32) * 2.0 / ROPE_DIM)
    ang = pos.astype(F32)[:, None] * inv[None, :]
    cos, sin = jnp.cos(ang), jnp.sin(ang)
    n = pos.shape[0]
    zero = jnp.zeros((n, ROPE_HALF), F32)
    rest0 = jnp.zeros((n, HEAD_DIM - ROPE_DIM), F32)
    c = jnp.concatenate([cos, cos, rest0 + 1.0], axis=1)
    s_up = jnp.concatenate([-sin, zero, rest0], axis=1)
    s_dn = jnp.concatenate([zero, sin, rest0], axis=1)
    return c, s_up, s_dn


def _identity_tables(n):
    return jnp.ones((n, HEAD_DIM), F32), jnp.zeros((n, HEAD_DIM), F32), jnp.zeros((n, HEAD_DIM), F32)


def _nsa_prep(nkv, kc, vc, g_kc, g_ks, g_kw, pos_k, k_w1, k_w2, pos_v, v_w1, v_w2, b, t):
    nb = t // CMP_STRIDE
    assert HEAD_DIM + t // SEL_LEN <= LANES
    flat = CMP_STRIDE * HEAD_DIM
    kc2 = kc.reshape(b * nb, flat)
    vc2 = vc.reshape(b * nb, flat)
    rope_t = _rope_tables(jnp.arange(t))
    iden_t = _identity_tables(t)
    tabs = [jnp.concatenate([r, i, r, i], axis=1) for r, i in zip(rope_t, iden_t)]
    rope_c = _rope_tables(jnp.arange(nb) * CMP_STRIDE + CMP_LEN - 1)
    iden_c = _identity_tables(nb)
    tabs_c = [jnp.concatenate([r, i], axis=1) for r, i in zip(rope_c, iden_c)]
    one = jnp.ones((HEAD_DIM,), F32)
    gkv = jnp.concatenate([g_ks, one, g_kw, one]).reshape(1, W_MIX)
    nmask = jnp.concatenate([one, 0 * one, one, 0 * one]).reshape(1, W_MIX)
    gc = jnp.concatenate([g_kc, one]).reshape(1, LANES)
    zpad = jnp.zeros_like(k_w2)
    kw2 = jnp.concatenate([k_w2, zpad], axis=1).astype(BF16)
    vw2 = jnp.concatenate([zpad, v_w2], axis=1).astype(BF16)
    const = lambda shape: pl.BlockSpec(shape, lambda i: (0,) * len(shape))
    seq = lambda width: pl.BlockSpec((t, width), lambda i: (i, 0))
    cmp_in = pl.BlockSpec((nb, flat), lambda i: (i, 0))
    cmp_out = pl.BlockSpec((nb, LANES), lambda i: (i, 0))
    return pl.pallas_call(
        _nsa_prep_kernel,
        grid=(b,),
        in_specs=[seq(W_MIX), cmp_in, cmp_in, const((1, 2 * flat)), const((1, 2 * flat)),
                  const(k_w1.shape), const(v_w1.shape), const(kw2.shape), const(vw2.shape),
                  const((1, W_MIX)), const((1, W_MIX)), const((W_MIX, W_MIX)),
                  const((t, W_MIX)), const((t, W_MIX)), const((t, W_MIX)),
                  const((1, LANES)), const((nb, LANES)), const((nb, LANES)), const((nb, LANES))],
        out_specs=[seq(LANES)] * 4 + [cmp_out] * 2,
        out_shape=[jax.ShapeDtypeStruct((b * t, LANES), BF16)] * 4 + [jax.ShapeDtypeStruct((b * nb, LANES), BF16)] * 2,
        compiler_params=_params("parallel"),
        name="nsa_prep",
    )(nkv, kc2, vc2, pos_k.reshape(1, 2 * flat), pos_v.reshape(1, 2 * flat),
      k_w1.astype(BF16), v_w1.astype(BF16), kw2, vw2, gkv, nmask, _group_mean_matrix(W_MIX),
      *tabs, gc, *tabs_c)


def _nsa_kernel(q_ref, misc_ref, gb_ref, gq_ref, gmean_ref, c_ref, su_ref, sd_ref, ovt_ref,
                kcmp_ref, vcmp_ref, ks_ref, vs_ref, kw_ref, vw_ref, o_ref, *, n_top):
    tq = q_ref.shape[0]
    qi = pl.program_id(1)
    t0 = qi * tq
    nb = kcmp_ref.shape[0]
    nsel = ks_ref.shape[0] // SEL_LEN
    rows = N_HEADS * tq
    stack = lambda x: jnp.concatenate([x] * N_HEADS, axis=0)

    qn = _rope(_group_rms(q_ref[...], gmean_ref[...]) * gq_ref[...], c_ref[...], su_ref[...], sd_ref[...]) * ATTN_SCALE
    q_cols = [_head_column(qn, h) for h in range(N_HEADS)]
    lane_q = lax.broadcasted_iota(jnp.int32, (tq, LANES), 1)

    q0 = jnp.concatenate([jnp.where(lane_q < HEAD_DIM, col, 0.0) for col in q_cols], axis=0).astype(BF16)
    cmp_end = lax.broadcasted_iota(jnp.int32, (rows, nb), 1) * CMP_STRIDE + (CMP_LEN - 1)
    qpos_c = (lax.broadcasted_iota(jnp.int32, (rows, nb), 0) & (tq - 1)) + t0
    m_cmp = cmp_end <= qpos_c
    s = jnp.where(m_cmp, _dot_nt(q0, kcmp_ref[...]), NEG_INF)
    e = jnp.where(m_cmp, jnp.exp(s - jnp.max(s, axis=-1, keepdims=True)), 0.0)
    denom = jnp.sum(e, axis=-1, keepdims=True)
    p_cmp = e / jnp.where(denom > 0.0, denom, 1.0)
    o_cmp = _dot(p_cmp.astype(BF16), vcmp_ref[...])

    p_sum = p_cmp[0:tq] + p_cmp[tq:2 * tq] + p_cmp[2 * tq:3 * tq] + p_cmp[3 * tq:4 * tq]
    nsp = -(-nsel // 8) * 8
    imp = lax.dot_general(ovt_ref[0:nsp, :], p_sum, (((1,), (1,)), ((), ())), precision=HIGHEST,
                          preferred_element_type=F32)
    blk = lax.broadcasted_iota(jnp.int32, (nsp, tq), 0)
    cur = (lax.broadcasted_iota(jnp.int32, (nsp, tq), 1) + t0) // SEL_LEN
    valid = blk <= cur
    forced = ((blk == 0) | (blk == cur) | (blk == cur - 1)).astype(F32)
    imp = jnp.where(valid, imp + SEL_FORCE * forced, NEG_INF)
    cur_row = (lax.broadcasted_iota(jnp.int32, (1, tq), 1) + t0) // SEL_LEN
    bias_t = jnp.where(blk < nsel, NEG_INF, 0.0)
    for j in range(nsel):
        mine = imp[j:j + 1, :]
        beats = (imp > mine) | ((imp == mine) & (blk < j))
        count = jnp.sum(jnp.where(beats & valid, 1.0, 0.0), axis=0, keepdims=True)
        keep = (count < n_top) & (cur_row >= j)
        bias_t = jnp.where(blk == j, jnp.where(keep, 0.0, NEG_INF), bias_t)
    if nsp < LANES:
        bias_t = jnp.concatenate([bias_t, jnp.zeros((LANES - nsp, tq), F32)], axis=0)
    bias = jnp.concatenate([bias_t[:, c * LANES:(c + 1) * LANES].T for c in range(tq // LANES)], axis=0)
    bias = pltpu.roll(bias, HEAD_DIM, axis=1)
    q = jnp.concatenate([jnp.where(lane_q < HEAD_DIM, col, bias) for col in q_cols], axis=0).astype(BF16)

    trow = lax.broadcasted_iota(jnp.int32, (rows, tq), 0) & (tq - 1)
    tcol = lax.broadcasted_iota(jnp.int32, (rows, tq), 1)
    causal_bias = jnp.where(tcol <= trow, 0.0, NEG_INF)

    def tile(j):
        return pl.ds(pl.multiple_of(j * tq, tq), tq)

    def sel_step(kj, extra_bias, st):
        m, acc = st
        s = _dot_nt(q, ks_ref[tile(kj), :])
        if extra_bias is not None:
            s = s + extra_bias
        m_new = jnp.maximum(m, jnp.max(s, axis=-1, keepdims=True))
        acc = jnp.exp(m - m_new) * acc + _dot(jnp.exp(s - m_new).astype(BF16), vs_ref[tile(kj), :])
        return m_new, acc

    st = (jnp.full((rows, 1), NEG_INF, F32), jnp.zeros((rows, LANES), F32))
    st = lax.fori_loop(0, qi, lambda kj, st: sel_step(kj, None, st), st)
    _, acc = sel_step(qi, causal_bias, st)
    o_sel = acc / acc[:, ONE_LANE:ONE_LANE + 1]

    span = WINDOW // tq
    parts = []
    for d in range(span, -1, -1):
        kj = jnp.maximum(qi - d, 0)
        if d == 0:
            bias_w = causal_bias
        elif d == span:
            bias_w = jnp.where((tcol > trow) & (qi >= d), 0.0, NEG_INF)
        else:
            bias_w = jnp.where(qi >= d, 0.0, NEG_INF)
        parts.append((_dot_nt(q, kw_ref[tile(kj), :]) + bias_w, kj))
    m = functools.reduce(jnp.maximum, [jnp.max(s, axis=-1, keepdims=True) for s, _ in parts])
    acc = sum(_dot(jnp.exp(s - m).astype(BF16), vw_ref[tile(kj), :]) for s, kj in parts)
    o_win = acc / acc[:, ONE_LANE:ONE_LANE + 1]

    gate = jax.nn.sigmoid(misc_ref[...] + gb_ref[...])
    outs = []
    for h in range(N_HEADS):
        hr = slice(h * tq, (h + 1) * tq)
        g0 = _MISC_GATE + 3 * h
        outs.append(gate[:, g0:g0 + 1] * o_cmp[hr] + gate[:, g0 + 1:g0 + 2] * o_sel[hr] + gate[:, g0 + 2:g0 + 3] * o_win[hr])
    for hp in range(N_HEADS // 2):
        o_ref[:, hp * LANES:(hp + 1) * LANES] = _pair_heads(outs[2 * hp], outs[2 * hp + 1])


def _cmp_to_sel_overlap_t(nb, nsel):
    nc = nb - 1
    cs = np.arange(nc) * CMP_STRIDE
    ss = np.arange(nsel) * SEL_LEN
    ov = np.clip(np.minimum(cs[:, None] + CMP_LEN, ss[None, :] + SEL_LEN) - np.maximum(cs[:, None], ss[None, :]), 0, None)
    out = np.zeros((LANES, nb), np.float32)
    out[:nsel, :nc] = (ov / CMP_LEN).T
    return jnp.asarray(out)


def _nsa(nq, misc, gate_b, g_q, ks, vs, kw, vw, kcmp, vcmp, b, t):
    tq = min(NSA_TILE, t)
    nb = t // CMP_STRIDE
    nsel = t // SEL_LEN
    assert HEAD_DIM + nsel <= LANES and nb % 8 == 0 and WINDOW % tq == 0 and tq & (tq - 1) == 0 and tq % LANES == 0
    gb = jnp.zeros((1, LANES), F32).at[0, _MISC_GATE:_MISC_GATE + 3 * N_HEADS].set(gate_b)
    gq = jnp.tile(g_q, N_HEADS).reshape(1, W_MIX)
    tabs = [jnp.tile(x, (1, N_HEADS)) for x in _rope_tables(jnp.arange(t))]
    nq_tiles = t // tq
    const = lambda shape: pl.BlockSpec(shape, lambda i, j: (0,) * len(shape))
    qtile = lambda width: pl.BlockSpec((tq, width), lambda i, j: (i * nq_tiles + j, 0))
    ptile = pl.BlockSpec((tq, W_MIX), lambda i, j: (j, 0))
    seq = pl.BlockSpec((t, LANES), lambda i, j: (i, 0))
    cmp = pl.BlockSpec((nb, LANES), lambda i, j: (i, 0))
    return pl.pallas_call(
        functools.partial(_nsa_kernel, n_top=min(SEL_TOP, nsel)),
        grid=(b, nq_tiles),
        in_specs=[qtile(W_MIX), qtile(LANES), const((1, LANES)), const((1, W_MIX)), const((W_MIX, W_MIX)),
                  ptile, ptile, ptile, const((LANES, nb)), cmp, cmp, seq, seq, seq, seq],
        out_specs=qtile(W_MIX),
        out_shape=jax.ShapeDtypeStruct((b * t, W_MIX), F32),
        compiler_params=_params("parallel", "arbitrary"),
        name="nsa",
    )(nq, misc, gb, gq, _group_mean_matrix(W_MIX), *tabs, _cmp_to_sel_overlap_t(nb, nsel),
      kcmp, vcmp, ks, vs, kw, vw)


def kernel(x, ffn1_norm, ffn1_w1, ffn1_w3, ffn1_w2, mix_norm, w_in, w_out, fox_f_bias, fox_q_norm, fox_k_norm, gmlp_v_norm, gmlp_w_s, gmlp_b_s, nsa_q_norm, nsa_kc_norm, nsa_ks_norm, nsa_kw_norm, nsa_cmp_pos_k, nsa_cmp_k_w1, nsa_cmp_k_w2, nsa_cmp_pos_v, nsa_cmp_v_w1, nsa_cmp_v_w2, nsa_gate_bias, pool_w, pool_scale, ffn2_norm, ffn2_w1, ffn2_w3, ffn2_w2):
    b, t, d = x.shape
    assert t % GMLP_CHUNK == 0 and t >= CMP_LEN
    n = b * t
    xf = x.reshape(n, d)
    for l in range(ffn1_norm.shape[0]):
        xf = _ffn(xf, ffn1_norm[l], ffn1_w1[l], ffn1_w3[l], ffn1_w2[l])
        fqkv, guv, nq, nkv, pz, kc, vc, misc = _inproj(xf, mix_norm[l], w_in[l])
        o_a = _fox(fqkv, misc, fox_f_bias[l], fox_q_norm[l], fox_k_norm[l], b, t)
        o_b = _gmlp(guv, gmlp_v_norm[l], gmlp_w_s[l], gmlp_b_s[l], n)
        ks, vs, kw, vw, kcmp, vcmp = _nsa_prep(
            nkv, kc, vc, nsa_kc_norm[l], nsa_ks_norm[l], nsa_kw_norm[l],
            nsa_cmp_pos_k[l], nsa_cmp_k_w1[l], nsa_cmp_k_w2[l],
            nsa_cmp_pos_v[l], nsa_cmp_v_w1[l], nsa_cmp_v_w2[l], b, t)
        o_c = _nsa(nq, misc, nsa_gate_bias[l], nsa_q_norm[l], ks, vs, kw, vw, kcmp, vcmp, b, t)
        o_d = _pool(pz, pool_w[l], pool_scale[l], b, t)
        xf = _outproj(xf, o_a, o_b, o_c, o_d, w_out[l])
        xf = _ffn(xf, ffn2_norm[l], ffn2_w1[l], ffn2_w3[l], ffn2_w2[l])
    return xf.reshape(b, t, d)
```

```python
import functools

import numpy as np
import jax
import jax.numpy as jnp
from jax import lax
from jax.experimental import pallas as pl
from jax.experimental.pallas import tpu as pltpu

F32 = jnp.float32
BF16 = jnp.bfloat16
HIGHEST = lax.Precision.HIGHEST

HEAD_DIM = 64
N_HEADS = 4
W_MIX = N_HEADS * HEAD_DIM
ROPE_THETA = 500000.0
ROPE_DIM = HEAD_DIM // 4
ROPE_HALF = ROPE_DIM // 2
GMLP_CHUNK = 128
CMP_LEN = 32
CMP_STRIDE = 16
SEL_LEN = 64
SEL_TOP = 16
WINDOW = 512
POOL_SIZES = (2, 4, 8, 16)
FFN_RES_WEIGHT = 0.5
EPS = 1e-6
NEG_INF = -1e30
SEL_FORCE = 1e3
ATTN_SCALE = HEAD_DIM ** -0.5
LOG2E = 1.4426950408889634

LANES = 128
VMEM_LIMIT = 52 * 1024 * 1024

FFN_TOKENS = 1024
FFN_MIX_TOKENS = 1024
FFN_HIDDEN = 256
FOX_PREP_ROWS = 256
FOX_TILE = 512
NSA_TILE = 512
NSA_WINDOW_TILE = 512
ONE_LANE = HEAD_DIM


def _params(*sem):
    return pltpu.CompilerParams(dimension_semantics=sem, vmem_limit_bytes=VMEM_LIMIT)


def _dot(a, b, **kw):
    return jnp.dot(a, b, preferred_element_type=F32, **kw)


def _dot_nt(a, b):
    return lax.dot_general(a, b, (((1,), (1,)), ((), ())), preferred_element_type=F32)


def _rms(x):
    return x * lax.rsqrt(jnp.mean(x * x, axis=-1, keepdims=True) + EPS)


def _group_rms(x, gmean):
    sq = x * x
    hi = sq.astype(BF16)
    lo = (sq - hi.astype(F32)).astype(BF16)
    ms = _dot(hi, gmean) + _dot(lo, gmean)
    return x * lax.rsqrt(ms + EPS)


def _rope(x, c, s_up, s_dn):
    w = x.shape[-1]
    return x * c + pltpu.roll(x, w - ROPE_HALF, axis=1) * s_up + pltpu.roll(x, ROPE_HALF, axis=1) * s_dn


def _gelu(x):
    return 0.5 * x * (1.0 + jnp.tanh(0.7978845608028654 * (x + 0.044715 * (x * x * x))))


def _log_sigmoid(x):
    return jnp.minimum(x, 0.0) - jnp.log(1.0 + jnp.exp(-jnp.abs(x)))


def _head_column(x, h):
    col = x[:, (h // 2) * LANES:(h // 2 + 1) * LANES]
    return pltpu.roll(col, HEAD_DIM, axis=1) if h % 2 else col


def _pair_heads(even, odd):
    lane = lax.broadcasted_iota(jnp.int32, even.shape, 1)
    return jnp.where(lane < HEAD_DIM, even, pltpu.roll(odd, HEAD_DIM, axis=1))


def _ones_lane(shape):
    return jnp.where(lax.broadcasted_iota(jnp.int32, shape, 1) == ONE_LANE, 1.0, 0.0)


def _ffn_kernel(x_ref, g_ref, w1_ref, w3_ref, w2_ref, *rest):
    *mix_refs, o_ref = rest
    x = x_ref[...]
    if mix_refs:
        *mixers, wout_ref = mix_refs
        for k, ref in enumerate(mixers):
            x = x + _dot(ref[...], wout_ref[k * W_MIX:(k + 1) * W_MIX, :])
    h = (_rms(x) * g_ref[...]).astype(BF16)
    acc = jnp.zeros_like(x)
    for c in range(w1_ref.shape[1] // FFN_HIDDEN):
        cs = slice(c * FFN_HIDDEN, (c + 1) * FFN_HIDDEN)
        a = _dot(h, w1_ref[:, cs])
        b = _dot(h, w3_ref[:, cs])
        acc = acc + _dot((a * jax.nn.sigmoid(a) * b).astype(BF16), w2_ref[cs, :])
    o_ref[...] = x + FFN_RES_WEIGHT * acc


def _ffn(x, g, w1, w3, w2, mixers=(), w_out=None):
    n, d = x.shape
    tm = min(FFN_MIX_TOKENS if mixers else FFN_TOKENS, n)
    resident = lambda shape: pl.BlockSpec(shape, lambda i: (0, 0), pipeline_mode=pl.Buffered(1))
    mix_specs = [pl.BlockSpec((tm, W_MIX), lambda i: (i, 0)) for _ in mixers] + ([resident(w_out.shape)] if mixers else [])
    mix_args = list(mixers) + ([w_out.astype(BF16)] if mixers else [])
    return pl.pallas_call(
        _ffn_kernel,
        grid=(n // tm,),
        in_specs=[pl.BlockSpec((tm, d), lambda i: (i, 0)), resident((1, d)),
                  resident(w1.shape), resident(w3.shape), resident(w2.shape)] + mix_specs,
        out_specs=pl.BlockSpec((tm, d), lambda i: (i, 0)),
        out_shape=jax.ShapeDtypeStruct((n, d), F32),
        compiler_params=_params("parallel"),
        name="ffn_mix" if mixers else "ffn",
    )(x, g.reshape(1, d), w1.astype(BF16), w3.astype(BF16), w2.astype(BF16), *mix_args)


_IN_GROUPS = (768, 512, 256, 256, 256, 128, 128)
_MISC_FORGET = 0
_MISC_GATE = 4


def _inproj_kernel(x_ref, g_ref, w_ref, fqkv_ref, guv_ref, nq_ref, nkv_ref, pz_ref, kc_ref, vc_ref, misc_ref):
    h = (_rms(x_ref[...]) * g_ref[...]).astype(BF16)
    off = 0
    for ref, width in zip((fqkv_ref, guv_ref, nq_ref, nkv_ref, pz_ref), _IN_GROUPS[:5]):
        ref[...] = _dot(h, w_ref[:, off:off + width])
        off += width
    tail = _dot(h, w_ref[:, off:off + 2 * LANES])
    kc_ref[...] = tail[:, :HEAD_DIM]
    vc_ref[...] = tail[:, HEAD_DIM:LANES]
    misc_ref[...] = tail[:, LANES:]


def _relayout_w_in(w_in):
    o = np.cumsum((0, 256, 256, 256, 4, 256, 256, 256, 64, 64, 64, 64, 64, 64, 12, 256))
    fq, ff, gu, nq, nkc, nks, ng, pz, end = o[0], o[3], o[4], o[6], o[7], o[9], o[13], o[14], o[15]
    d = w_in.shape[0]
    pad = jnp.zeros((d, LANES - 16), w_in.dtype)
    return jnp.concatenate([
        w_in[:, fq:ff], w_in[:, gu:nq], w_in[:, nq:nkc], w_in[:, nks:ng], w_in[:, pz:end],
        w_in[:, nkc:nks], w_in[:, ff:gu], w_in[:, ng:pz], pad], axis=1)


def _inproj(x, g, w_in):
    n, d = x.shape
    tm = min(FFN_TOKENS, n)
    w = _relayout_w_in(w_in).astype(BF16)
    widths = (768, 512, 256, 256, 256, HEAD_DIM, HEAD_DIM, LANES)
    return pl.pallas_call(
        _inproj_kernel,
        grid=(n // tm,),
        in_specs=[
            pl.BlockSpec((tm, d), lambda i: (i, 0)),
            pl.BlockSpec((1, d), lambda i: (0, 0)),
            pl.BlockSpec(w.shape, lambda i: (0, 0)),
        ],
        out_specs=[pl.BlockSpec((tm, wd), lambda i: (i, 0)) for wd in widths],
        out_shape=[jax.ShapeDtypeStruct((n, wd), F32) for wd in widths],
        compiler_params=_params("parallel"),
        name="inproj",
    )(x, g.reshape(1, d), w)


def _bf16_pieces(x):
    p1 = x.astype(BF16).astype(F32)
    r1 = x - p1
    p2 = r1.astype(BF16).astype(F32)
    p3 = (r1 - p2).astype(BF16).astype(F32)
    return p1, p2, p3


_N_PIECES = 3
_N_EXTRA = _N_PIECES * N_HEADS
_PACK_ONE = _N_EXTRA


def _fox_placement():
    pq = np.zeros((LANES, LANES), np.float32)
    pk = np.zeros((LANES, LANES), np.float32)
    for h in range(N_HEADS):
        for p in range(_N_PIECES):
            pq[p * N_HEADS + h, HEAD_DIM + _N_PIECES * h + p] = 1.0
            pk[p * N_HEADS + h, HEAD_DIM + _N_EXTRA + _N_PIECES * h + p] = -1.0
    pq[_PACK_ONE, HEAD_DIM + _N_EXTRA:HEAD_DIM + 2 * _N_EXTRA] = 1.0
    pk[_PACK_ONE, HEAD_DIM:HEAD_DIM + _N_EXTRA] = 1.0
    return jnp.asarray(pq, BF16), jnp.asarray(pk, BF16)


def _fox_kernel(qkv_ref, misc_ref, fb_ref, gq_ref, gk_ref, gmean_ref, pq_ref, pk_ref, o_ref, q_s, k_s, v_s, o_s):
    t = qkv_ref.shape[0]
    pr = min(FOX_PREP_ROWS, t)
    gmean = gmean_ref[...]
    tri = (lax.broadcasted_iota(jnp.int32, (pr, pr), 1) <= lax.broadcasted_iota(jnp.int32, (pr, pr), 0)).astype(BF16)
    lane = lax.broadcasted_iota(jnp.int32, (pr, LANES), 1)
    extra = lane - HEAD_DIM
    own = [((extra >= _N_PIECES * h) & (extra < _N_PIECES * (h + 1)))
           | ((extra >= _N_EXTRA + _N_PIECES * h) & (extra < _N_EXTRA + _N_PIECES * (h + 1))) for h in range(N_HEADS)]
    ones_v = jnp.where(extra == 0, 1.0, 0.0)
    carry = jnp.zeros((1, LANES), F32)
    for r in range(t // pr):
        sl = slice(r * pr, (r + 1) * pr)
        log_f = _log_sigmoid(misc_ref[sl, :] + fb_ref[...]) * LOG2E
        c = sum(_dot(tri, piece.astype(BF16)) for piece in _bf16_pieces(log_f)) + carry
        carry = c[pr - 1:pr, :]
        p1, p2, p3 = _bf16_pieces(c)
        packed = jnp.where(lane < N_HEADS, p1, jnp.where(lane < 2 * N_HEADS, pltpu.roll(p2, N_HEADS, axis=1), jnp.where(
            lane < _N_EXTRA, pltpu.roll(p3, 2 * N_HEADS, axis=1), jnp.where(lane == _PACK_ONE, 1.0, 0.0)))).astype(BF16)
        q_extra = _dot(packed, pq_ref[...])
        k_extra = _dot(packed, pk_ref[...])
        qn = _group_rms(qkv_ref[sl, 0:W_MIX], gmean) * gq_ref[...] * (ATTN_SCALE * LOG2E)
        kn = _group_rms(qkv_ref[sl, W_MIX:2 * W_MIX], gmean) * gk_ref[...]
        v = qkv_ref[sl, 2 * W_MIX:3 * W_MIX]
        for h in range(N_HEADS):
            q_s[h, sl, :] = jnp.where(extra < 0, _head_column(qn, h), q_extra).astype(BF16)
            k_s[h, sl, :] = jnp.where(extra < 0, _head_column(kn, h), jnp.where(own[h], k_extra, 0.0)).astype(BF16)
            v_s[h, sl, :] = jnp.where(extra < 0, _head_column(v, h), ones_v).astype(BF16)

    tq = min(FOX_TILE, t)
    causal = lax.broadcasted_iota(jnp.int32, (tq, tq), 1) <= lax.broadcasted_iota(jnp.int32, (tq, tq), 0)

    def head_body(h, _):
        for qi in range(t // tq):
            dsl = slice(qi * tq, (qi + 1) * tq)
            q = q_s[h, dsl, :]
            s_d = jnp.where(causal, _dot_nt(q, k_s[h, dsl, :]), NEG_INF)
            m = jnp.max(s_d, axis=-1, keepdims=True)
            if qi:
                s_o = _dot_nt(q, k_s[h, 0:qi * tq, :])
                m = jnp.maximum(m, jnp.max(s_o, axis=-1, keepdims=True))
            acc = _dot(jnp.exp2(s_d - m).astype(BF16), v_s[h, dsl, :])
            if qi:
                acc = acc + _dot(jnp.exp2(s_o - m).astype(BF16), v_s[h, 0:qi * tq, :])
            o_s[h, dsl, :] = acc / acc[:, ONE_LANE:ONE_LANE + 1]
        return 0

    lax.fori_loop(0, N_HEADS, head_body, 0)
    for hp in range(N_HEADS // 2):
        o_ref[:, hp * LANES:(hp + 1) * LANES] = _pair_heads(o_s[2 * hp], o_s[2 * hp + 1]).astype(o_ref.dtype)


def _group_mean_matrix(width):
    g = np.kron(np.eye(width // HEAD_DIM, dtype=np.float32), np.full((HEAD_DIM, HEAD_DIM), 1.0 / HEAD_DIM, np.float32))
    return jnp.asarray(g, BF16)


def _fox(fqkv, misc, f_bias, g_q, g_k, b, t):
    assert t % min(FOX_TILE, t) == 0 and t % min(FOX_PREP_ROWS, t) == 0
    fb = jnp.zeros((1, LANES), F32).at[0, _MISC_FORGET:_MISC_FORGET + N_HEADS].set(f_bias)
    gq = jnp.tile(g_q, N_HEADS).reshape(1, W_MIX)
    gk = jnp.tile(g_k, N_HEADS).reshape(1, W_MIX)
    const = lambda shape: pl.BlockSpec(shape, lambda i: (0,) * len(shape))
    head_scratch = lambda dtype: pltpu.VMEM((N_HEADS, t, LANES), dtype)
    return pl.pallas_call(
        _fox_kernel,
        grid=(b,),
        in_specs=[pl.BlockSpec((t, 3 * W_MIX), lambda i: (i, 0)), pl.BlockSpec((t, LANES), lambda i: (i, 0)),
                  const((1, LANES)), const((1, W_MIX)), const((1, W_MIX)), const((W_MIX, W_MIX)),
                  const((LANES, LANES)), const((LANES, LANES))],
        out_specs=pl.BlockSpec((t, W_MIX), lambda i: (i, 0)),
        out_shape=jax.ShapeDtypeStruct((b * t, W_MIX), BF16),
        scratch_shapes=[head_scratch(BF16), head_scratch(BF16), head_scratch(BF16), head_scratch(F32)],
        compiler_params=_params("parallel"),
        name="fox",
    )(fqkv, misc, fb, gq, gk, _group_mean_matrix(W_MIX), *_fox_placement())


def _gmlp_kernel(uv_ref, gv_ref, gmean_ref, w_ref, bias_ref, o_ref):
    c = GMLP_CHUNK
    rows = lax.broadcasted_iota(jnp.int32, (c, c), 0)
    cols = lax.broadcasted_iota(jnp.int32, (c, c), 1)
    lane_group = lax.broadcasted_iota(jnp.int32, (c, W_MIX), 1) // HEAD_DIM
    w_tril = [jnp.where(cols <= rows, w_ref[g], 0.0).astype(BF16) for g in range(N_HEADS)]
    for r in range(uv_ref.shape[0] // c):
        sl = slice(r * c, (r + 1) * c)
        u = _gelu(uv_ref[sl, 0:W_MIX])
        v = _group_rms(_gelu(uv_ref[sl, W_MIX:2 * W_MIX]), gmean_ref[...]) * gv_ref[...]
        s = bias_ref[...]
        for g in range(N_HEADS):
            s = s + _dot(w_tril[g], jnp.where(lane_group == g, v, 0.0).astype(BF16))
        o_ref[sl, :] = (u * s).astype(o_ref.dtype)


def _gmlp(guv, g_v, w_s, b_s, n):
    tt = min(1024, n)
    bias = jnp.repeat(b_s.T, HEAD_DIM, axis=1)
    const = lambda shape: pl.BlockSpec(shape, lambda i: (0,) * len(shape))
    return pl.pallas_call(
        _gmlp_kernel,
        grid=(n // tt,),
        in_specs=[pl.BlockSpec((tt, 2 * W_MIX), lambda i: (i, 0)), const((1, W_MIX)), const((W_MIX, W_MIX)),
                  const(w_s.shape), const(bias.shape)],
        out_specs=pl.BlockSpec((tt, W_MIX), lambda i: (i, 0)),
        out_shape=jax.ShapeDtypeStruct((n, W_MIX), BF16),
        compiler_params=_params("parallel"),
        name="gmlp",
    )(guv, g_v.reshape(1, W_MIX), _group_mean_matrix(W_MIX), w_s, bias)


def _pool_kernel(z_ref, w_ref, scale_ref, o_ref):
    t = z_ref.shape[0]
    z = z_ref[...]
    row = lax.broadcasted_iota(jnp.int32, (t, W_MIX), 0)
    lane_group = lax.broadcasted_iota(jnp.int32, (t, W_MIX), 1) // HEAD_DIM
    sums = []
    s = z
    k = 1
    while k < max(POOL_SIZES):
        s = s + jnp.where(row >= k, pltpu.roll(s, k, axis=0), 0.0)
        k *= 2
        sums.append((k, s))
    win_sum = jnp.zeros_like(z)
    win = jnp.zeros((t, W_MIX), jnp.int32)
    for g, size in enumerate(POOL_SIZES):
        win_sum = jnp.where(lane_group == g, dict(sums)[size], win_sum)
        win = jnp.where(lane_group == g, size, win)
    cnt = jnp.minimum(row + 1, win).astype(F32)
    pooled = win_sum / cnt - z
    o_ref[...] = (_dot(pooled.astype(BF16), w_ref[...]) * scale_ref[...]).astype(o_ref.dtype)


def _pool(pz, w_p, scale, b, t):
    w_bd = jax.scipy.linalg.block_diag(*[w_p[g] for g in range(N_HEADS)]).astype(BF16)
    const = lambda shape: pl.BlockSpec(shape, lambda i: (0,) * len(shape))
    return pl.pallas_call(
        _pool_kernel,
        grid=(b,),
        in_specs=[pl.BlockSpec((t, W_MIX), lambda i: (i, 0)), const((W_MIX, W_MIX)), const((1, W_MIX))],
        out_specs=pl.BlockSpec((t, W_MIX), lambda i: (i, 0)),
        out_shape=jax.ShapeDtypeStruct((b * t, W_MIX), BF16),
        compiler_params=_params("parallel"),
        name="pool",
    )(pz, w_bd, scale.reshape(1, W_MIX))


def _nsa_prep_kernel(kv_ref, kc_ref, vc_ref, posk_ref, posv_ref, kw1_ref, vw1_ref, kw2_ref, vw2_ref,
                     gkv_ref, nmask_ref, gmean_ref, c_ref, su_ref, sd_ref, gc_ref, cc_ref, csu_ref, csd_ref,
                     ks_ref, vs_ref, kw_ref, vw_ref, kcmp_ref, vcmp_ref):
    t = kv_ref.shape[0]
    x = kv_ref[...]
    normed = _group_rms(x, gmean_ref[...]) * gkv_ref[...]
    y = _rope(jnp.where(nmask_ref[...] > 0.5, normed, x), c_ref[...], su_ref[...], sd_ref[...])
    lane = lax.broadcasted_iota(jnp.int32, (t, LANES), 1)
    block = lax.broadcasted_iota(jnp.int32, (t, LANES), 0) // SEL_LEN
    key = lane < HEAD_DIM
    ks_v, kw_v = y[:, 0:LANES], y[:, LANES:2 * LANES]
    ks_ref[...] = jnp.where(key, ks_v, jnp.where(lane - HEAD_DIM == block, 1.0, 0.0)).astype(BF16)
    vs_ref[...] = jnp.where(key, pltpu.roll(ks_v, HEAD_DIM, axis=1), _ones_lane((t, LANES))).astype(BF16)
    kw_ref[...] = jnp.where(key, kw_v, 0.0).astype(BF16)
    vw_ref[...] = jnp.where(key, pltpu.roll(kw_v, HEAD_DIM, axis=1), _ones_lane((t, LANES))).astype(BF16)

    half = kc_ref.shape[1]
    nb = kc_ref.shape[0]

    def hidden(x_ref, pos_ref, w1_ref):
        top = _dot((x_ref[...] + pos_ref[:, 0:half]).astype(BF16), w1_ref[0:half, :])
        bot = _dot((x_ref[...] + pos_ref[:, half:2 * half]).astype(BF16), w1_ref[half:2 * half, :])
        return _gelu(top + pltpu.roll(bot, nb - 1, axis=0)).astype(BF16)

    kv_cmp = _dot(hidden(kc_ref, posk_ref, kw1_ref), kw2_ref[...]) + _dot(hidden(vc_ref, posv_ref, vw1_ref), vw2_ref[...])
    key = lax.broadcasted_iota(jnp.int32, kv_cmp.shape, 1) < HEAD_DIM
    normed = _group_rms(kv_cmp, gmean_ref[0:LANES, 0:LANES]) * gc_ref[...]
    y = _rope(jnp.where(key, normed, kv_cmp), cc_ref[...], csu_ref[...], csd_ref[...])
    kcmp_ref[...] = jnp.where(key, y, 0.0).astype(BF16)
    vcmp_ref[...] = jnp.where(key, pltpu.roll(y, HEAD_DIM, axis=1), _ones_lane(y.shape)).astype(BF16)


def _rope_tables(pos):
    inv = ROPE_THETA ** (-jnp.arange(ROPE_HALF, dtype=F32) * 2.0 / ROPE_DIM)
    ang = pos.astype(F32)[:, None] * inv[None, :]
    cos, sin = jnp.cos(ang), jnp.sin(ang)
    n = pos.shape[0]
    zero = jnp.zeros((n, ROPE_HALF), F32)
    rest0 = jnp.zeros((n, HEAD_DIM - ROPE_DIM), F32)
    c = jnp.concatenate([cos, cos, rest0 + 1.0], axis=1)
    s_up = jnp.concatenate([-sin, zero, rest0], axis=1)
    s_dn = jnp.concatenate([zero, sin, rest0], axis=1)
    return c, s_up, s_dn


def _identity_tables(n):
    return jnp.ones((n, HEAD_DIM), F32), jnp.zeros((n, HEAD_DIM), F32), jnp.zeros((n, HEAD_DIM), F32)


def _nsa_prep(nkv, kc, vc, g_kc, g_ks, g_kw, pos_k, k_w1, k_w2, pos_v, v_w1, v_w2, b, t):
    nb = t // CMP_STRIDE
    assert HEAD_DIM + t // SEL_LEN <= LANES
    flat = CMP_STRIDE * HEAD_DIM
    kc2 = kc.reshape(b * nb, flat)
    vc2 = vc.reshape(b * nb, flat)
    rope_t = _rope_tables(jnp.arange(t))
    iden_t = _identity_tables(t)
    tabs = [jnp.concatenate([r, i, r, i], axis=1) for r, i in zip(rope_t, iden_t)]
    rope_c = _rope_tables(jnp.arange(nb) * CMP_STRIDE + CMP_LEN - 1)
    iden_c = _identity_tables(nb)
    tabs_c = [jnp.concatenate([r, i], axis=1) for r, i in zip(rope_c, iden_c)]
    one = jnp.ones((HEAD_DIM,), F32)
    gkv = jnp.concatenate([g_ks, one, g_kw, one]).reshape(1, W_MIX)
    nmask = jnp.concatenate([one, 0 * one, one, 0 * one]).reshape(1, W_MIX)
    gc = jnp.concatenate([g_kc, one]).reshape(1, LANES)
    zpad = jnp.zeros_like(k_w2)
    kw2 = jnp.concatenate([k_w2, zpad], axis=1).astype(BF16)
    vw2 = jnp.concatenate([zpad, v_w2], axis=1).astype(BF16)
    const = lambda shape: pl.BlockSpec(shape, lambda i: (0,) * len(shape))
    seq = lambda width: pl.BlockSpec((t, width), lambda i: (i, 0))
    cmp_in = pl.BlockSpec((nb, flat), lambda i: (i, 0))
    cmp_out = pl.BlockSpec((nb, LANES), lambda i: (i, 0))
    return pl.pallas_call(
        _nsa_prep_kernel,
        grid=(b,),
        in_specs=[seq(W_MIX), cmp_in, cmp_in, const((1, 2 * flat)), const((1, 2 * flat)),
                  const(k_w1.shape), const(v_w1.shape), const(kw2.shape), const(vw2.shape),
                  const((1, W_MIX)), const((1, W_MIX)), const((W_MIX, W_MIX)),
                  const((t, W_MIX)), const((t, W_MIX)), const((t, W_MIX)),
                  const((1, LANES)), const((nb, LANES)), const((nb, LANES)), const((nb, LANES))],
        out_specs=[seq(LANES)] * 4 + [cmp_out] * 2,
        out_shape=[jax.ShapeDtypeStruct((b * t, LANES), BF16)] * 4 + [jax.ShapeDtypeStruct((b * nb, LANES), BF16)] * 2,
        compiler_params=_params("parallel"),
        name="nsa_prep",
    )(nkv, kc2, vc2, pos_k.reshape(1, 2 * flat), pos_v.reshape(1, 2 * flat),
      k_w1.astype(BF16), v_w1.astype(BF16), kw2, vw2, gkv, nmask, _group_mean_matrix(W_MIX),
      *tabs, gc, *tabs_c)


def _nsa_kernel(q_ref, misc_ref, gb_ref, gq_ref, gmean_ref, c_ref, su_ref, sd_ref, ovt_ref,
                kcmp_ref, vcmp_ref, ks_ref, vs_ref, kw_ref, vw_ref, o_ref, *, n_top):
    tq = q_ref.shape[0]
    qi = pl.program_id(1)
    t0 = qi * tq
    nb = kcmp_ref.shape[0]
    nsel = ks_ref.shape[0] // SEL_LEN
    rows = N_HEADS * tq
    stack = lambda x: jnp.concatenate([x] * N_HEADS, axis=0)

    qn = _rope(_group_rms(q_ref[...], gmean_ref[...]) * gq_ref[...], c_ref[...], su_ref[...], sd_ref[...]) * (ATTN_SCALE * LOG2E)
    q_cols = [_head_column(qn, h) for h in range(N_HEADS)]
    lane_q = lax.broadcasted_iota(jnp.int32, (tq, LANES), 1)

    q0 = jnp.concatenate([jnp.where(lane_q < HEAD_DIM, col, 0.0) for col in q_cols], axis=0).astype(BF16)
    cmp_end = lax.broadcasted_iota(jnp.int32, (rows, nb), 1) * CMP_STRIDE + (CMP_LEN - 1)
    qpos_c = (lax.broadcasted_iota(jnp.int32, (rows, nb), 0) & (tq - 1)) + t0
    m_cmp = cmp_end <= qpos_c
    s = jnp.where(m_cmp, _dot_nt(q0, kcmp_ref[...]), NEG_INF)
    e = jnp.where(m_cmp, jnp.exp2(s - jnp.max(s, axis=-1, keepdims=True)), 0.0)
    denom = jnp.sum(e, axis=-1, keepdims=True)
    p_cmp = e / jnp.where(denom > 0.0, denom, 1.0)
    o_cmp = _dot(p_cmp.astype(BF16), vcmp_ref[...])

    p_sum = p_cmp[0:tq] + p_cmp[tq:2 * tq] + p_cmp[2 * tq:3 * tq] + p_cmp[3 * tq:4 * tq]
    nsp = -(-nsel // 8) * 8
    imp = lax.dot_general(ovt_ref[0:nsp, :], p_sum, (((1,), (1,)), ((), ())), precision=HIGHEST,
                          preferred_element_type=F32)
    blk = lax.broadcasted_iota(jnp.int32, (nsp, tq), 0)
    cur = (lax.broadcasted_iota(jnp.int32, (nsp, tq), 1) + t0) // SEL_LEN
    valid = blk <= cur
    forced = ((blk == 0) | (blk == cur) | (blk == cur - 1)).astype(F32)
    imp = jnp.where(valid, imp + SEL_FORCE * forced, NEG_INF)
    cur_row = (lax.broadcasted_iota(jnp.int32, (1, tq), 1) + t0) // SEL_LEN
    outranked = jnp.zeros((nsp, tq), F32)
    for j in range(nsel):
        rival = imp[j:j + 1, :]
        wins = (rival > imp) | ((rival == imp) & (blk > j))
        outranked = outranked + jnp.where(wins & (cur_row >= j), 1.0, 0.0)
    keep = (outranked < n_top) & valid
    bias_t = jnp.where(keep | (blk >= nsel), 0.0, NEG_INF)
    if nsp < LANES:
        bias_t = jnp.concatenate([bias_t, jnp.zeros((LANES - nsp, tq), F32)], axis=0)
    bias = jnp.concatenate([bias_t[:, c * LANES:(c + 1) * LANES].T for c in range(tq // LANES)], axis=0)
    bias = pltpu.roll(bias, HEAD_DIM, axis=1)
    q = jnp.concatenate([jnp.where(lane_q < HEAD_DIM, col, bias) for col in q_cols], axis=0).astype(BF16)

    trow = lax.broadcasted_iota(jnp.int32, (rows, tq), 0) & (tq - 1)
    tcol = lax.broadcasted_iota(jnp.int32, (rows, tq), 1)
    causal_bias = jnp.where(tcol <= trow, 0.0, NEG_INF)

    def tile(j):
        return pl.ds(pl.multiple_of(j * tq, tq), tq)

    def sel_step(kj, extra_bias, st):
        m, acc = st
        s = _dot_nt(q, ks_ref[tile(kj), :])
        if extra_bias is not None:
            s = s + extra_bias
        m_new = jnp.maximum(m, jnp.max(s, axis=-1, keepdims=True))
        acc = jnp.exp2(m - m_new) * acc + _dot(jnp.exp2(s - m_new).astype(BF16), vs_ref[tile(kj), :])
        return m_new, acc

    st = (jnp.full((rows, 1), NEG_INF, F32), jnp.zeros((rows, LANES), F32))
    st = lax.fori_loop(0, qi, lambda kj, st: sel_step(kj, None, st), st)
    _, acc = sel_step(qi, causal_bias, st)
    o_sel = acc / acc[:, ONE_LANE:ONE_LANE + 1]

    wt = min(NSA_WINDOW_TILE, tq)
    span = WINDOW // wt
    wrow = lax.broadcasted_iota(jnp.int32, (N_HEADS * wt, wt), 0) & (wt - 1)
    wcol = lax.broadcasted_iota(jnp.int32, (N_HEADS * wt, wt), 1)
    o_win = []
    for sub in range(tq // wt):
        q_sub = jnp.concatenate([q[h * tq + sub * wt:h * tq + (sub + 1) * wt] for h in range(N_HEADS)], axis=0)
        diag = qi * (tq // wt) + sub
        parts = []
        for d in range(span, -1, -1):
            ksl = pl.ds(pl.multiple_of(jnp.maximum(diag - d, 0) * wt, wt), wt)
            if d == 0:
                bias_w = jnp.where(wcol <= wrow, 0.0, NEG_INF)
            elif d == span:
                bias_w = jnp.where((wcol > wrow) & (diag >= d), 0.0, NEG_INF)
            else:
                bias_w = jnp.where(diag >= d, 0.0, NEG_INF)
            parts.append((_dot_nt(q_sub, kw_ref[ksl, :]) + bias_w, ksl))
        m = functools.reduce(jnp.maximum, [jnp.max(s, axis=-1, keepdims=True) for s, _ in parts])
        acc = sum(_dot(jnp.exp2(s - m).astype(BF16), vw_ref[ksl, :]) for s, ksl in parts)
        o_win.append(acc / acc[:, ONE_LANE:ONE_LANE + 1])

    gate = jax.nn.sigmoid(misc_ref[...] + gb_ref[...])
    outs = []
    for h in range(N_HEADS):
        hr = slice(h * tq, (h + 1) * tq)
        g0 = _MISC_GATE + 3 * h
        o_win_h = jnp.concatenate([part[h * wt:(h + 1) * wt] for part in o_win], axis=0)
        outs.append(gate[:, g0:g0 + 1] * o_cmp[hr] + gate[:, g0 + 1:g0 + 2] * o_sel[hr] + gate[:, g0 + 2:g0 + 3] * o_win_h)
    for hp in range(N_HEADS // 2):
        o_ref[:, hp * LANES:(hp + 1) * LANES] = _pair_heads(outs[2 * hp], outs[2 * hp + 1]).astype(o_ref.dtype)


def _cmp_to_sel_overlap_t(nb, nsel):
    nc = nb - 1
    cs = np.arange(nc) * CMP_STRIDE
    ss = np.arange(nsel) * SEL_LEN
    ov = np.clip(np.minimum(cs[:, None] + CMP_LEN, ss[None, :] + SEL_LEN) - np.maximum(cs[:, None], ss[None, :]), 0, None)
    out = np.zeros((LANES, nb), np.float32)
    out[:nsel, :nc] = (ov / CMP_LEN).T
    return jnp.asarray(out)


def _nsa(nq, misc, gate_b, g_q, ks, vs, kw, vw, kcmp, vcmp, b, t):
    tq = min(NSA_TILE, t)
    nb = t // CMP_STRIDE
    nsel = t // SEL_LEN
    wt = min(NSA_WINDOW_TILE, tq)
    assert HEAD_DIM + nsel <= LANES and nb % 8 == 0 and tq & (tq - 1) == 0 and tq % LANES == 0
    assert WINDOW % wt == 0 and tq % wt == 0 and wt & (wt - 1) == 0
    gb = jnp.zeros((1, LANES), F32).at[0, _MISC_GATE:_MISC_GATE + 3 * N_HEADS].set(gate_b)
    gq = jnp.tile(g_q, N_HEADS).reshape(1, W_MIX)
    tabs = [jnp.tile(x, (1, N_HEADS)) for x in _rope_tables(jnp.arange(t))]
    nq_tiles = t // tq
    const = lambda shape: pl.BlockSpec(shape, lambda i, j: (0,) * len(shape))
    qtile = lambda width: pl.BlockSpec((tq, width), lambda i, j: (i * nq_tiles + j, 0))
    ptile = pl.BlockSpec((tq, W_MIX), lambda i, j: (j, 0))
    seq = pl.BlockSpec((t, LANES), lambda i, j: (i, 0))
    cmp = pl.BlockSpec((nb, LANES), lambda i, j: (i, 0))
    return pl.pallas_call(
        functools.partial(_nsa_kernel, n_top=min(SEL_TOP, nsel)),
        grid=(b, nq_tiles),
        in_specs=[qtile(W_MIX), qtile(LANES), const((1, LANES)), const((1, W_MIX)), const((W_MIX, W_MIX)),
                  ptile, ptile, ptile, const((LANES, nb)), cmp, cmp, seq, seq, seq, seq],
        out_specs=qtile(W_MIX),
        out_shape=jax.ShapeDtypeStruct((b * t, W_MIX), BF16),
        compiler_params=_params("parallel", "arbitrary"),
        name="nsa",
    )(nq, misc, gb, gq, _group_mean_matrix(W_MIX), *tabs, _cmp_to_sel_overlap_t(nb, nsel),
      kcmp, vcmp, ks, vs, kw, vw)


def kernel(x, ffn1_norm, ffn1_w1, ffn1_w3, ffn1_w2, mix_norm, w_in, w_out, fox_f_bias, fox_q_norm, fox_k_norm, gmlp_v_norm, gmlp_w_s, gmlp_b_s, nsa_q_norm, nsa_kc_norm, nsa_ks_norm, nsa_kw_norm, nsa_cmp_pos_k, nsa_cmp_k_w1, nsa_cmp_k_w2, nsa_cmp_pos_v, nsa_cmp_v_w1, nsa_cmp_v_w2, nsa_gate_bias, pool_w, pool_scale, ffn2_norm, ffn2_w1, ffn2_w3, ffn2_w2):
    b, t, d = x.shape
    assert t % GMLP_CHUNK == 0 and t >= CMP_LEN
    n = b * t
    xf = x.reshape(n, d)
    for l in range(ffn1_norm.shape[0]):
        xf = _ffn(xf, ffn1_norm[l], ffn1_w1[l], ffn1_w3[l], ffn1_w2[l])
        fqkv, guv, nq, nkv, pz, kc, vc, misc = _inproj(xf, mix_norm[l], w_in[l])
        o_a = _fox(fqkv, misc, fox_f_bias[l], fox_q_norm[l], fox_k_norm[l], b, t)
        o_b = _gmlp(guv, gmlp_v_norm[l], gmlp_w_s[l], gmlp_b_s[l], n)
        ks, vs, kw, vw, kcmp, vcmp = _nsa_prep(
            nkv, kc, vc, nsa_kc_norm[l], nsa_ks_norm[l], nsa_kw_norm[l],
            nsa_cmp_pos_k[l], nsa_cmp_k_w1[l], nsa_cmp_k_w2[l],
            nsa_cmp_pos_v[l], nsa_cmp_v_w1[l], nsa_cmp_v_w2[l], b, t)
        o_c = _nsa(nq, misc, nsa_gate_bias[l], nsa_q_norm[l], ks, vs, kw, vw, kcmp, vcmp, b, t)
        o_d = _pool(pz, pool_w[l], pool_scale[l], b, t)
        xf = _ffn(xf, ffn2_norm[l], ffn2_w1[l], ffn2_w3[l], ffn2_w2[l], mixers=(o_a, o_b, o_c, o_d), w_out=w_out[l])
    return xf.reshape(b, t, d)
```

```python
import functools
from typing import Callable, NamedTuple

import numpy as np
import jax
import jax.numpy as jnp
from jax import lax
from jax.experimental import pallas as pl
from jax.experimental.pallas import tpu as pltpu

F32 = jnp.float32
BF16 = jnp.bfloat16
HIGHEST = lax.Precision.HIGHEST

HEAD_DIM = 64
N_HEADS = 4
W_MIX = N_HEADS * HEAD_DIM
ROPE_THETA = 500000.0
ROPE_DIM = HEAD_DIM // 4
ROPE_HALF = ROPE_DIM // 2
GMLP_CHUNK = 128
CMP_LEN = 32
CMP_STRIDE = 16
SEL_LEN = 64
SEL_TOP = 16
WINDOW = 512
POOL_SIZES = (2, 4, 8, 16)
FFN_RES_WEIGHT = 0.5
EPS = 1e-6
NEG_INF = -1e30
SEL_FORCE = 1e3
ATTN_SCALE = HEAD_DIM ** -0.5
LOG2E = 1.4426950408889634

LANES = 128
VMEM_LIMIT = 52 * 1024 * 1024

FFN_TOKENS = 1024
FFN_MIX_TOKENS = 1024
FFN_HIDDEN = 256
FOX_PREP_ROWS = 256
FOX_TILE = 256
NSA_TILE = 512
NSA_WINDOW_TILE = 512
ONE_LANE = HEAD_DIM


def _params(*sem):
    return pltpu.CompilerParams(dimension_semantics=sem, vmem_limit_bytes=VMEM_LIMIT)


def _dot(a, b, **kw):
    return jnp.dot(a, b, preferred_element_type=F32, **kw)


def _dot_nt(a, b):
    return lax.dot_general(a, b, (((1,), (1,)), ((), ())), preferred_element_type=F32)


def _rms(x):
    return x * lax.rsqrt(jnp.mean(x * x, axis=-1, keepdims=True) + EPS)


def _group_rms(x, gmean):
    sq = x * x
    hi = sq.astype(BF16)
    lo = (sq - hi.astype(F32)).astype(BF16)
    ms = _dot(hi, gmean) + _dot(lo, gmean)
    return x * lax.rsqrt(ms + EPS)


def _rope(x, c, s_up, s_dn):
    w = x.shape[-1]
    return x * c + pltpu.roll(x, w - ROPE_HALF, axis=1) * s_up + pltpu.roll(x, ROPE_HALF, axis=1) * s_dn


def _gelu(x):
    return 0.5 * x * (1.0 + jnp.tanh(0.7978845608028654 * (x + 0.044715 * (x * x * x))))


def _log_sigmoid(x):
    return jnp.minimum(x, 0.0) - jnp.log(1.0 + jnp.exp(-jnp.abs(x)))


def _head_column(x, h):
    col = x[:, (h // 2) * LANES:(h // 2 + 1) * LANES]
    return pltpu.roll(col, HEAD_DIM, axis=1) if h % 2 else col


def _pair_heads(even, odd):
    lane = lax.broadcasted_iota(jnp.int32, even.shape, 1)
    return jnp.where(lane < HEAD_DIM, even, pltpu.roll(odd, HEAD_DIM, axis=1))


def _ones_lane(shape):
    return jnp.where(lax.broadcasted_iota(jnp.int32, shape, 1) == ONE_LANE, 1.0, 0.0)


def _ffn_kernel(x_ref, g_ref, w1_ref, w3_ref, w2_ref, *rest):
    *mix_refs, o_ref = rest
    x = x_ref[...]
    if mix_refs:
        *mixers, wout_ref = mix_refs
        for k, ref in enumerate(mixers):
            x = x + _dot(ref[...], wout_ref[k * W_MIX:(k + 1) * W_MIX, :])
    h = (_rms(x) * g_ref[...]).astype(BF16)
    acc = jnp.zeros_like(x)
    for c in range(w1_ref.shape[1] // FFN_HIDDEN):
        cs = slice(c * FFN_HIDDEN, (c + 1) * FFN_HIDDEN)
        a = _dot(h, w1_ref[:, cs])
        b = _dot(h, w3_ref[:, cs])
        acc = acc + _dot((a * jax.nn.sigmoid(a) * b).astype(BF16), w2_ref[cs, :])
    o_ref[...] = x + FFN_RES_WEIGHT * acc


def _ffn(x, g, w1, w3, w2, mixers=(), w_out=None):
    n, d = x.shape
    tm = min(FFN_MIX_TOKENS if mixers else FFN_TOKENS, n)
    resident = lambda shape: pl.BlockSpec(shape, lambda i: (0, 0), pipeline_mode=pl.Buffered(1))
    mix_specs = [pl.BlockSpec((tm, W_MIX), lambda i: (i, 0)) for _ in mixers] + ([resident(w_out.shape)] if mixers else [])
    mix_args = list(mixers) + ([w_out.astype(BF16)] if mixers else [])
    return pl.pallas_call(
        _ffn_kernel,
        grid=(n // tm,),
        in_specs=[pl.BlockSpec((tm, d), lambda i: (i, 0)), resident((1, d)),
                  resident(w1.shape), resident(w3.shape), resident(w2.shape)] + mix_specs,
        out_specs=pl.BlockSpec((tm, d), lambda i: (i, 0)),
        out_shape=jax.ShapeDtypeStruct((n, d), F32),
        compiler_params=_params("parallel"),
        name="ffn_mix" if mixers else "ffn",
    )(x, g.reshape(1, d), w1.astype(BF16), w3.astype(BF16), w2.astype(BF16), *mix_args)


_IN_GROUPS = (768, 512, 256, 256, 256, 128, 128)
_MISC_FORGET = 0
_MISC_GATE = 4


def _inproj_kernel(x_ref, g_ref, w_ref, fqkv_ref, guv_ref, nq_ref, nkv_ref, pz_ref, kc_ref, vc_ref, misc_ref):
    h = (_rms(x_ref[...]) * g_ref[...]).astype(BF16)
    off = 0
    for ref, width in zip((fqkv_ref, guv_ref, nq_ref, nkv_ref, pz_ref), _IN_GROUPS[:5]):
        ref[...] = _dot(h, w_ref[:, off:off + width])
        off += width
    tail = _dot(h, w_ref[:, off:off + 2 * LANES])
    kc_ref[...] = tail[:, :HEAD_DIM]
    vc_ref[...] = tail[:, HEAD_DIM:LANES]
    misc_ref[...] = tail[:, LANES:]


def _relayout_w_in(w_in):
    o = np.cumsum((0, 256, 256, 256, 4, 256, 256, 256, 64, 64, 64, 64, 64, 64, 12, 256))
    fq, ff, gu, nq, nkc, nks, ng, pz, end = o[0], o[3], o[4], o[6], o[7], o[9], o[13], o[14], o[15]
    d = w_in.shape[0]
    pad = jnp.zeros((d, LANES - 16), w_in.dtype)
    return jnp.concatenate([
        w_in[:, fq:ff], w_in[:, gu:nq], w_in[:, nq:nkc], w_in[:, nks:ng], w_in[:, pz:end],
        w_in[:, nkc:nks], w_in[:, ff:gu], w_in[:, ng:pz], pad], axis=1)


def _inproj(x, g, w_in):
    n, d = x.shape
    tm = min(FFN_TOKENS, n)
    w = _relayout_w_in(w_in).astype(BF16)
    widths = (768, 512, 256, 256, 256, HEAD_DIM, HEAD_DIM, LANES)
    return pl.pallas_call(
        _inproj_kernel,
        grid=(n // tm,),
        in_specs=[
            pl.BlockSpec((tm, d), lambda i: (i, 0)),
            pl.BlockSpec((1, d), lambda i: (0, 0)),
            pl.BlockSpec(w.shape, lambda i: (0, 0)),
        ],
        out_specs=[pl.BlockSpec((tm, wd), lambda i: (i, 0)) for wd in widths],
        out_shape=[jax.ShapeDtypeStruct((n, wd), F32) for wd in widths],
        compiler_params=_params("parallel"),
        name="inproj",
    )(x, g.reshape(1, d), w)


def _bf16_pieces(x):
    p1 = x.astype(BF16).astype(F32)
    r1 = x - p1
    p2 = r1.astype(BF16).astype(F32)
    p3 = (r1 - p2).astype(BF16).astype(F32)
    return p1, p2, p3


_N_PIECES = 3
_N_EXTRA = _N_PIECES * N_HEADS
_PACK_ONE = _N_EXTRA


def _fox_placement():
    pq = np.zeros((LANES, LANES), np.float32)
    pk = np.zeros((LANES, LANES), np.float32)
    for h in range(N_HEADS):
        for p in range(_N_PIECES):
            pq[p * N_HEADS + h, HEAD_DIM + _N_PIECES * h + p] = 1.0
            pk[p * N_HEADS + h, HEAD_DIM + _N_EXTRA + _N_PIECES * h + p] = -1.0
    pq[_PACK_ONE, HEAD_DIM + _N_EXTRA:HEAD_DIM + 2 * _N_EXTRA] = 1.0
    pk[_PACK_ONE, HEAD_DIM:HEAD_DIM + _N_EXTRA] = 1.0
    return jnp.asarray(pq, BF16), jnp.asarray(pk, BF16)


def _fox_kernel(qkv_ref, misc_ref, fb_ref, gq_ref, gk_ref, gmean_ref, pq_ref, pk_ref, o_ref, q_s, k_s, v_s, o_s):
    t = qkv_ref.shape[0]
    pr = min(FOX_PREP_ROWS, t)
    gmean = gmean_ref[...]
    tri = (lax.broadcasted_iota(jnp.int32, (pr, pr), 1) <= lax.broadcasted_iota(jnp.int32, (pr, pr), 0)).astype(BF16)
    lane = lax.broadcasted_iota(jnp.int32, (pr, LANES), 1)
    extra = lane - HEAD_DIM
    own = [((extra >= _N_PIECES * h) & (extra < _N_PIECES * (h + 1)))
           | ((extra >= _N_EXTRA + _N_PIECES * h) & (extra < _N_EXTRA + _N_PIECES * (h + 1))) for h in range(N_HEADS)]
    ones_v = jnp.where(extra == 0, 1.0, 0.0)
    carry = jnp.zeros((1, LANES), F32)
    for r in range(t // pr):
        sl = slice(r * pr, (r + 1) * pr)
        log_f = _log_sigmoid(misc_ref[sl, :] + fb_ref[...]) * LOG2E
        c = sum(_dot(tri, piece.astype(BF16)) for piece in _bf16_pieces(log_f)) + carry
        carry = c[pr - 1:pr, :]
        p1, p2, p3 = _bf16_pieces(c)
        packed = jnp.where(lane < N_HEADS, p1, jnp.where(lane < 2 * N_HEADS, pltpu.roll(p2, N_HEADS, axis=1), jnp.where(
            lane < _N_EXTRA, pltpu.roll(p3, 2 * N_HEADS, axis=1), jnp.where(lane == _PACK_ONE, 1.0, 0.0)))).astype(BF16)
        q_extra = _dot(packed, pq_ref[...])
        k_extra = _dot(packed, pk_ref[...])
        qn = _group_rms(qkv_ref[sl, 0:W_MIX], gmean) * gq_ref[...] * (ATTN_SCALE * LOG2E)
        kn = _group_rms(qkv_ref[sl, W_MIX:2 * W_MIX], gmean) * gk_ref[...]
        v = qkv_ref[sl, 2 * W_MIX:3 * W_MIX]
        for h in range(N_HEADS):
            q_s[h, sl, :] = jnp.where(extra < 0, _head_column(qn, h), q_extra).astype(BF16)
            k_s[h, sl, :] = jnp.where(extra < 0, _head_column(kn, h), jnp.where(own[h], k_extra, 0.0)).astype(BF16)
            v_s[h, sl, :] = jnp.where(extra < 0, _head_column(v, h), ones_v).astype(BF16)

    tq = min(FOX_TILE, t)
    causal = lax.broadcasted_iota(jnp.int32, (tq, tq), 1) <= lax.broadcasted_iota(jnp.int32, (tq, tq), 0)

    def one_head(h):
        for qi in range(t // tq):
            dsl = slice(qi * tq, (qi + 1) * tq)
            q = q_s[h, dsl, :]
            s_d = jnp.where(causal, _dot_nt(q, k_s[h, dsl, :]), NEG_INF)
            m = jnp.max(s_d, axis=-1, keepdims=True)
            if qi:
                s_o = _dot_nt(q, k_s[h, 0:qi * tq, :])
                m = jnp.maximum(m, jnp.max(s_o, axis=-1, keepdims=True))
            acc = _dot(jnp.exp2(s_d - m).astype(BF16), v_s[h, dsl, :])
            if qi:
                acc = acc + _dot(jnp.exp2(s_o - m).astype(BF16), v_s[h, 0:qi * tq, :])
            o_s[h, dsl, :] = acc / acc[:, ONE_LANE:ONE_LANE + 1]

    def head_pair(hp, _):
        one_head(2 * hp)
        one_head(2 * hp + 1)
        return 0

    lax.fori_loop(0, N_HEADS // 2, head_pair, 0)
    for hp in range(N_HEADS // 2):
        o_ref[:, hp * LANES:(hp + 1) * LANES] = _pair_heads(o_s[2 * hp], o_s[2 * hp + 1]).astype(o_ref.dtype)


def _group_mean_matrix(width):
    g = np.kron(np.eye(width // HEAD_DIM, dtype=np.float32), np.full((HEAD_DIM, HEAD_DIM), 1.0 / HEAD_DIM, np.float32))
    return jnp.asarray(g, BF16)


def _fox(fqkv, misc, f_bias, g_q, g_k, b, t):
    assert t % min(FOX_TILE, t) == 0 and t % min(FOX_PREP_ROWS, t) == 0
    fb = jnp.zeros((1, LANES), F32).at[0, _MISC_FORGET:_MISC_FORGET + N_HEADS].set(f_bias)
    gq = jnp.tile(g_q, N_HEADS).reshape(1, W_MIX)
    gk = jnp.tile(g_k, N_HEADS).reshape(1, W_MIX)
    const = lambda shape: pl.BlockSpec(shape, lambda i: (0,) * len(shape))
    head_scratch = lambda dtype: pltpu.VMEM((N_HEADS, t, LANES), dtype)
    return pl.pallas_call(
        _fox_kernel,
        grid=(b,),
        in_specs=[pl.BlockSpec((t, 3 * W_MIX), lambda i: (i, 0)), pl.BlockSpec((t, LANES), lambda i: (i, 0)),
                  const((1, LANES)), const((1, W_MIX)), const((1, W_MIX)), const((W_MIX, W_MIX)),
                  const((LANES, LANES)), const((LANES, LANES))],
        out_specs=pl.BlockSpec((t, W_MIX), lambda i: (i, 0)),
        out_shape=jax.ShapeDtypeStruct((b * t, W_MIX), BF16),
        scratch_shapes=[head_scratch(BF16), head_scratch(BF16), head_scratch(BF16), head_scratch(F32)],
        compiler_params=_params("parallel"),
        name="fox",
    )(fqkv, misc, fb, gq, gk, _group_mean_matrix(W_MIX), *_fox_placement())


def _gmlp_kernel(uv_ref, gv_ref, gmean_ref, w_ref, bias_ref, o_ref):
    c = GMLP_CHUNK
    rows = lax.broadcasted_iota(jnp.int32, (c, c), 0)
    cols = lax.broadcasted_iota(jnp.int32, (c, c), 1)
    lane_group = lax.broadcasted_iota(jnp.int32, (c, W_MIX), 1) // HEAD_DIM
    w_tril = [jnp.where(cols <= rows, w_ref[g], 0.0).astype(BF16) for g in range(N_HEADS)]
    for r in range(uv_ref.shape[0] // c):
        sl = slice(r * c, (r + 1) * c)
        u = _gelu(uv_ref[sl, 0:W_MIX])
        v = _group_rms(_gelu(uv_ref[sl, W_MIX:2 * W_MIX]), gmean_ref[...]) * gv_ref[...]
        s = bias_ref[...]
        for g in range(N_HEADS):
            s = s + _dot(w_tril[g], jnp.where(lane_group == g, v, 0.0).astype(BF16))
        o_ref[sl, :] = (u * s).astype(o_ref.dtype)


class _Part(NamedTuple):
    body: Callable
    args: list
    in_specs: list
    out_specs: list
    out_shape: list


def _const_spec(shape):
    return pl.BlockSpec(shape, lambda i: (0,) * len(shape))


def _run_parts(parts, b, name):
    n_in = [len(p.in_specs) for p in parts]
    n_out = [len(p.out_specs) for p in parts]

    def body(*refs):
        ins, outs = refs[:sum(n_in)], refs[sum(n_in):]
        for p, i0, o0 in zip(parts, np.cumsum([0] + n_in), np.cumsum([0] + n_out)):
            p.body(*ins[i0:i0 + len(p.in_specs)], *outs[o0:o0 + len(p.out_specs)])

    outs = pl.pallas_call(
        body,
        grid=(b,),
        in_specs=[s for p in parts for s in p.in_specs],
        out_specs=[s for p in parts for s in p.out_specs],
        out_shape=[s for p in parts for s in p.out_shape],
        compiler_params=_params("parallel"),
        name=name,
    )(*[a for p in parts for a in p.args])
    return [outs[o0:o0 + k] for o0, k in zip(np.cumsum([0] + n_out), n_out)]


def _gmlp(guv, g_v, w_s, b_s, b, t):
    bias = jnp.repeat(b_s.T, HEAD_DIM, axis=1)
    return _Part(
        _gmlp_kernel,
        [guv, g_v.reshape(1, W_MIX), _group_mean_matrix(W_MIX), w_s, bias],
        [pl.BlockSpec((t, 2 * W_MIX), lambda i: (i, 0)), _const_spec((1, W_MIX)), _const_spec((W_MIX, W_MIX)),
         _const_spec(w_s.shape), _const_spec(bias.shape)],
        [pl.BlockSpec((t, W_MIX), lambda i: (i, 0))],
        [jax.ShapeDtypeStruct((b * t, W_MIX), BF16)])


def _pool_kernel(z_ref, w_ref, scale_ref, o_ref):
    t = z_ref.shape[0]
    z = z_ref[...]
    row = lax.broadcasted_iota(jnp.int32, (t, W_MIX), 0)
    lane_group = lax.broadcasted_iota(jnp.int32, (t, W_MIX), 1) // HEAD_DIM
    sums = []
    s = z
    k = 1
    while k < max(POOL_SIZES):
        s = s + jnp.where(row >= k, pltpu.roll(s, k, axis=0), 0.0)
        k *= 2
        sums.append((k, s))
    win_sum = jnp.zeros_like(z)
    win = jnp.zeros((t, W_MIX), jnp.int32)
    for g, size in enumerate(POOL_SIZES):
        win_sum = jnp.where(lane_group == g, dict(sums)[size], win_sum)
        win = jnp.where(lane_group == g, size, win)
    cnt = jnp.minimum(row + 1, win).astype(F32)
    pooled = win_sum / cnt - z
    o_ref[...] = (_dot(pooled.astype(BF16), w_ref[...]) * scale_ref[...]).astype(o_ref.dtype)


def _pool(pz, w_p, scale, b, t):
    w_bd = jax.scipy.linalg.block_diag(*[w_p[g] for g in range(N_HEADS)]).astype(BF16)
    return _Part(
        _pool_kernel,
        [pz, w_bd, scale.reshape(1, W_MIX)],
        [pl.BlockSpec((t, W_MIX), lambda i: (i, 0)), _const_spec((W_MIX, W_MIX)), _const_spec((1, W_MIX))],
        [pl.BlockSpec((t, W_MIX), lambda i: (i, 0))],
        [jax.ShapeDtypeStruct((b * t, W_MIX), BF16)])


def _nsa_prep_kernel(kv_ref, kc_ref, vc_ref, posk_ref, posv_ref, kw1_ref, vw1_ref, kw2_ref, vw2_ref,
                     gkv_ref, nmask_ref, gmean_ref, c_ref, su_ref, sd_ref, gc_ref, cc_ref, csu_ref, csd_ref,
                     ks_ref, vs_ref, kw_ref, vw_ref, kcmp_ref, vcmp_ref):
    t = kv_ref.shape[0]
    x = kv_ref[...]
    normed = _group_rms(x, gmean_ref[...]) * gkv_ref[...]
    y = _rope(jnp.where(nmask_ref[...] > 0.5, normed, x), c_ref[...], su_ref[...], sd_ref[...])
    lane = lax.broadcasted_iota(jnp.int32, (t, LANES), 1)
    block = lax.broadcasted_iota(jnp.int32, (t, LANES), 0) // SEL_LEN
    key = lane < HEAD_DIM
    ks_v, kw_v = y[:, 0:LANES], y[:, LANES:2 * LANES]
    ks_ref[...] = jnp.where(key, ks_v, jnp.where(lane - HEAD_DIM == block, 1.0, 0.0)).astype(BF16)
    vs_ref[...] = jnp.where(key, pltpu.roll(ks_v, HEAD_DIM, axis=1), _ones_lane((t, LANES))).astype(BF16)
    kw_ref[...] = jnp.where(key, kw_v, 0.0).astype(BF16)
    vw_ref[...] = jnp.where(key, pltpu.roll(kw_v, HEAD_DIM, axis=1), _ones_lane((t, LANES))).astype(BF16)

    half = kc_ref.shape[1]
    nb = kc_ref.shape[0]

    def hidden(x_ref, pos_ref, w1_ref):
        top = _dot((x_ref[...] + pos_ref[:, 0:half]).astype(BF16), w1_ref[0:half, :])
        bot = _dot((x_ref[...] + pos_ref[:, half:2 * half]).astype(BF16), w1_ref[half:2 * half, :])
        return _gelu(top + pltpu.roll(bot, nb - 1, axis=0)).astype(BF16)

    kv_cmp = _dot(hidden(kc_ref, posk_ref, kw1_ref), kw2_ref[...]) + _dot(hidden(vc_ref, posv_ref, vw1_ref), vw2_ref[...])
    key = lax.broadcasted_iota(jnp.int32, kv_cmp.shape, 1) < HEAD_DIM
    normed = _group_rms(kv_cmp, gmean_ref[0:LANES, 0:LANES]) * gc_ref[...]
    y = _rope(jnp.where(key, normed, kv_cmp), cc_ref[...], csu_ref[...], csd_ref[...])
    kcmp_ref[...] = jnp.where(key, y, 0.0).astype(BF16)
    vcmp_ref[...] = jnp.where(key, pltpu.roll(y, HEAD_DIM, axis=1), _ones_lane(y.shape)).astype(BF16)


def _rope_tables(pos):
    inv = ROPE_THETA ** (-jnp.arange(ROPE_HALF, dtype=F32) * 2.0 / ROPE_DIM)
    ang = pos.astype(F32)[:, None] * inv[None, :]
    cos, sin = jnp.cos(ang), jnp.sin(ang)
    n = pos.shape[0]
    zero = jnp.zeros((n, ROPE_HALF), F32)
    rest0 = jnp.zeros((n, HEAD_DIM - ROPE_DIM), F32)
    c = jnp.concatenate([cos, cos, rest0 + 1.0], axis=1)
    s_up = jnp.concatenate([-sin, zero, rest0], axis=1)
    s_dn = jnp.concatenate([zero, sin, rest0], axis=1)
    return c, s_up, s_dn


def _identity_tables(n):
    return jnp.ones((n, HEAD_DIM), F32), jnp.zeros((n, HEAD_DIM), F32), jnp.zeros((n, HEAD_DIM), F32)


def _nsa_prep(nkv, kc, vc, g_kc, g_ks, g_kw, pos_k, k_w1, k_w2, pos_v, v_w1, v_w2, b, t):
    nb = t // CMP_STRIDE
    assert HEAD_DIM + t // SEL_LEN <= LANES
    flat = CMP_STRIDE * HEAD_DIM
    kc2 = kc.reshape(b * nb, flat)
    vc2 = vc.reshape(b * nb, flat)
    rope_t = _rope_tables(jnp.arange(t))
    iden_t = _identity_tables(t)
    tabs = [jnp.concatenate([r, i, r, i], axis=1) for r, i in zip(rope_t, iden_t)]
    rope_c = _rope_tables(jnp.arange(nb) * CMP_STRIDE + CMP_LEN - 1)
    iden_c = _identity_tables(nb)
    tabs_c = [jnp.concatenate([r, i], axis=1) for r, i in zip(rope_c, iden_c)]
    one = jnp.ones((HEAD_DIM,), F32)
    gkv = jnp.concatenate([g_ks, one, g_kw, one]).reshape(1, W_MIX)
    nmask = jnp.concatenate([one, 0 * one, one, 0 * one]).reshape(1, W_MIX)
    gc = jnp.concatenate([g_kc, one]).reshape(1, LANES)
    zpad = jnp.zeros_like(k_w2)
    kw2 = jnp.concatenate([k_w2, zpad], axis=1).astype(BF16)
    vw2 = jnp.concatenate([zpad, v_w2], axis=1).astype(BF16)
    const = _const_spec
    seq = lambda width: pl.BlockSpec((t, width), lambda i: (i, 0))
    cmp_in = pl.BlockSpec((nb, flat), lambda i: (i, 0))
    cmp_out = pl.BlockSpec((nb, LANES), lambda i: (i, 0))
    return _Part(
        _nsa_prep_kernel,
        [nkv, kc2, vc2, pos_k.reshape(1, 2 * flat), pos_v.reshape(1, 2 * flat),
         k_w1.astype(BF16), v_w1.astype(BF16), kw2, vw2, gkv, nmask, _group_mean_matrix(W_MIX), *tabs, gc, *tabs_c],
        [seq(W_MIX), cmp_in, cmp_in, const((1, 2 * flat)), const((1, 2 * flat)),
         const(k_w1.shape), const(v_w1.shape), const(kw2.shape), const(vw2.shape),
         const((1, W_MIX)), const((1, W_MIX)), const((W_MIX, W_MIX)),
         const((t, W_MIX)), const((t, W_MIX)), const((t, W_MIX)),
         const((1, LANES)), const((nb, LANES)), const((nb, LANES)), const((nb, LANES))],
        [seq(LANES)] * 4 + [cmp_out] * 2,
        [jax.ShapeDtypeStruct((b * t, LANES), BF16)] * 4 + [jax.ShapeDtypeStruct((b * nb, LANES), BF16)] * 2)


def _nsa_kernel(q_ref, misc_ref, gb_ref, gq_ref, gmean_ref, c_ref, su_ref, sd_ref, ovt_ref,
                kcmp_ref, vcmp_ref, ks_ref, vs_ref, kw_ref, vw_ref, o_ref, *, n_top):
    tq = q_ref.shape[0]
    qi = pl.program_id(1)
    t0 = qi * tq
    nb = kcmp_ref.shape[0]
    nsel = ks_ref.shape[0] // SEL_LEN
    rows = N_HEADS * tq

    qn = _rope(_group_rms(q_ref[...], gmean_ref[...]) * gq_ref[...], c_ref[...], su_ref[...], sd_ref[...]) * (ATTN_SCALE * LOG2E)
    q_cols = [_head_column(qn, h) for h in range(N_HEADS)]
    lane_q = lax.broadcasted_iota(jnp.int32, (tq, LANES), 1)

    q0 = jnp.concatenate([jnp.where(lane_q < HEAD_DIM, col, 0.0) for col in q_cols], axis=0).astype(BF16)
    cmp_end = lax.broadcasted_iota(jnp.int32, (rows, nb), 1) * CMP_STRIDE + (CMP_LEN - 1)
    qpos_c = (lax.broadcasted_iota(jnp.int32, (rows, nb), 0) & (tq - 1)) + t0
    m_cmp = cmp_end <= qpos_c
    s = jnp.where(m_cmp, _dot_nt(q0, kcmp_ref[...]), NEG_INF)
    e = jnp.where(m_cmp, jnp.exp2(s - jnp.max(s, axis=-1, keepdims=True)), 0.0)
    denom = jnp.sum(e, axis=-1, keepdims=True)
    p_cmp = e / jnp.where(denom > 0.0, denom, 1.0)
    o_cmp = _dot(p_cmp.astype(BF16), vcmp_ref[...])

    p_sum = p_cmp[0:tq] + p_cmp[tq:2 * tq] + p_cmp[2 * tq:3 * tq] + p_cmp[3 * tq:4 * tq]
    nsp = -(-nsel // 8) * 8
    imp = lax.dot_general(ovt_ref[0:nsp, :], p_sum, (((1,), (1,)), ((), ())), precision=HIGHEST,
                          preferred_element_type=F32)
    blk = lax.broadcasted_iota(jnp.int32, (nsp, tq), 0)
    cur = (lax.broadcasted_iota(jnp.int32, (nsp, tq), 1) + t0) // SEL_LEN
    valid = blk <= cur
    forced = ((blk == 0) | (blk == cur) | (blk == cur - 1)).astype(F32)
    imp = jnp.where(valid, imp + SEL_FORCE * forced, NEG_INF)
    cur_row = (lax.broadcasted_iota(jnp.int32, (1, tq), 1) + t0) // SEL_LEN
    outranked = jnp.zeros((nsp, tq), F32)
    for j in range(nsel):
        rival = imp[j:j + 1, :]
        wins = (rival > imp) | ((rival == imp) & (blk > j))
        outranked = outranked + jnp.where(wins & (cur_row >= j), 1.0, 0.0)
    keep = (outranked < n_top) & valid
    bias_t = jnp.where(keep | (blk >= nsel), 0.0, NEG_INF)
    if nsp < LANES:
        bias_t = jnp.concatenate([bias_t, jnp.zeros((LANES - nsp, tq), F32)], axis=0)
    bias = jnp.concatenate([bias_t[:, c * LANES:(c + 1) * LANES].T for c in range(tq // LANES)], axis=0)
    bias = pltpu.roll(bias, HEAD_DIM, axis=1)
    q = jnp.concatenate([jnp.where(lane_q < HEAD_DIM, col, bias) for col in q_cols], axis=0).astype(BF16)

    trow = lax.broadcasted_iota(jnp.int32, (rows, tq), 0) & (tq - 1)
    tcol = lax.broadcasted_iota(jnp.int32, (rows, tq), 1)
    causal_bias = jnp.where(tcol <= trow, 0.0, NEG_INF)

    def tile(j):
        return pl.ds(pl.multiple_of(j * tq, tq), tq)

    def sel_step(kj, extra_bias, st):
        m, acc = st
        s = _dot_nt(q, ks_ref[tile(kj), :])
        if extra_bias is not None:
            s = s + extra_bias
        m_new = jnp.maximum(m, jnp.max(s, axis=-1, keepdims=True))
        acc = jnp.exp2(m - m_new) * acc + _dot(jnp.exp2(s - m_new).astype(BF16), vs_ref[tile(kj), :])
        return m_new, acc

    st = (jnp.full((rows, 1), NEG_INF, F32), jnp.zeros((rows, LANES), F32))
    st = lax.fori_loop(0, qi, lambda kj, st: sel_step(kj, None, st), st)
    _, acc = sel_step(qi, causal_bias, st)
    o_sel = acc / acc[:, ONE_LANE:ONE_LANE + 1]

    wt = min(NSA_WINDOW_TILE, tq)
    span = WINDOW // wt
    wrow = lax.broadcasted_iota(jnp.int32, (N_HEADS * wt, wt), 0) & (wt - 1)
    wcol = lax.broadcasted_iota(jnp.int32, (N_HEADS * wt, wt), 1)
    o_win = []
    for sub in range(tq // wt):
        q_sub = jnp.concatenate([q[h * tq + sub * wt:h * tq + (sub + 1) * wt] for h in range(N_HEADS)], axis=0)
        diag = qi * (tq // wt) + sub
        parts = []
        for d in range(span, -1, -1):
            ksl = pl.ds(pl.multiple_of(jnp.maximum(diag - d, 0) * wt, wt), wt)
            if d == 0:
                bias_w = jnp.where(wcol <= wrow, 0.0, NEG_INF)
            elif d == span:
                bias_w = jnp.where((wcol > wrow) & (diag >= d), 0.0, NEG_INF)
            else:
                bias_w = jnp.where(diag >= d, 0.0, NEG_INF)
            parts.append((_dot_nt(q_sub, kw_ref[ksl, :]) + bias_w, ksl))
        m = functools.reduce(jnp.maximum, [jnp.max(s, axis=-1, keepdims=True) for s, _ in parts])
        acc = sum(_dot(jnp.exp2(s - m).astype(BF16), vw_ref[ksl, :]) for s, ksl in parts)
        o_win.append(acc / acc[:, ONE_LANE:ONE_LANE + 1])

    gate = jax.nn.sigmoid(misc_ref[...] + gb_ref[...])
    outs = []
    for h in range(N_HEADS):
        hr = slice(h * tq, (h + 1) * tq)
        g0 = _MISC_GATE + 3 * h
        o_win_h = jnp.concatenate([part[h * wt:(h + 1) * wt] for part in o_win], axis=0)
        outs.append(gate[:, g0:g0 + 1] * o_cmp[hr] + gate[:, g0 + 1:g0 + 2] * o_sel[hr] + gate[:, g0 + 2:g0 + 3] * o_win_h)
    for hp in range(N_HEADS // 2):
        o_ref[:, hp * LANES:(hp + 1) * LANES] = _pair_heads(outs[2 * hp], outs[2 * hp + 1]).astype(o_ref.dtype)


def _cmp_to_sel_overlap_t(nb, nsel):
    nc = nb - 1
    cs = np.arange(nc) * CMP_STRIDE
    ss = np.arange(nsel) * SEL_LEN
    ov = np.clip(np.minimum(cs[:, None] + CMP_LEN, ss[None, :] + SEL_LEN) - np.maximum(cs[:, None], ss[None, :]), 0, None)
    out = np.zeros((LANES, nb), np.float32)
    out[:nsel, :nc] = (ov / CMP_LEN).T
    return jnp.asarray(out)


def _nsa(nq, misc, gate_b, g_q, ks, vs, kw, vw, kcmp, vcmp, b, t):
    tq = min(NSA_TILE, t)
    nb = t // CMP_STRIDE
    nsel = t // SEL_LEN
    wt = min(NSA_WINDOW_TILE, tq)
    assert HEAD_DIM + nsel <= LANES and nb % 8 == 0 and tq & (tq - 1) == 0 and tq % LANES == 0
    assert WINDOW % wt == 0 and tq % wt == 0 and wt & (wt - 1) == 0
    gb = jnp.zeros((1, LANES), F32).at[0, _MISC_GATE:_MISC_GATE + 3 * N_HEADS].set(gate_b)
    gq = jnp.tile(g_q, N_HEADS).reshape(1, W_MIX)
    tabs = [jnp.tile(x, (1, N_HEADS)) for x in _rope_tables(jnp.arange(t))]
    nq_tiles = t // tq
    const = lambda shape: pl.BlockSpec(shape, lambda i, j: (0,) * len(shape))
    qtile = lambda width: pl.BlockSpec((tq, width), lambda i, j: (i * nq_tiles + j, 0))
    ptile = pl.BlockSpec((tq, W_MIX), lambda i, j: (j, 0))
    seq = pl.BlockSpec((t, LANES), lambda i, j: (i, 0))
    cmp = pl.BlockSpec((nb, LANES), lambda i, j: (i, 0))
    return pl.pallas_call(
        functools.partial(_nsa_kernel, n_top=min(SEL_TOP, nsel)),
        grid=(b, nq_tiles),
        in_specs=[qtile(W_MIX), qtile(LANES), const((1, LANES)), const((1, W_MIX)), const((W_MIX, W_MIX)),
                  ptile, ptile, ptile, const((LANES, nb)), cmp, cmp, seq, seq, seq, seq],
        out_specs=qtile(W_MIX),
        out_shape=jax.ShapeDtypeStruct((b * t, W_MIX), BF16),
        compiler_params=_params("parallel", "arbitrary"),
        name="nsa",
    )(nq, misc, gb, gq, _group_mean_matrix(W_MIX), *tabs, _cmp_to_sel_overlap_t(nb, nsel),
      kcmp, vcmp, ks, vs, kw, vw)


def kernel(x, ffn1_norm, ffn1_w1, ffn1_w3, ffn1_w2, mix_norm, w_in, w_out, fox_f_bias, fox_q_norm, fox_k_norm, gmlp_v_norm, gmlp_w_s, gmlp_b_s, nsa_q_norm, nsa_kc_norm, nsa_ks_norm, nsa_kw_norm, nsa_cmp_pos_k, nsa_cmp_k_w1, nsa_cmp_k_w2, nsa_cmp_pos_v, nsa_cmp_v_w1, nsa_cmp_v_w2, nsa_gate_bias, pool_w, pool_scale, ffn2_norm, ffn2_w1, ffn2_w3, ffn2_w2):
    b, t, d = x.shape
    assert t % GMLP_CHUNK == 0 and t >= CMP_LEN
    n = b * t
    xf = x.reshape(n, d)
    for l in range(ffn1_norm.shape[0]):
        xf = _ffn(xf, ffn1_norm[l], ffn1_w1[l], ffn1_w3[l], ffn1_w2[l])
        fqkv, guv, nq, nkv, pz, kc, vc, misc = _inproj(xf, mix_norm[l], w_in[l])
        o_a = _fox(fqkv, misc, fox_f_bias[l], fox_q_norm[l], fox_k_norm[l], b, t)
        (o_b,), (o_d,), (ks, vs, kw, vw, kcmp, vcmp) = _run_parts([
            _gmlp(guv, gmlp_v_norm[l], gmlp_w_s[l], gmlp_b_s[l], b, t),
            _pool(pz, pool_w[l], pool_scale[l], b, t),
            _nsa_prep(nkv, kc, vc, nsa_kc_norm[l], nsa_ks_norm[l], nsa_kw_norm[l],
                      nsa_cmp_pos_k[l], nsa_cmp_k_w1[l], nsa_cmp_k_w2[l],
                      nsa_cmp_pos_v[l], nsa_cmp_v_w1[l], nsa_cmp_v_w2[l], b, t)], b, "seq_mixers")
        o_c = _nsa(nq, misc, nsa_gate_bias[l], nsa_q_norm[l], ks, vs, kw, vw, kcmp, vcmp, b, t)
        xf = _ffn(xf, ffn2_norm[l], ffn2_w1[l], ffn2_w3[l], ffn2_w2[l], mixers=(o_a, o_b, o_c, o_d), w_out=w_out[l])
    return xf.reshape(b, t, d)
```

```python
import functools
from typing import Callable, NamedTuple

import numpy as np
import jax
import jax.numpy as jnp
from jax import lax
from jax.experimental import pallas as pl
from jax.experimental.pallas import tpu as pltpu

F32 = jnp.float32
BF16 = jnp.bfloat16
HIGHEST = lax.Precision.HIGHEST

HEAD_DIM = 64
N_HEADS = 4
W_MIX = N_HEADS * HEAD_DIM
ROPE_THETA = 500000.0
ROPE_DIM = HEAD_DIM // 4
ROPE_HALF = ROPE_DIM // 2
GMLP_CHUNK = 128
CMP_LEN = 32
CMP_STRIDE = 16
SEL_LEN = 64
SEL_TOP = 16
WINDOW = 512
POOL_SIZES = (2, 4, 8, 16)
FFN_RES_WEIGHT = 0.5
EPS = 1e-6
NEG_INF = -1e30
SEL_FORCE = 1e3
ATTN_SCALE = HEAD_DIM ** -0.5
LOG2E = 1.4426950408889634

LANES = 128
VMEM_LIMIT = 52 * 1024 * 1024

FFN_TOKENS = 1024
FFN_MIX_TOKENS = 1024
FFN_HIDDEN = 256
FOX_PREP_ROWS = 256
FOX_TILE = 512
NSA_TILE = 512
NSA_WINDOW_TILE = 512
ONE_LANE = HEAD_DIM


def _params(*sem):
    return pltpu.CompilerParams(dimension_semantics=sem, vmem_limit_bytes=VMEM_LIMIT)


def _dot(a, b, **kw):
    return jnp.dot(a, b, preferred_element_type=F32, **kw)


def _dot_nt(a, b):
    return lax.dot_general(a, b, (((1,), (1,)), ((), ())), preferred_element_type=F32)


def _rms(x):
    return x * lax.rsqrt(jnp.mean(x * x, axis=-1, keepdims=True) + EPS)


def _group_rms(x, gmean):
    sq = x * x
    hi = sq.astype(BF16)
    lo = (sq - hi.astype(F32)).astype(BF16)
    ms = _dot(hi, gmean) + _dot(lo, gmean)
    return x * lax.rsqrt(ms + EPS)


def _rope(x, c, s_up, s_dn):
    w = x.shape[-1]
    return x * c + pltpu.roll(x, w - ROPE_HALF, axis=1) * s_up + pltpu.roll(x, ROPE_HALF, axis=1) * s_dn


def _gelu(x):
    return 0.5 * x * (1.0 + jnp.tanh(0.7978845608028654 * (x + 0.044715 * (x * x * x))))


def _log_sigmoid(x):
    return jnp.minimum(x, 0.0) - jnp.log(1.0 + jnp.exp(-jnp.abs(x)))


def _head_column(x, h):
    col = x[:, (h // 2) * LANES:(h // 2 + 1) * LANES]
    return pltpu.roll(col, HEAD_DIM, axis=1) if h % 2 else col


def _pair_heads(even, odd):
    lane = lax.broadcasted_iota(jnp.int32, even.shape, 1)
    return jnp.where(lane < HEAD_DIM, even, pltpu.roll(odd, HEAD_DIM, axis=1))


def _ones_lane(shape):
    return jnp.where(lax.broadcasted_iota(jnp.int32, shape, 1) == ONE_LANE, 1.0, 0.0)


def _ffn_kernel(x_ref, g_ref, w1_ref, w3_ref, w2_ref, *rest):
    *mix_refs, o_ref = rest
    x = x_ref[...]
    if mix_refs:
        *mixers, wout_ref = mix_refs
        for k, ref in enumerate(mixers):
            x = x + _dot(ref[...], wout_ref[k * W_MIX:(k + 1) * W_MIX, :])
    h = (_rms(x) * g_ref[...]).astype(BF16)
    acc = jnp.zeros_like(x)
    for c in range(w1_ref.shape[1] // FFN_HIDDEN):
        cs = slice(c * FFN_HIDDEN, (c + 1) * FFN_HIDDEN)
        a = _dot(h, w1_ref[:, cs])
        b = _dot(h, w3_ref[:, cs])
        acc = acc + _dot((a * jax.nn.sigmoid(a) * b).astype(BF16), w2_ref[cs, :])
    o_ref[...] = x + FFN_RES_WEIGHT * acc


def _ffn(x, g, w1, w3, w2, mixers=(), w_out=None):
    n, d = x.shape
    tm = min(FFN_MIX_TOKENS if mixers else FFN_TOKENS, n)
    resident = lambda shape: pl.BlockSpec(shape, lambda i: (0, 0), pipeline_mode=pl.Buffered(1))
    mix_specs = [pl.BlockSpec((tm, W_MIX), lambda i: (i, 0)) for _ in mixers] + ([resident(w_out.shape)] if mixers else [])
    mix_args = list(mixers) + ([w_out.astype(BF16)] if mixers else [])
    return pl.pallas_call(
        _ffn_kernel,
        grid=(n // tm,),
        in_specs=[pl.BlockSpec((tm, d), lambda i: (i, 0)), resident((1, d)),
                  resident(w1.shape), resident(w3.shape), resident(w2.shape)] + mix_specs,
        out_specs=pl.BlockSpec((tm, d), lambda i: (i, 0)),
        out_shape=jax.ShapeDtypeStruct((n, d), F32),
        compiler_params=_params("parallel"),
        name="ffn_mix" if mixers else "ffn",
    )(x, g.reshape(1, d), w1.astype(BF16), w3.astype(BF16), w2.astype(BF16), *mix_args)


_IN_GROUPS = (768, 512, 256, 256, 256, 128, 128)
_MISC_FORGET = 0
_MISC_GATE = 4


def _inproj_kernel(x_ref, g_ref, w_ref, fqkv_ref, guv_ref, nq_ref, nkv_ref, pz_ref, kc_ref, vc_ref, misc_ref):
    h = (_rms(x_ref[...]) * g_ref[...]).astype(BF16)
    off = 0
    for ref, width in zip((fqkv_ref, guv_ref, nq_ref, nkv_ref, pz_ref), _IN_GROUPS[:5]):
        ref[...] = _dot(h, w_ref[:, off:off + width])
        off += width
    tail = _dot(h, w_ref[:, off:off + 2 * LANES])
    kc_ref[...] = tail[:, :HEAD_DIM]
    vc_ref[...] = tail[:, HEAD_DIM:LANES]
    misc_ref[...] = tail[:, LANES:]


def _relayout_w_in(w_in):
    o = np.cumsum((0, 256, 256, 256, 4, 256, 256, 256, 64, 64, 64, 64, 64, 64, 12, 256))
    fq, ff, gu, nq, nkc, nks, ng, pz, end = o[0], o[3], o[4], o[6], o[7], o[9], o[13], o[14], o[15]
    d = w_in.shape[0]
    pad = jnp.zeros((d, LANES - 16), w_in.dtype)
    return jnp.concatenate([
        w_in[:, fq:ff], w_in[:, gu:nq], w_in[:, nq:nkc], w_in[:, nks:ng], w_in[:, pz:end],
        w_in[:, nkc:nks], w_in[:, ff:gu], w_in[:, ng:pz], pad], axis=1)


def _inproj(x, g, w_in):
    n, d = x.shape
    tm = min(FFN_TOKENS, n)
    w = _relayout_w_in(w_in).astype(BF16)
    widths = (768, 512, 256, 256, 256, HEAD_DIM, HEAD_DIM, LANES)
    return pl.pallas_call(
        _inproj_kernel,
        grid=(n // tm,),
        in_specs=[
            pl.BlockSpec((tm, d), lambda i: (i, 0)),
            pl.BlockSpec((1, d), lambda i: (0, 0)),
            pl.BlockSpec(w.shape, lambda i: (0, 0)),
        ],
        out_specs=[pl.BlockSpec((tm, wd), lambda i: (i, 0)) for wd in widths],
        out_shape=[jax.ShapeDtypeStruct((n, wd), F32) for wd in widths],
        compiler_params=_params("parallel"),
        name="inproj",
    )(x, g.reshape(1, d), w)


def _bf16_pieces(x):
    p1 = x.astype(BF16).astype(F32)
    r1 = x - p1
    p2 = r1.astype(BF16).astype(F32)
    p3 = (r1 - p2).astype(BF16).astype(F32)
    return p1, p2, p3


_N_PIECES = 3
_N_EXTRA = _N_PIECES * N_HEADS
_PACK_ONE = _N_EXTRA


def _fox_placement():
    pq = np.zeros((LANES, LANES), np.float32)
    pk = np.zeros((LANES, LANES), np.float32)
    for h in range(N_HEADS):
        for p in range(_N_PIECES):
            pq[p * N_HEADS + h, HEAD_DIM + _N_PIECES * h + p] = 1.0
            pk[p * N_HEADS + h, HEAD_DIM + _N_EXTRA + _N_PIECES * h + p] = -1.0
    pq[_PACK_ONE, HEAD_DIM + _N_EXTRA:HEAD_DIM + 2 * _N_EXTRA] = 1.0
    pk[_PACK_ONE, HEAD_DIM:HEAD_DIM + _N_EXTRA] = 1.0
    return jnp.asarray(pq, BF16), jnp.asarray(pk, BF16)


def _fox_kernel(qkv_ref, misc_ref, fb_ref, gq_ref, gk_ref, gmean_ref, pq_ref, pk_ref, o_ref, q_s, k_s, v_s, o_s):
    t = qkv_ref.shape[0]
    pr = min(FOX_PREP_ROWS, t)
    gmean = gmean_ref[...]
    tri = (lax.broadcasted_iota(jnp.int32, (pr, pr), 1) <= lax.broadcasted_iota(jnp.int32, (pr, pr), 0)).astype(BF16)
    lane = lax.broadcasted_iota(jnp.int32, (pr, LANES), 1)
    extra = lane - HEAD_DIM
    own = [((extra >= _N_PIECES * h) & (extra < _N_PIECES * (h + 1)))
           | ((extra >= _N_EXTRA + _N_PIECES * h) & (extra < _N_EXTRA + _N_PIECES * (h + 1))) for h in range(N_HEADS)]
    ones_v = jnp.where(extra == 0, 1.0, 0.0)
    carry = jnp.zeros((1, LANES), F32)
    for r in range(t // pr):
        sl = slice(r * pr, (r + 1) * pr)
        log_f = _log_sigmoid(misc_ref[sl, :] + fb_ref[...]) * LOG2E
        c = sum(_dot(tri, piece.astype(BF16)) for piece in _bf16_pieces(log_f)) + carry
        carry = c[pr - 1:pr, :]
        p1, p2, p3 = _bf16_pieces(c)
        packed = jnp.where(lane < N_HEADS, p1, jnp.where(lane < 2 * N_HEADS, pltpu.roll(p2, N_HEADS, axis=1), jnp.where(
            lane < _N_EXTRA, pltpu.roll(p3, 2 * N_HEADS, axis=1), jnp.where(lane == _PACK_ONE, 1.0, 0.0)))).astype(BF16)
        q_extra = _dot(packed, pq_ref[...])
        k_extra = _dot(packed, pk_ref[...])
        qn = _group_rms(qkv_ref[sl, 0:W_MIX], gmean) * gq_ref[...] * (ATTN_SCALE * LOG2E)
        kn = _group_rms(qkv_ref[sl, W_MIX:2 * W_MIX], gmean) * gk_ref[...]
        v = qkv_ref[sl, 2 * W_MIX:3 * W_MIX]
        for h in range(N_HEADS):
            q_s[h, sl, :] = jnp.where(extra < 0, _head_column(qn, h), q_extra).astype(BF16)
            k_s[h, sl, :] = jnp.where(extra < 0, _head_column(kn, h), jnp.where(own[h], k_extra, 0.0)).astype(BF16)
            v_aug = jnp.where(extra < 0, _head_column(v, h), ones_v)
            v_s[h, :, sl] = v_aug.T.astype(BF16)

    tq = min(FOX_TILE, t)
    causal = lax.broadcasted_iota(jnp.int32, (tq, tq), 0) <= lax.broadcasted_iota(jnp.int32, (tq, tq), 1)

    def one_head(h):
        for qi in range(t // tq):
            dsl = slice(qi * tq, (qi + 1) * tq)
            q = q_s[h, dsl, :]
            s_d = jnp.where(causal, _dot_nt(k_s[h, dsl, :], q), NEG_INF)
            m = jnp.max(s_d, axis=0, keepdims=True)
            if qi:
                s_o = _dot_nt(k_s[h, 0:qi * tq, :], q)
                m = jnp.maximum(m, jnp.max(s_o, axis=0, keepdims=True))
            acc = _dot(v_s[h, :, dsl], jnp.exp2(s_d - m).astype(BF16))
            if qi:
                acc = acc + _dot(v_s[h, :, 0:qi * tq], jnp.exp2(s_o - m).astype(BF16))
            o_s[h, dsl, :] = (acc / acc[ONE_LANE:ONE_LANE + 1, :]).T

    def head_pair(hp, _):
        one_head(2 * hp)
        one_head(2 * hp + 1)
        return 0

    lax.fori_loop(0, N_HEADS // 2, head_pair, 0)
    for hp in range(N_HEADS // 2):
        o_ref[:, hp * LANES:(hp + 1) * LANES] = _pair_heads(o_s[2 * hp], o_s[2 * hp + 1]).astype(o_ref.dtype)


def _group_mean_matrix(width):
    g = np.kron(np.eye(width // HEAD_DIM, dtype=np.float32), np.full((HEAD_DIM, HEAD_DIM), 1.0 / HEAD_DIM, np.float32))
    return jnp.asarray(g, BF16)


def _fox(fqkv, misc, f_bias, g_q, g_k, b, t):
    assert t % min(FOX_TILE, t) == 0 and t % min(FOX_PREP_ROWS, t) == 0
    fb = jnp.zeros((1, LANES), F32).at[0, _MISC_FORGET:_MISC_FORGET + N_HEADS].set(f_bias)
    gq = jnp.tile(g_q, N_HEADS).reshape(1, W_MIX)
    gk = jnp.tile(g_k, N_HEADS).reshape(1, W_MIX)
    const = lambda shape: pl.BlockSpec(shape, lambda i: (0,) * len(shape))
    head_scratch = lambda dtype: pltpu.VMEM((N_HEADS, t, LANES), dtype)
    return pl.pallas_call(
        _fox_kernel,
        grid=(b,),
        in_specs=[pl.BlockSpec((t, 3 * W_MIX), lambda i: (i, 0)), pl.BlockSpec((t, LANES), lambda i: (i, 0)),
                  const((1, LANES)), const((1, W_MIX)), const((1, W_MIX)), const((W_MIX, W_MIX)),
                  const((LANES, LANES)), const((LANES, LANES))],
        out_specs=pl.BlockSpec((t, W_MIX), lambda i: (i, 0)),
        out_shape=jax.ShapeDtypeStruct((b * t, W_MIX), BF16),
        scratch_shapes=[head_scratch(BF16), head_scratch(BF16), pltpu.VMEM((N_HEADS, LANES, t), BF16), head_scratch(F32)],
        compiler_params=_params("parallel"),
        name="fox",
    )(fqkv, misc, fb, gq, gk, _group_mean_matrix(W_MIX), *_fox_placement())


def _gmlp_kernel(uv_ref, gv_ref, gmean_ref, w_ref, bias_ref, o_ref):
    c = GMLP_CHUNK
    rows = lax.broadcasted_iota(jnp.int32, (c, c), 0)
    cols = lax.broadcasted_iota(jnp.int32, (c, c), 1)
    lane_group = lax.broadcasted_iota(jnp.int32, (c, W_MIX), 1) // HEAD_DIM
    w_tril = [jnp.where(cols <= rows, w_ref[g], 0.0).astype(BF16) for g in range(N_HEADS)]
    for r in range(uv_ref.shape[0] // c):
        sl = slice(r * c, (r + 1) * c)
        u = _gelu(uv_ref[sl, 0:W_MIX])
        v = _group_rms(_gelu(uv_ref[sl, W_MIX:2 * W_MIX]), gmean_ref[...]) * gv_ref[...]
        s = bias_ref[...]
        for g in range(N_HEADS):
            s = s + _dot(w_tril[g], jnp.where(lane_group == g, v, 0.0).astype(BF16))
        o_ref[sl, :] = (u * s).astype(o_ref.dtype)


class _Part(NamedTuple):
    body: Callable
    args: list
    in_specs: list
    out_specs: list
    out_shape: list


def _const_spec(shape):
    return pl.BlockSpec(shape, lambda i: (0,) * len(shape))


def _run_parts(parts, b, name):
    n_in = [len(p.in_specs) for p in parts]
    n_out = [len(p.out_specs) for p in parts]

    def body(*refs):
        ins, outs = refs[:sum(n_in)], refs[sum(n_in):]
        for p, i0, o0 in zip(parts, np.cumsum([0] + n_in), np.cumsum([0] + n_out)):
            p.body(*ins[i0:i0 + len(p.in_specs)], *outs[o0:o0 + len(p.out_specs)])

    outs = pl.pallas_call(
        body,
        grid=(b,),
        in_specs=[s for p in parts for s in p.in_specs],
        out_specs=[s for p in parts for s in p.out_specs],
        out_shape=[s for p in parts for s in p.out_shape],
        compiler_params=_params("parallel"),
        name=name,
    )(*[a for p in parts for a in p.args])
    return [outs[o0:o0 + k] for o0, k in zip(np.cumsum([0] + n_out), n_out)]


def _gmlp(guv, g_v, w_s, b_s, b, t):
    bias = jnp.repeat(b_s.T, HEAD_DIM, axis=1)
    return _Part(
        _gmlp_kernel,
        [guv, g_v.reshape(1, W_MIX), _group_mean_matrix(W_MIX), w_s, bias],
        [pl.BlockSpec((t, 2 * W_MIX), lambda i: (i, 0)), _const_spec((1, W_MIX)), _const_spec((W_MIX, W_MIX)),
         _const_spec(w_s.shape), _const_spec(bias.shape)],
        [pl.BlockSpec((t, W_MIX), lambda i: (i, 0))],
        [jax.ShapeDtypeStruct((b * t, W_MIX), BF16)])


def _pool_kernel(z_ref, w_ref, scale_ref, o_ref):
    t = z_ref.shape[0]
    z = z_ref[...]
    row = lax.broadcasted_iota(jnp.int32, (t, W_MIX), 0)
    lane_group = lax.broadcasted_iota(jnp.int32, (t, W_MIX), 1) // HEAD_DIM
    sums = []
    s = z
    k = 1
    while k < max(POOL_SIZES):
        s = s + jnp.where(row >= k, pltpu.roll(s, k, axis=0), 0.0)
        k *= 2
        sums.append((k, s))
    win_sum = jnp.zeros_like(z)
    win = jnp.zeros((t, W_MIX), jnp.int32)
    for g, size in enumerate(POOL_SIZES):
        win_sum = jnp.where(lane_group == g, dict(sums)[size], win_sum)
        win = jnp.where(lane_group == g, size, win)
    cnt = jnp.minimum(row + 1, win).astype(F32)
    pooled = win_sum / cnt - z
    o_ref[...] = (_dot(pooled.astype(BF16), w_ref[...]) * scale_ref[...]).astype(o_ref.dtype)


def _pool(pz, w_p, scale, b, t):
    w_bd = jax.scipy.linalg.block_diag(*[w_p[g] for g in range(N_HEADS)]).astype(BF16)
    return _Part(
        _pool_kernel,
        [pz, w_bd, scale.reshape(1, W_MIX)],
        [pl.BlockSpec((t, W_MIX), lambda i: (i, 0)), _const_spec((W_MIX, W_MIX)), _const_spec((1, W_MIX))],
        [pl.BlockSpec((t, W_MIX), lambda i: (i, 0))],
        [jax.ShapeDtypeStruct((b * t, W_MIX), BF16)])


def _nsa_prep_kernel(kv_ref, kc_ref, vc_ref, posk_ref, posv_ref, kw1_ref, vw1_ref, kw2_ref, vw2_ref,
                     gkv_ref, nmask_ref, gmean_ref, c_ref, su_ref, sd_ref, gc_ref, cc_ref, csu_ref, csd_ref,
                     ks_ref, vs_ref, kw_ref, vw_ref, kcmp_ref, vcmp_ref):
    t = kv_ref.shape[0]
    x = kv_ref[...]
    normed = _group_rms(x, gmean_ref[...]) * gkv_ref[...]
    y = _rope(jnp.where(nmask_ref[...] > 0.5, normed, x), c_ref[...], su_ref[...], sd_ref[...])
    lane = lax.broadcasted_iota(jnp.int32, (t, LANES), 1)
    block = lax.broadcasted_iota(jnp.int32, (t, LANES), 0) // SEL_LEN
    key = lane < HEAD_DIM
    ks_v, kw_v = y[:, 0:LANES], y[:, LANES:2 * LANES]
    ks_ref[...] = jnp.where(key, ks_v, jnp.where(lane - HEAD_DIM == block, 1.0, 0.0)).astype(BF16)
    vs_ref[...] = jnp.where(key, pltpu.roll(ks_v, HEAD_DIM, axis=1), _ones_lane((t, LANES))).astype(BF16)
    kw_ref[...] = jnp.where(key, kw_v, 0.0).astype(BF16)
    vw_ref[...] = jnp.where(key, pltpu.roll(kw_v, HEAD_DIM, axis=1), _ones_lane((t, LANES))).astype(BF16)

    half = kc_ref.shape[1]
    nb = kc_ref.shape[0]

    def hidden(x_ref, pos_ref, w1_ref):
        top = _dot((x_ref[...] + pos_ref[:, 0:half]).astype(BF16), w1_ref[0:half, :])
        bot = _dot((x_ref[...] + pos_ref[:, half:2 * half]).astype(BF16), w1_ref[half:2 * half, :])
        return _gelu(top + pltpu.roll(bot, nb - 1, axis=0)).astype(BF16)

    kv_cmp = _dot(hidden(kc_ref, posk_ref, kw1_ref), kw2_ref[...]) + _dot(hidden(vc_ref, posv_ref, vw1_ref), vw2_ref[...])
    key = lax.broadcasted_iota(jnp.int32, kv_cmp.shape, 1) < HEAD_DIM
    normed = _group_rms(kv_cmp, gmean_ref[0:LANES, 0:LANES]) * gc_ref[...]
    y = _rope(jnp.where(key, normed, kv_cmp), cc_ref[...], csu_ref[...], csd_ref[...])
    kcmp_ref[...] = jnp.where(key, y, 0.0).astype(BF16)
    vcmp_ref[...] = jnp.where(key, pltpu.roll(y, HEAD_DIM, axis=1), _ones_lane(y.shape)).astype(BF16)


def _rope_tables(pos):
    inv = ROPE_THETA ** (-jnp.arange(ROPE_HALF, dtype=F32) * 2.0 / ROPE_DIM)
    ang = pos.astype(F32)[:, None] * inv[None, :]
    cos, sin = jnp.cos(ang), jnp.sin(ang)
    n = pos.shape[0]
    zero = jnp.zeros((n, ROPE_HALF), F32)
    rest0 = jnp.zeros((n, HEAD_DIM - ROPE_DIM), F32)
    c = jnp.concatenate([cos, cos, rest0 + 1.0], axis=1)
    s_up = jnp.concatenate([-sin, zero, rest0], axis=1)
    s_dn = jnp.concatenate([zero, sin, rest0], axis=1)
    return c, s_up, s_dn


def _identity_tables(n):
    return jnp.ones((n, HEAD_DIM), F32), jnp.zeros((n, HEAD_DIM), F32), jnp.zeros((n, HEAD_DIM), F32)


def _nsa_prep(nkv, kc, vc, g_kc, g_ks, g_kw, pos_k, k_w1, k_w2, pos_v, v_w1, v_w2, b, t):
    nb = t // CMP_STRIDE
    assert HEAD_DIM + t // SEL_LEN <= LANES
    flat = CMP_STRIDE * HEAD_DIM
    kc2 = kc.reshape(b * nb, flat)
    vc2 = vc.reshape(b * nb, flat)
    rope_t = _rope_tables(jnp.arange(t))
    iden_t = _identity_tables(t)
    tabs = [jnp.concatenate([r, i, r, i], axis=1) for r, i in zip(rope_t, iden_t)]
    rope_c = _rope_tables(jnp.arange(nb) * CMP_STRIDE + CMP_LEN - 1)
    iden_c = _identity_tables(nb)
    tabs_c = [jnp.concatenate([r, i], axis=1) for r, i in zip(rope_c, iden_c)]
    one = jnp.ones((HEAD_DIM,), F32)
    gkv = jnp.concatenate([g_ks, one, g_kw, one]).reshape(1, W_MIX)
    nmask = jnp.concatenate([one, 0 * one, one, 0 * one]).reshape(1, W_MIX)
    gc = jnp.concatenate([g_kc, one]).reshape(1, LANES)
    zpad = jnp.zeros_like(k_w2)
    kw2 = jnp.concatenate([k_w2, zpad], axis=1).astype(BF16)
    vw2 = jnp.concatenate([zpad, v_w2], axis=1).astype(BF16)
    const = _const_spec
    seq = lambda width: pl.BlockSpec((t, width), lambda i: (i, 0))
    cmp_in = pl.BlockSpec((nb, flat), lambda i: (i, 0))
    cmp_out = pl.BlockSpec((nb, LANES), lambda i: (i, 0))
    return _Part(
        _nsa_prep_kernel,
        [nkv, kc2, vc2, pos_k.reshape(1, 2 * flat), pos_v.reshape(1, 2 * flat),
         k_w1.astype(BF16), v_w1.astype(BF16), kw2, vw2, gkv, nmask, _group_mean_matrix(W_MIX), *tabs, gc, *tabs_c],
        [seq(W_MIX), cmp_in, cmp_in, const((1, 2 * flat)), const((1, 2 * flat)),
         const(k_w1.shape), const(v_w1.shape), const(kw2.shape), const(vw2.shape),
         const((1, W_MIX)), const((1, W_MIX)), const((W_MIX, W_MIX)),
         const((t, W_MIX)), const((t, W_MIX)), const((t, W_MIX)),
         const((1, LANES)), const((nb, LANES)), const((nb, LANES)), const((nb, LANES))],
        [seq(LANES)] * 4 + [cmp_out] * 2,
        [jax.ShapeDtypeStruct((b * t, LANES), BF16)] * 4 + [jax.ShapeDtypeStruct((b * nb, LANES), BF16)] * 2)


def _nsa_kernel(q_ref, misc_ref, gb_ref, gq_ref, gmean_ref, c_ref, su_ref, sd_ref, ovt_ref,
                kcmp_ref, vcmp_ref, ks_ref, vs_ref, kw_ref, vw_ref, o_ref, *, n_top):
    tq = q_ref.shape[0]
    qi = pl.program_id(1)
    t0 = qi * tq
    nb = kcmp_ref.shape[0]
    nsel = ks_ref.shape[0] // SEL_LEN
    rows = N_HEADS * tq

    qn = _rope(_group_rms(q_ref[...], gmean_ref[...]) * gq_ref[...], c_ref[...], su_ref[...], sd_ref[...]) * (ATTN_SCALE * LOG2E)
    q_cols = [_head_column(qn, h) for h in range(N_HEADS)]
    lane_q = lax.broadcasted_iota(jnp.int32, (tq, LANES), 1)

    q0 = jnp.concatenate([jnp.where(lane_q < HEAD_DIM, col, 0.0) for col in q_cols], axis=0).astype(BF16)
    cmp_end = lax.broadcasted_iota(jnp.int32, (rows, nb), 1) * CMP_STRIDE + (CMP_LEN - 1)
    qpos_c = (lax.broadcasted_iota(jnp.int32, (rows, nb), 0) & (tq - 1)) + t0
    m_cmp = cmp_end <= qpos_c
    s = jnp.where(m_cmp, _dot_nt(q0, kcmp_ref[...]), NEG_INF)
    e = jnp.where(m_cmp, jnp.exp2(s - jnp.max(s, axis=-1, keepdims=True)), 0.0)
    denom = jnp.sum(e, axis=-1, keepdims=True)
    p_cmp = e / jnp.where(denom > 0.0, denom, 1.0)
    o_cmp = _dot(p_cmp.astype(BF16), vcmp_ref[...])

    p_sum = p_cmp[0:tq] + p_cmp[tq:2 * tq] + p_cmp[2 * tq:3 * tq] + p_cmp[3 * tq:4 * tq]
    nsp = -(-nsel // 8) * 8
    imp = lax.dot_general(ovt_ref[0:nsp, :], p_sum, (((1,), (1,)), ((), ())), precision=HIGHEST,
                          preferred_element_type=F32)
    blk = lax.broadcasted_iota(jnp.int32, (nsp, tq), 0)
    cur = (lax.broadcasted_iota(jnp.int32, (nsp, tq), 1) + t0) // SEL_LEN
    valid = blk <= cur
    forced = ((blk == 0) | (blk == cur) | (blk == cur - 1)).astype(F32)
    imp = jnp.where(valid, imp + SEL_FORCE * forced, NEG_INF)
    cur_row = (lax.broadcasted_iota(jnp.int32, (1, tq), 1) + t0) // SEL_LEN
    outranked = jnp.zeros((nsp, tq), F32)
    for j in range(nsel):
        rival = imp[j:j + 1, :]
        wins = (rival > imp) | ((rival == imp) & (blk > j))
        outranked = outranked + jnp.where(wins & (cur_row >= j), 1.0, 0.0)
    keep = (outranked < n_top) & valid
    bias_t = jnp.where(keep | (blk >= nsel), 0.0, NEG_INF)
    if nsp < LANES:
        bias_t = jnp.concatenate([bias_t, jnp.zeros((LANES - nsp, tq), F32)], axis=0)
    bias = jnp.concatenate([bias_t[:, c * LANES:(c + 1) * LANES].T for c in range(tq // LANES)], axis=0)
    bias = pltpu.roll(bias, HEAD_DIM, axis=1)
    q = jnp.concatenate([jnp.where(lane_q < HEAD_DIM, col, bias) for col in q_cols], axis=0).astype(BF16)

    trow = lax.broadcasted_iota(jnp.int32, (rows, tq), 0) & (tq - 1)
    tcol = lax.broadcasted_iota(jnp.int32, (rows, tq), 1)
    causal_bias = jnp.where(tcol <= trow, 0.0, NEG_INF)

    def tile(j):
        return pl.ds(pl.multiple_of(j * tq, tq), tq)

    def sel_step(kj, extra_bias, st):
        m, acc = st
        s = _dot_nt(q, ks_ref[tile(kj), :])
        if extra_bias is not None:
            s = s + extra_bias
        m_new = jnp.maximum(m, jnp.max(s, axis=-1, keepdims=True))
        acc = jnp.exp2(m - m_new) * acc + _dot(jnp.exp2(s - m_new).astype(BF16), vs_ref[tile(kj), :])
        return m_new, acc

    st = (jnp.full((rows, 1), NEG_INF, F32), jnp.zeros((rows, LANES), F32))
    st = lax.fori_loop(0, qi, lambda kj, st: sel_step(kj, None, st), st)
    _, acc = sel_step(qi, causal_bias, st)
    o_sel = acc / acc[:, ONE_LANE:ONE_LANE + 1]

    wt = min(NSA_WINDOW_TILE, tq)
    span = WINDOW // wt
    wrow = lax.broadcasted_iota(jnp.int32, (N_HEADS * wt, wt), 0) & (wt - 1)
    wcol = lax.broadcasted_iota(jnp.int32, (N_HEADS * wt, wt), 1)
    o_win = []
    for sub in range(tq // wt):
        q_sub = jnp.concatenate([q[h * tq + sub * wt:h * tq + (sub + 1) * wt] for h in range(N_HEADS)], axis=0)
        diag = qi * (tq // wt) + sub
        parts = []
        for d in range(span, -1, -1):
            ksl = pl.ds(pl.multiple_of(jnp.maximum(diag - d, 0) * wt, wt), wt)
            if d == 0:
                bias_w = jnp.where(wcol <= wrow, 0.0, NEG_INF)
            elif d == span:
                bias_w = jnp.where((wcol > wrow) & (diag >= d), 0.0, NEG_INF)
            else:
                bias_w = jnp.where(diag >= d, 0.0, NEG_INF)
            parts.append((_dot_nt(q_sub, kw_ref[ksl, :]) + bias_w, ksl))
        m = functools.reduce(jnp.maximum, [jnp.max(s, axis=-1, keepdims=True) for s, _ in parts])
        acc = sum(_dot(jnp.exp2(s - m).astype(BF16), vw_ref[ksl, :]) for s, ksl in parts)
        o_win.append(acc / acc[:, ONE_LANE:ONE_LANE + 1])

    gate = jax.nn.sigmoid(misc_ref[...] + gb_ref[...])
    outs = []
    for h in range(N_HEADS):
        hr = slice(h * tq, (h + 1) * tq)
        g0 = _MISC_GATE + 3 * h
        o_win_h = jnp.concatenate([part[h * wt:(h + 1) * wt] for part in o_win], axis=0)
        outs.append(gate[:, g0:g0 + 1] * o_cmp[hr] + gate[:, g0 + 1:g0 + 2] * o_sel[hr] + gate[:, g0 + 2:g0 + 3] * o_win_h)
    for hp in range(N_HEADS // 2):
        o_ref[:, hp * LANES:(hp + 1) * LANES] = _pair_heads(outs[2 * hp], outs[2 * hp + 1]).astype(o_ref.dtype)


def _cmp_to_sel_overlap_t(nb, nsel):
    nc = nb - 1
    cs = np.arange(nc) * CMP_STRIDE
    ss = np.arange(nsel) * SEL_LEN
    ov = np.clip(np.minimum(cs[:, None] + CMP_LEN, ss[None, :] + SEL_LEN) - np.maximum(cs[:, None], ss[None, :]), 0, None)
    out = np.zeros((LANES, nb), np.float32)
    out[:nsel, :nc] = (ov / CMP_LEN).T
    return jnp.asarray(out)


def _nsa(nq, misc, gate_b, g_q, ks, vs, kw, vw, kcmp, vcmp, b, t):
    tq = min(NSA_TILE, t)
    nb = t // CMP_STRIDE
    nsel = t // SEL_LEN
    wt = min(NSA_WINDOW_TILE, tq)
    assert HEAD_DIM + nsel <= LANES and nb % 8 == 0 and tq & (tq - 1) == 0 and tq % LANES == 0
    assert WINDOW % wt == 0 and tq % wt == 0 and wt & (wt - 1) == 0
    gb = jnp.zeros((1, LANES), F32).at[0, _MISC_GATE:_MISC_GATE + 3 * N_HEADS].set(gate_b)
    gq = jnp.tile(g_q, N_HEADS).reshape(1, W_MIX)
    tabs = [jnp.tile(x, (1, N_HEADS)) for x in _rope_tables(jnp.arange(t))]
    nq_tiles = t // tq
    const = lambda shape: pl.BlockSpec(shape, lambda i, j: (0,) * len(shape))
    qtile = lambda width: pl.BlockSpec((tq, width), lambda i, j: (i * nq_tiles + j, 0))
    ptile = pl.BlockSpec((tq, W_MIX), lambda i, j: (j, 0))
    seq = pl.BlockSpec((t, LANES), lambda i, j: (i, 0))
    cmp = pl.BlockSpec((nb, LANES), lambda i, j: (i, 0))
    return pl.pallas_call(
        functools.partial(_nsa_kernel, n_top=min(SEL_TOP, nsel)),
        grid=(b, nq_tiles),
        in_specs=[qtile(W_MIX), qtile(LANES), const((1, LANES)), const((1, W_MIX)), const((W_MIX, W_MIX)),
                  ptile, ptile, ptile, const((LANES, nb)), cmp, cmp, seq, seq, seq, seq],
        out_specs=qtile(W_MIX),
        out_shape=jax.ShapeDtypeStruct((b * t, W_MIX), BF16),
        compiler_params=_params("parallel", "arbitrary"),
        name="nsa",
    )(nq, misc, gb, gq, _group_mean_matrix(W_MIX), *tabs, _cmp_to_sel_overlap_t(nb, nsel),
      kcmp, vcmp, ks, vs, kw, vw)


def kernel(x, ffn1_norm, ffn1_w1, ffn1_w3, ffn1_w2, mix_norm, w_in, w_out, fox_f_bias, fox_q_norm, fox_k_norm, gmlp_v_norm, gmlp_w_s, gmlp_b_s, nsa_q_norm, nsa_kc_norm, nsa_ks_norm, nsa_kw_norm, nsa_cmp_pos_k, nsa_cmp_k_w1, nsa_cmp_k_w2, nsa_cmp_pos_v, nsa_cmp_v_w1, nsa_cmp_v_w2, nsa_gate_bias, pool_w, pool_scale, ffn2_norm, ffn2_w1, ffn2_w3, ffn2_w2):
    b, t, d = x.shape
    assert t % GMLP_CHUNK == 0 and t >= CMP_LEN
    n = b * t
    xf = x.reshape(n, d)
    for l in range(ffn1_norm.shape[0]):
        xf = _ffn(xf, ffn1_norm[l], ffn1_w1[l], ffn1_w3[l], ffn1_w2[l])
        fqkv, guv, nq, nkv, pz, kc, vc, misc = _inproj(xf, mix_norm[l], w_in[l])
        o_a = _fox(fqkv, misc, fox_f_bias[l], fox_q_norm[l], fox_k_norm[l], b, t)
        (o_b,), (o_d,), (ks, vs, kw, vw, kcmp, vcmp) = _run_parts([
            _gmlp(guv, gmlp_v_norm[l], gmlp_w_s[l], gmlp_b_s[l], b, t),
            _pool(pz, pool_w[l], pool_scale[l], b, t),
            _nsa_prep(nkv, kc, vc, nsa_kc_norm[l], nsa_ks_norm[l], nsa_kw_norm[l],
                      nsa_cmp_pos_k[l], nsa_cmp_k_w1[l], nsa_cmp_k_w2[l],
                      nsa_cmp_pos_v[l], nsa_cmp_v_w1[l], nsa_cmp_v_w2[l], b, t)], b, "seq_mixers")
        o_c = _nsa(nq, misc, nsa_gate_bias[l], nsa_q_norm[l], ks, vs, kw, vw, kcmp, vcmp, b, t)
        xf = _ffn(xf, ffn2_norm[l], ffn2_w1[l], ffn2_w3[l], ffn2_w2[l], mixers=(o_a, o_b, o_c, o_d), w_out=w_out[l])
    return xf.reshape(b, t, d)
```

```python
import functools
from typing import Callable, NamedTuple

import numpy as np
import jax
import jax.numpy as jnp
from jax import lax
from jax.experimental import pallas as pl
from jax.experimental.pallas import tpu as pltpu

F32 = jnp.float32
BF16 = jnp.bfloat16
HIGHEST = lax.Precision.HIGHEST

HEAD_DIM = 64
N_HEADS = 4
W_MIX = N_HEADS * HEAD_DIM
ROPE_THETA = 500000.0
ROPE_DIM = HEAD_DIM // 4
ROPE_HALF = ROPE_DIM // 2
GMLP_CHUNK = 128
CMP_LEN = 32
CMP_STRIDE = 16
SEL_LEN = 64
SEL_TOP = 16
WINDOW = 512
POOL_SIZES = (2, 4, 8, 16)
FFN_RES_WEIGHT = 0.5
EPS = 1e-6
NEG_INF = -1e30
SEL_FORCE = 1e3
ATTN_SCALE = HEAD_DIM ** -0.5
LOG2E = 1.4426950408889634

LANES = 128
VMEM_LIMIT = 52 * 1024 * 1024

FFN_TOKENS = 1024
FFN_MIX_TOKENS = 1024
FFN_HIDDEN = 256
FOX_PREP_ROWS = 256
FOX_TILE = 1024
NSA_TILE = 512
NSA_WINDOW_TILE = 512
ONE_LANE = HEAD_DIM


def _params(*sem):
    return pltpu.CompilerParams(dimension_semantics=sem, vmem_limit_bytes=VMEM_LIMIT)


def _dot(a, b, **kw):
    return jnp.dot(a, b, preferred_element_type=F32, **kw)


def _dot_nt(a, b):
    return lax.dot_general(a, b, (((1,), (1,)), ((), ())), preferred_element_type=F32)


def _rms(x):
    return x * lax.rsqrt(jnp.mean(x * x, axis=-1, keepdims=True) + EPS)


def _group_rms(x, gmean):
    sq = x * x
    hi = sq.astype(BF16)
    lo = (sq - hi.astype(F32)).astype(BF16)
    ms = _dot(hi, gmean) + _dot(lo, gmean)
    return x * lax.rsqrt(ms + EPS)


def _rope(x, c, s_up, s_dn):
    w = x.shape[-1]
    return x * c + pltpu.roll(x, w - ROPE_HALF, axis=1) * s_up + pltpu.roll(x, ROPE_HALF, axis=1) * s_dn


_GELU_C = 0.7978845608028654


def _gelu(x):
    half = 0.5 * x
    return half + half * jnp.tanh(x * (_GELU_C + (_GELU_C * 0.044715) * (x * x)))


def _log_sigmoid(x):
    return jnp.minimum(x, 0.0) - jnp.log(1.0 + jnp.exp(-jnp.abs(x)))


def _head_column(x, h):
    col = x[:, (h // 2) * LANES:(h // 2 + 1) * LANES]
    return pltpu.roll(col, HEAD_DIM, axis=1) if h % 2 else col


def _pair_heads(even, odd):
    lane = lax.broadcasted_iota(jnp.int32, even.shape, 1)
    return jnp.where(lane < HEAD_DIM, even, pltpu.roll(odd, HEAD_DIM, axis=1))


def _ones_lane(shape):
    return jnp.where(lax.broadcasted_iota(jnp.int32, shape, 1) == ONE_LANE, 1.0, 0.0)


def _ffn_kernel(x_ref, g_ref, w1_ref, w3_ref, w2_ref, *rest):
    *mix_refs, o_ref = rest
    x = x_ref[...]
    if mix_refs:
        *mixers, wout_ref = mix_refs
        for k, ref in enumerate(mixers):
            x = x + _dot(ref[...], wout_ref[k * W_MIX:(k + 1) * W_MIX, :])
    h = (_rms(x) * g_ref[...]).astype(BF16)
    acc = jnp.zeros_like(x)
    for c in range(w1_ref.shape[1] // FFN_HIDDEN):
        cs = slice(c * FFN_HIDDEN, (c + 1) * FFN_HIDDEN)
        a = _dot(h, w1_ref[:, cs])
        b = _dot(h, w3_ref[:, cs])
        acc = acc + _dot((a * jax.nn.sigmoid(a) * b).astype(BF16), w2_ref[cs, :])
    o_ref[...] = x + FFN_RES_WEIGHT * acc


def _ffn(x, g, w1, w3, w2, mixers=(), w_out=None):
    n, d = x.shape
    tm = min(FFN_MIX_TOKENS if mixers else FFN_TOKENS, n)
    resident = lambda shape: pl.BlockSpec(shape, lambda i: (0, 0), pipeline_mode=pl.Buffered(1))
    mix_specs = [pl.BlockSpec((tm, W_MIX), lambda i: (i, 0)) for _ in mixers] + ([resident(w_out.shape)] if mixers else [])
    mix_args = list(mixers) + ([w_out.astype(BF16)] if mixers else [])
    return pl.pallas_call(
        _ffn_kernel,
        grid=(n // tm,),
        in_specs=[pl.BlockSpec((tm, d), lambda i: (i, 0)), resident((1, d)),
                  resident(w1.shape), resident(w3.shape), resident(w2.shape)] + mix_specs,
        out_specs=pl.BlockSpec((tm, d), lambda i: (i, 0)),
        out_shape=jax.ShapeDtypeStruct((n, d), F32),
        compiler_params=_params("parallel"),
        name="ffn_mix" if mixers else "ffn",
    )(x, g.reshape(1, d), w1.astype(BF16), w3.astype(BF16), w2.astype(BF16), *mix_args)


_IN_GROUPS = (768, 512, 256, 256, 256, 128, 128)
_MISC_FORGET = 0
_MISC_GATE = 4


def _inproj_kernel(x_ref, g_ref, w_ref, fqkv_ref, guv_ref, nq_ref, nkv_ref, pz_ref, kc_ref, vc_ref, misc_ref, kcvc_s):
    h = (_rms(x_ref[...]) * g_ref[...]).astype(BF16)
    off = 0
    for ref, width in zip((fqkv_ref, guv_ref, nq_ref, nkv_ref, pz_ref), _IN_GROUPS[:5]):
        ref[...] = _dot(h, w_ref[:, off:off + width])
        off += width
    tail = _dot(h, w_ref[:, off:off + 2 * LANES])
    misc_ref[...] = tail[:, LANES:]
    kcvc_s[...] = tail[:, :LANES]
    groups = x_ref.shape[0] // CMP_STRIDE
    lane = lax.broadcasted_iota(jnp.int32, (groups, LANES), 1)
    for j in range(0, CMP_STRIDE, 2):
        even = kcvc_s[pl.ds(j, groups, stride=CMP_STRIDE), :]
        odd = kcvc_s[pl.ds(j + 1, groups, stride=CMP_STRIDE), :]
        col = slice((j // 2) * LANES, (j // 2 + 1) * LANES)
        kc_ref[:, col] = jnp.where(lane < HEAD_DIM, even, pltpu.roll(odd, HEAD_DIM, axis=1))
        vc_ref[:, col] = jnp.where(lane < HEAD_DIM, pltpu.roll(even, HEAD_DIM, axis=1), odd)


def _relayout_w_in(w_in):
    o = np.cumsum((0, 256, 256, 256, 4, 256, 256, 256, 64, 64, 64, 64, 64, 64, 12, 256))
    fq, ff, gu, nq, nkc, nks, ng, pz, end = o[0], o[3], o[4], o[6], o[7], o[9], o[13], o[14], o[15]
    d = w_in.shape[0]
    pad = jnp.zeros((d, LANES - 16), w_in.dtype)
    return jnp.concatenate([
        w_in[:, fq:ff], w_in[:, gu:nq], w_in[:, nq:nkc], w_in[:, nks:ng], w_in[:, pz:end],
        w_in[:, nkc:nks], w_in[:, ff:gu], w_in[:, ng:pz], pad], axis=1)


def _inproj(x, g, w_in):
    n, d = x.shape
    tm = min(FFN_TOKENS, n)
    assert tm % (8 * CMP_STRIDE) == 0
    w = _relayout_w_in(w_in).astype(BF16)
    flat = CMP_STRIDE * HEAD_DIM
    outs = [(tm, n, wd) for wd in (768, 512, 256, 256, 256)] + [(tm // CMP_STRIDE, n // CMP_STRIDE, flat)] * 2 + [(tm, n, LANES)]
    return pl.pallas_call(
        _inproj_kernel,
        grid=(n // tm,),
        in_specs=[
            pl.BlockSpec((tm, d), lambda i: (i, 0)),
            pl.BlockSpec((1, d), lambda i: (0, 0)),
            pl.BlockSpec(w.shape, lambda i: (0, 0)),
        ],
        out_specs=[pl.BlockSpec((rows, wd), lambda i: (i, 0)) for rows, _, wd in outs],
        out_shape=[jax.ShapeDtypeStruct((total, wd), F32) for _, total, wd in outs],
        scratch_shapes=[pltpu.VMEM((tm, LANES), F32)],
        compiler_params=_params("parallel"),
        name="inproj",
    )(x, g.reshape(1, d), w)


def _bf16_pieces(x):
    p1 = x.astype(BF16).astype(F32)
    r1 = x - p1
    p2 = r1.astype(BF16).astype(F32)
    p3 = (r1 - p2).astype(BF16).astype(F32)
    return p1, p2, p3


_N_PIECES = 3
_N_EXTRA = _N_PIECES * N_HEADS
_PACK_ONE = _N_EXTRA


def _fox_placement():
    pq = np.zeros((LANES, LANES), np.float32)
    pk = np.zeros((LANES, LANES), np.float32)
    for h in range(N_HEADS):
        for p in range(_N_PIECES):
            pq[p * N_HEADS + h, HEAD_DIM + _N_PIECES * h + p] = 1.0
            pk[p * N_HEADS + h, HEAD_DIM + _N_EXTRA + _N_PIECES * h + p] = -1.0
    pq[_PACK_ONE, HEAD_DIM + _N_EXTRA:HEAD_DIM + 2 * _N_EXTRA] = 1.0
    pk[_PACK_ONE, HEAD_DIM:HEAD_DIM + _N_EXTRA] = 1.0
    return jnp.asarray(pq, BF16), jnp.asarray(pk, BF16)


def _fox_kernel(qkv_ref, misc_ref, fbt_ref, gq_ref, gk_ref, gmean_ref, pq_ref, pk_ref, o_ref, q_s, k_s, v_s, o_s):
    t = qkv_ref.shape[0]
    pr = min(FOX_PREP_ROWS, t)
    gmean = gmean_ref[...]
    upper = (lax.broadcasted_iota(jnp.int32, (pr, pr), 0) <= lax.broadcasted_iota(jnp.int32, (pr, pr), 1)).astype(BF16)
    row8 = lax.broadcasted_iota(jnp.int32, (8, pr), 0)
    lane = lax.broadcasted_iota(jnp.int32, (pr, LANES), 1)
    extra = lane - HEAD_DIM
    own = [((extra >= _N_PIECES * h) & (extra < _N_PIECES * (h + 1)))
           | ((extra >= _N_EXTRA + _N_PIECES * h) & (extra < _N_EXTRA + _N_PIECES * (h + 1))) for h in range(N_HEADS)]
    ones_v = jnp.where(extra == 0, 1.0, 0.0)
    carry = jnp.zeros((8, pr), F32)
    for r in range(t // pr):
        sl = slice(r * pr, (r + 1) * pr)
        logits_t = misc_ref[sl, :].T[0:8, :]
        log_f = _log_sigmoid(logits_t + fbt_ref[...]) * LOG2E
        c = sum(_dot(piece.astype(BF16), upper) for piece in _bf16_pieces(log_f)) + carry
        carry = jnp.broadcast_to(c[:, pr - 1:pr], (8, pr))
        p1, p2, p3 = _bf16_pieces(c)
        lo = jnp.where(row8 < N_HEADS, p1, pltpu.roll(p2, N_HEADS, axis=0))
        hi = jnp.where(row8 < N_HEADS, p3, jnp.where(row8 == _PACK_ONE - 8, 1.0, 0.0))
        packed = jnp.concatenate([lo, hi, jnp.zeros((LANES - 16, pr), F32)], axis=0).T.astype(BF16)
        q_extra = _dot(packed, pq_ref[...])
        k_extra = _dot(packed, pk_ref[...])
        qn = _group_rms(qkv_ref[sl, 0:W_MIX], gmean) * gq_ref[...] * (ATTN_SCALE * LOG2E)
        kn = _group_rms(qkv_ref[sl, W_MIX:2 * W_MIX], gmean) * gk_ref[...]
        v = qkv_ref[sl, 2 * W_MIX:3 * W_MIX]
        for h in range(N_HEADS):
            q_s[h, sl, :] = jnp.where(extra < 0, _head_column(qn, h), q_extra).astype(BF16)
            k_s[h, sl, :] = jnp.where(extra < 0, _head_column(kn, h), jnp.where(own[h], k_extra, 0.0)).astype(BF16)
            v_aug = jnp.where(extra < 0, _head_column(v, h), ones_v)
            v_s[h, :, sl] = v_aug.T.astype(BF16)

    tq = min(FOX_TILE, t)
    causal = lax.broadcasted_iota(jnp.int32, (tq, tq), 0) <= lax.broadcasted_iota(jnp.int32, (tq, tq), 1)

    def one_head(h):
        for qi in range(t // tq):
            dsl = slice(qi * tq, (qi + 1) * tq)
            q = q_s[h, dsl, :]
            s_d = jnp.where(causal, _dot_nt(k_s[h, dsl, :], q), NEG_INF)
            m = jnp.max(s_d, axis=0, keepdims=True)
            if qi:
                s_o = _dot_nt(k_s[h, 0:qi * tq, :], q)
                m = jnp.maximum(m, jnp.max(s_o, axis=0, keepdims=True))
            acc = _dot(v_s[h, :, dsl], jnp.exp2(s_d - m).astype(BF16))
            if qi:
                acc = acc + _dot(v_s[h, :, 0:qi * tq], jnp.exp2(s_o - m).astype(BF16))
            o_s[h, dsl, :] = (acc / acc[ONE_LANE:ONE_LANE + 1, :]).T

    def head_pair(hp, _):
        one_head(2 * hp)
        one_head(2 * hp + 1)
        return 0

    lax.fori_loop(0, N_HEADS // 2, head_pair, 0)
    for hp in range(N_HEADS // 2):
        o_ref[:, hp * LANES:(hp + 1) * LANES] = _pair_heads(o_s[2 * hp], o_s[2 * hp + 1]).astype(o_ref.dtype)


def _group_mean_matrix(width):
    g = np.kron(np.eye(width // HEAD_DIM, dtype=np.float32), np.full((HEAD_DIM, HEAD_DIM), 1.0 / HEAD_DIM, np.float32))
    return jnp.asarray(g, BF16)


def _fox(fqkv, misc, f_bias, g_q, g_k, b, t):
    assert t % min(FOX_TILE, t) == 0 and t % min(FOX_PREP_ROWS, t) == 0
    pr = min(FOX_PREP_ROWS, t)
    assert _MISC_FORGET == 0 and N_HEADS <= 4 and pr % LANES == 0
    fbt = jnp.broadcast_to(jnp.zeros((8,), F32).at[:N_HEADS].set(f_bias)[:, None], (8, pr))
    gq = jnp.tile(g_q, N_HEADS).reshape(1, W_MIX)
    gk = jnp.tile(g_k, N_HEADS).reshape(1, W_MIX)
    const = lambda shape: pl.BlockSpec(shape, lambda i: (0,) * len(shape))
    head_scratch = lambda dtype: pltpu.VMEM((N_HEADS, t, LANES), dtype)
    return pl.pallas_call(
        _fox_kernel,
        grid=(b,),
        in_specs=[pl.BlockSpec((t, 3 * W_MIX), lambda i: (i, 0)), pl.BlockSpec((t, LANES), lambda i: (i, 0)),
                  const((8, pr)), const((1, W_MIX)), const((1, W_MIX)), const((W_MIX, W_MIX)),
                  const((LANES, LANES)), const((LANES, LANES))],
        out_specs=pl.BlockSpec((t, W_MIX), lambda i: (i, 0)),
        out_shape=jax.ShapeDtypeStruct((b * t, W_MIX), BF16),
        scratch_shapes=[head_scratch(BF16), head_scratch(BF16), pltpu.VMEM((N_HEADS, LANES, t), BF16), head_scratch(F32)],
        compiler_params=_params("parallel"),
        name="fox",
    )(fqkv, misc, fbt, gq, gk, _group_mean_matrix(W_MIX), *_fox_placement())


def _gmlp_kernel(uv_ref, gv_ref, gmean_ref, w_ref, bias_ref, o_ref):
    c = GMLP_CHUNK
    rows = lax.broadcasted_iota(jnp.int32, (c, c), 0)
    cols = lax.broadcasted_iota(jnp.int32, (c, c), 1)
    lane_group = lax.broadcasted_iota(jnp.int32, (c, W_MIX), 1) // HEAD_DIM
    w_tril = [jnp.where(cols <= rows, w_ref[g], 0.0).astype(BF16) for g in range(N_HEADS)]
    for r in range(uv_ref.shape[0] // c):
        sl = slice(r * c, (r + 1) * c)
        u = _gelu(uv_ref[sl, 0:W_MIX])
        v = _group_rms(_gelu(uv_ref[sl, W_MIX:2 * W_MIX]), gmean_ref[...]) * gv_ref[...]
        s = bias_ref[...]
        for g in range(N_HEADS):
            s = s + _dot(w_tril[g], jnp.where(lane_group == g, v, 0.0).astype(BF16))
        o_ref[sl, :] = (u * s).astype(o_ref.dtype)


class _Part(NamedTuple):
    body: Callable
    args: list
    in_specs: list
    out_specs: list
    out_shape: list


def _const_spec(shape):
    return pl.BlockSpec(shape, lambda i: (0,) * len(shape))


def _run_parts(parts, b, name):
    n_in = [len(p.in_specs) for p in parts]
    n_out = [len(p.out_specs) for p in parts]

    def body(*refs):
        ins, outs = refs[:sum(n_in)], refs[sum(n_in):]
        for p, i0, o0 in zip(parts, np.cumsum([0] + n_in), np.cumsum([0] + n_out)):
            p.body(*ins[i0:i0 + len(p.in_specs)], *outs[o0:o0 + len(p.out_specs)])

    outs = pl.pallas_call(
        body,
        grid=(b,),
        in_specs=[s for p in parts for s in p.in_specs],
        out_specs=[s for p in parts for s in p.out_specs],
        out_shape=[s for p in parts for s in p.out_shape],
        compiler_params=_params("parallel"),
        name=name,
    )(*[a for p in parts for a in p.args])
    return [outs[o0:o0 + k] for o0, k in zip(np.cumsum([0] + n_out), n_out)]


def _gmlp(guv, g_v, w_s, b_s, b, t):
    bias = jnp.repeat(b_s.T, HEAD_DIM, axis=1)
    return _Part(
        _gmlp_kernel,
        [guv, g_v.reshape(1, W_MIX), _group_mean_matrix(W_MIX), w_s, bias],
        [pl.BlockSpec((t, 2 * W_MIX), lambda i: (i, 0)), _const_spec((1, W_MIX)), _const_spec((W_MIX, W_MIX)),
         _const_spec(w_s.shape), _const_spec(bias.shape)],
        [pl.BlockSpec((t, W_MIX), lambda i: (i, 0))],
        [jax.ShapeDtypeStruct((b * t, W_MIX), BF16)])


def _pool_kernel(z_ref, w_ref, scale_ref, o_ref):
    t = z_ref.shape[0]
    z = z_ref[...]
    row = lax.broadcasted_iota(jnp.int32, (t, W_MIX), 0)
    lane_group = lax.broadcasted_iota(jnp.int32, (t, W_MIX), 1) // HEAD_DIM
    sums = []
    s = z
    k = 1
    while k < max(POOL_SIZES):
        s = s + jnp.where(row >= k, pltpu.roll(s, k, axis=0), 0.0)
        k *= 2
        sums.append((k, s))
    win_sum = jnp.zeros_like(z)
    win = jnp.zeros((t, W_MIX), jnp.int32)
    for g, size in enumerate(POOL_SIZES):
        win_sum = jnp.where(lane_group == g, dict(sums)[size], win_sum)
        win = jnp.where(lane_group == g, size, win)
    cnt = jnp.minimum(row + 1, win).astype(F32)
    pooled = win_sum / cnt - z
    o_ref[...] = (_dot(pooled.astype(BF16), w_ref[...]) * scale_ref[...]).astype(o_ref.dtype)


def _pool(pz, w_p, scale, b, t):
    w_bd = jax.scipy.linalg.block_diag(*[w_p[g] for g in range(N_HEADS)]).astype(BF16)
    return _Part(
        _pool_kernel,
        [pz, w_bd, scale.reshape(1, W_MIX)],
        [pl.BlockSpec((t, W_MIX), lambda i: (i, 0)), _const_spec((W_MIX, W_MIX)), _const_spec((1, W_MIX))],
        [pl.BlockSpec((t, W_MIX), lambda i: (i, 0))],
        [jax.ShapeDtypeStruct((b * t, W_MIX), BF16)])


def _nsa_prep_kernel(kv_ref, kc_ref, vc_ref, posk_ref, posv_ref, kw1_ref, vw1_ref, kw2_ref, vw2_ref,
                     gkv_ref, nmask_ref, gmean_ref, c_ref, su_ref, sd_ref, gc_ref, cc_ref, csu_ref, csd_ref,
                     ks_ref, vs_ref, kw_ref, vw_ref, kcmp_ref, vcmp_ref):
    t = kv_ref.shape[0]
    x = kv_ref[...]
    normed = _group_rms(x, gmean_ref[...]) * gkv_ref[...]
    y = _rope(jnp.where(nmask_ref[...] > 0.5, normed, x), c_ref[...], su_ref[...], sd_ref[...])
    lane = lax.broadcasted_iota(jnp.int32, (t, LANES), 1)
    block = lax.broadcasted_iota(jnp.int32, (t, LANES), 0) // SEL_LEN
    key = lane < HEAD_DIM
    ks_v, kw_v = y[:, 0:LANES], y[:, LANES:2 * LANES]
    ks_ref[...] = jnp.where(key, ks_v, jnp.where(lane - HEAD_DIM == block, 1.0, 0.0)).astype(BF16)
    vs_ref[...] = jnp.where(key, pltpu.roll(ks_v, HEAD_DIM, axis=1), _ones_lane((t, LANES))).astype(BF16)
    kw_ref[...] = jnp.where(key, kw_v, 0.0).astype(BF16)
    vw_ref[...] = jnp.where(key, pltpu.roll(kw_v, HEAD_DIM, axis=1), _ones_lane((t, LANES))).astype(BF16)

    half = kc_ref.shape[1]
    nb = kc_ref.shape[0]

    def hidden(x_ref, pos_ref, w1_ref):
        top = _dot((x_ref[...] + pos_ref[:, 0:half]).astype(BF16), w1_ref[0:half, :])
        bot = _dot((x_ref[...] + pos_ref[:, half:2 * half]).astype(BF16), w1_ref[half:2 * half, :])
        return _gelu(top + pltpu.roll(bot, nb - 1, axis=0)).astype(BF16)

    kv_cmp = _dot(hidden(kc_ref, posk_ref, kw1_ref), kw2_ref[...]) + _dot(hidden(vc_ref, posv_ref, vw1_ref), vw2_ref[...])
    key = lax.broadcasted_iota(jnp.int32, kv_cmp.shape, 1) < HEAD_DIM
    normed = _group_rms(kv_cmp, gmean_ref[0:LANES, 0:LANES]) * gc_ref[...]
    y = _rope(jnp.where(key, normed, kv_cmp), cc_ref[...], csu_ref[...], csd_ref[...])
    kcmp_ref[...] = jnp.where(key, y, 0.0).astype(BF16)
    vcmp_ref[...] = jnp.where(key, pltpu.roll(y, HEAD_DIM, axis=1), _ones_lane(y.shape)).T.astype(BF16)


def _rope_tables(pos):
    inv = ROPE_THETA ** (-jnp.arange(ROPE_HALF, dtype=F32) * 2.0 / ROPE_DIM)
    ang = pos.astype(F32)[:, None] * inv[None, :]
    cos, sin = jnp.cos(ang), jnp.sin(ang)
    n = pos.shape[0]
    zero = jnp.zeros((n, ROPE_HALF), F32)
    rest0 = jnp.zeros((n, HEAD_DIM - ROPE_DIM), F32)
    c = jnp.concatenate([cos, cos, rest0 + 1.0], axis=1)
    s_up = jnp.concatenate([-sin, zero, rest0], axis=1)
    s_dn = jnp.concatenate([zero, sin, rest0], axis=1)
    return c, s_up, s_dn


def _identity_tables(n):
    return jnp.ones((n, HEAD_DIM), F32), jnp.zeros((n, HEAD_DIM), F32), jnp.zeros((n, HEAD_DIM), F32)


def _nsa_prep(nkv, kc2, vc2, g_kc, g_ks, g_kw, pos_k, k_w1, k_w2, pos_v, v_w1, v_w2, b, t):
    nb = t // CMP_STRIDE
    assert HEAD_DIM + t // SEL_LEN <= LANES
    flat = CMP_STRIDE * HEAD_DIM
    rope_t = _rope_tables(jnp.arange(t))
    iden_t = _identity_tables(t)
    tabs = [jnp.concatenate([r, i, r, i], axis=1) for r, i in zip(rope_t, iden_t)]
    rope_c = _rope_tables(jnp.arange(nb) * CMP_STRIDE + CMP_LEN - 1)
    iden_c = _identity_tables(nb)
    tabs_c = [jnp.concatenate([r, i], axis=1) for r, i in zip(rope_c, iden_c)]
    one = jnp.ones((HEAD_DIM,), F32)
    gkv = jnp.concatenate([g_ks, one, g_kw, one]).reshape(1, W_MIX)
    nmask = jnp.concatenate([one, 0 * one, one, 0 * one]).reshape(1, W_MIX)
    gc = jnp.concatenate([g_kc, one]).reshape(1, LANES)
    zpad = jnp.zeros_like(k_w2)
    kw2 = jnp.concatenate([k_w2, zpad], axis=1).astype(BF16)
    vw2 = jnp.concatenate([zpad, v_w2], axis=1).astype(BF16)
    const = _const_spec
    seq = lambda width: pl.BlockSpec((t, width), lambda i: (i, 0))
    cmp_in = pl.BlockSpec((nb, flat), lambda i: (i, 0))
    cmp_out = pl.BlockSpec((nb, LANES), lambda i: (i, 0))
    return _Part(
        _nsa_prep_kernel,
        [nkv, kc2, vc2, pos_k.reshape(1, 2 * flat), pos_v.reshape(1, 2 * flat),
         k_w1.astype(BF16), v_w1.astype(BF16), kw2, vw2, gkv, nmask, _group_mean_matrix(W_MIX), *tabs, gc, *tabs_c],
        [seq(W_MIX), cmp_in, cmp_in, const((1, 2 * flat)), const((1, 2 * flat)),
         const(k_w1.shape), const(v_w1.shape), const(kw2.shape), const(vw2.shape),
         const((1, W_MIX)), const((1, W_MIX)), const((W_MIX, W_MIX)),
         const((t, W_MIX)), const((t, W_MIX)), const((t, W_MIX)),
         const((1, LANES)), const((nb, LANES)), const((nb, LANES)), const((nb, LANES))],
        [seq(LANES)] * 4 + [cmp_out, pl.BlockSpec((LANES, nb), lambda i: (i, 0))],
        [jax.ShapeDtypeStruct((b * t, LANES), BF16)] * 4
        + [jax.ShapeDtypeStruct((b * nb, LANES), BF16), jax.ShapeDtypeStruct((b * LANES, nb), BF16)])


def _nsa_kernel(q_ref, misc_ref, gb_ref, gq_ref, gmean_ref, c_ref, su_ref, sd_ref, ovt_ref,
                kcmp_ref, vcmp_ref, ks_ref, vs_ref, kw_ref, vw_ref, o_ref, *, n_top):
    tq = q_ref.shape[0]
    qi = pl.program_id(1)
    t0 = qi * tq
    nb = kcmp_ref.shape[0]
    nsel = ks_ref.shape[0] // SEL_LEN
    rows = N_HEADS * tq

    qn = _rope(_group_rms(q_ref[...], gmean_ref[...]) * gq_ref[...], c_ref[...], su_ref[...], sd_ref[...]) * (ATTN_SCALE * LOG2E)
    q_cols = [_head_column(qn, h) for h in range(N_HEADS)]
    lane_q = lax.broadcasted_iota(jnp.int32, (tq, LANES), 1)

    q0 = jnp.concatenate([jnp.where(lane_q < HEAD_DIM, col, 0.0) for col in q_cols], axis=0).astype(BF16)
    cmp_end = lax.broadcasted_iota(jnp.int32, (nb, rows), 0) * CMP_STRIDE + (CMP_LEN - 1)
    qpos_c = (lax.broadcasted_iota(jnp.int32, (nb, rows), 1) & (tq - 1)) + t0
    m_cmp = cmp_end <= qpos_c
    s = jnp.where(m_cmp, _dot_nt(kcmp_ref[...], q0), NEG_INF)
    e = jnp.where(m_cmp, jnp.exp2(s - jnp.max(s, axis=0, keepdims=True)), 0.0)
    denom = jnp.sum(e, axis=0, keepdims=True)
    inv = 1.0 / jnp.where(denom > 0.0, denom, 1.0)
    p_cmp = e * inv
    o_cmp = (_dot(vcmp_ref[...], e.astype(BF16)) * inv).T

    p_sum = p_cmp[:, 0:tq] + p_cmp[:, tq:2 * tq] + p_cmp[:, 2 * tq:3 * tq] + p_cmp[:, 3 * tq:4 * tq]
    nsp = -(-nsel // 8) * 8
    imp = _dot(ovt_ref[0:nsp, :], p_sum, precision=HIGHEST)
    blk = lax.broadcasted_iota(jnp.int32, (nsp, tq), 0)
    cur = (lax.broadcasted_iota(jnp.int32, (nsp, tq), 1) + t0) // SEL_LEN
    valid = blk <= cur
    forced = ((blk == 0) | (blk == cur) | (blk == cur - 1)).astype(F32)
    imp = jnp.where(valid, imp + SEL_FORCE * forced, NEG_INF)
    cur_row = (lax.broadcasted_iota(jnp.int32, (1, tq), 1) + t0) // SEL_LEN
    outranked = jnp.zeros((nsp, tq), F32)
    for j in range(nsel):
        rival = imp[j:j + 1, :]
        wins = (rival > imp) | ((rival == imp) & (blk > j))
        outranked = outranked + jnp.where(wins & (cur_row >= j), 1.0, 0.0)
    keep = (outranked < n_top) & valid
    bias_t = jnp.where(keep | (blk >= nsel), 0.0, NEG_INF)
    if nsp < LANES:
        bias_t = jnp.concatenate([bias_t, jnp.zeros((LANES - nsp, tq), F32)], axis=0)
    bias = jnp.concatenate([bias_t[:, c * LANES:(c + 1) * LANES].T for c in range(tq // LANES)], axis=0)
    bias = pltpu.roll(bias, HEAD_DIM, axis=1)
    q = jnp.concatenate([jnp.where(lane_q < HEAD_DIM, col, bias) for col in q_cols], axis=0).astype(BF16)

    trow = lax.broadcasted_iota(jnp.int32, (rows, tq), 0) & (tq - 1)
    tcol = lax.broadcasted_iota(jnp.int32, (rows, tq), 1)
    causal_bias = jnp.where(tcol <= trow, 0.0, NEG_INF)

    def tile(j):
        return pl.ds(pl.multiple_of(j * tq, tq), tq)

    def sel_step(kj, extra_bias, st):
        m, acc = st
        s = _dot_nt(q, ks_ref[tile(kj), :])
        if extra_bias is not None:
            s = s + extra_bias
        m_new = jnp.maximum(m, jnp.max(s, axis=-1, keepdims=True))
        acc = jnp.exp2(m - m_new) * acc + _dot(jnp.exp2(s - m_new).astype(BF16), vs_ref[tile(kj), :])
        return m_new, acc

    st = (jnp.full((rows, 1), NEG_INF, F32), jnp.zeros((rows, LANES), F32))
    st = lax.fori_loop(0, qi, lambda kj, st: sel_step(kj, None, st), st)
    _, acc = sel_step(qi, causal_bias, st)
    o_sel = acc / acc[:, ONE_LANE:ONE_LANE + 1]

    wt = min(NSA_WINDOW_TILE, tq)
    span = WINDOW // wt
    wrow = lax.broadcasted_iota(jnp.int32, (N_HEADS * wt, wt), 0) & (wt - 1)
    wcol = lax.broadcasted_iota(jnp.int32, (N_HEADS * wt, wt), 1)
    o_win = []
    for sub in range(tq // wt):
        q_sub = jnp.concatenate([q[h * tq + sub * wt:h * tq + (sub + 1) * wt] for h in range(N_HEADS)], axis=0)
        diag = qi * (tq // wt) + sub
        parts = []
        for d in range(span, -1, -1):
            ksl = pl.ds(pl.multiple_of(jnp.maximum(diag - d, 0) * wt, wt), wt)
            if d == 0:
                bias_w = jnp.where(wcol <= wrow, 0.0, NEG_INF)
            elif d == span:
                bias_w = jnp.where((wcol > wrow) & (diag >= d), 0.0, NEG_INF)
            else:
                bias_w = jnp.where(diag >= d, 0.0, NEG_INF)
            parts.append((_dot_nt(q_sub, kw_ref[ksl, :]) + bias_w, ksl))
        m = functools.reduce(jnp.maximum, [jnp.max(s, axis=-1, keepdims=True) for s, _ in parts])
        acc = sum(_dot(jnp.exp2(s - m).astype(BF16), vw_ref[ksl, :]) for s, ksl in parts)
        o_win.append(acc / acc[:, ONE_LANE:ONE_LANE + 1])

    gate = jax.nn.sigmoid(misc_ref[...] + gb_ref[...])
    outs = []
    for h in range(N_HEADS):
        hr = slice(h * tq, (h + 1) * tq)
        g0 = _MISC_GATE + 3 * h
        o_win_h = jnp.concatenate([part[h * wt:(h + 1) * wt] for part in o_win], axis=0)
        outs.append(gate[:, g0:g0 + 1] * o_cmp[hr] + gate[:, g0 + 1:g0 + 2] * o_sel[hr] + gate[:, g0 + 2:g0 + 3] * o_win_h)
    for hp in range(N_HEADS // 2):
        o_ref[:, hp * LANES:(hp + 1) * LANES] = _pair_heads(outs[2 * hp], outs[2 * hp + 1]).astype(o_ref.dtype)


def _cmp_to_sel_overlap_t(nb, nsel):
    nc = nb - 1
    cs = np.arange(nc) * CMP_STRIDE
    ss = np.arange(nsel) * SEL_LEN
    ov = np.clip(np.minimum(cs[:, None] + CMP_LEN, ss[None, :] + SEL_LEN) - np.maximum(cs[:, None], ss[None, :]), 0, None)
    out = np.zeros((LANES, nb), np.float32)
    out[:nsel, :nc] = (ov / CMP_LEN).T
    return jnp.asarray(out)


def _nsa(nq, misc, gate_b, g_q, ks, vs, kw, vw, kcmp, vcmp, b, t):
    tq = min(NSA_TILE, t)
    nb = t // CMP_STRIDE
    nsel = t // SEL_LEN
    wt = min(NSA_WINDOW_TILE, tq)
    assert HEAD_DIM + nsel <= LANES and nb % 8 == 0 and tq & (tq - 1) == 0 and tq % LANES == 0
    assert WINDOW % wt == 0 and tq % wt == 0 and wt & (wt - 1) == 0
    gb = jnp.zeros((1, LANES), F32).at[0, _MISC_GATE:_MISC_GATE + 3 * N_HEADS].set(gate_b)
    gq = jnp.tile(g_q, N_HEADS).reshape(1, W_MIX)
    tabs = [jnp.tile(x, (1, N_HEADS)) for x in _rope_tables(jnp.arange(t))]
    nq_tiles = t // tq
    const = lambda shape: pl.BlockSpec(shape, lambda i, j: (0,) * len(shape))
    qtile = lambda width: pl.BlockSpec((tq, width), lambda i, j: (i * nq_tiles + j, 0))
    ptile = pl.BlockSpec((tq, W_MIX), lambda i, j: (j, 0))
    seq = pl.BlockSpec((t, LANES), lambda i, j: (i, 0))
    cmp = pl.BlockSpec((nb, LANES), lambda i, j: (i, 0))
    return pl.pallas_call(
        functools.partial(_nsa_kernel, n_top=min(SEL_TOP, nsel)),
        grid=(b, nq_tiles),
        in_specs=[qtile(W_MIX), qtile(LANES), const((1, LANES)), const((1, W_MIX)), const((W_MIX, W_MIX)),
                  ptile, ptile, ptile, const((LANES, nb)), cmp, pl.BlockSpec((LANES, nb), lambda i, j: (i, 0)),
                  seq, seq, seq, seq],
        out_specs=qtile(W_MIX),
        out_shape=jax.ShapeDtypeStruct((b * t, W_MIX), BF16),
        compiler_params=_params("parallel", "arbitrary"),
        name="nsa",
    )(nq, misc, gb, gq, _group_mean_matrix(W_MIX), *tabs, _cmp_to_sel_overlap_t(nb, nsel),
      kcmp, vcmp, ks, vs, kw, vw)


def kernel(x, ffn1_norm, ffn1_w1, ffn1_w3, ffn1_w2, mix_norm, w_in, w_out, fox_f_bias, fox_q_norm, fox_k_norm, gmlp_v_norm, gmlp_w_s, gmlp_b_s, nsa_q_norm, nsa_kc_norm, nsa_ks_norm, nsa_kw_norm, nsa_cmp_pos_k, nsa_cmp_k_w1, nsa_cmp_k_w2, nsa_cmp_pos_v, nsa_cmp_v_w1, nsa_cmp_v_w2, nsa_gate_bias, pool_w, pool_scale, ffn2_norm, ffn2_w1, ffn2_w3, ffn2_w2):
    b, t, d = x.shape
    assert t % GMLP_CHUNK == 0 and t >= CMP_LEN
    n = b * t
    xf = x.reshape(n, d)
    for l in range(ffn1_norm.shape[0]):
        xf = _ffn(xf, ffn1_norm[l], ffn1_w1[l], ffn1_w3[l], ffn1_w2[l])
        fqkv, guv, nq, nkv, pz, kc, vc, misc = _inproj(xf, mix_norm[l], w_in[l])
        o_a = _fox(fqkv, misc, fox_f_bias[l], fox_q_norm[l], fox_k_norm[l], b, t)
        (o_b,), (o_d,), (ks, vs, kw, vw, kcmp, vcmp) = _run_parts([
            _gmlp(guv, gmlp_v_norm[l], gmlp_w_s[l], gmlp_b_s[l], b, t),
            _pool(pz, pool_w[l], pool_scale[l], b, t),
            _nsa_prep(nkv, kc, vc, nsa_kc_norm[l], nsa_ks_norm[l], nsa_kw_norm[l],
                      nsa_cmp_pos_k[l], nsa_cmp_k_w1[l], nsa_cmp_k_w2[l],
                      nsa_cmp_pos_v[l], nsa_cmp_v_w1[l], nsa_cmp_v_w2[l], b, t)], b, "seq_mixers")
        o_c = _nsa(nq, misc, nsa_gate_bias[l], nsa_q_norm[l], ks, vs, kw, vw, kcmp, vcmp, b, t)
        xf = _ffn(xf, ffn2_norm[l], ffn2_w1[l], ffn2_w3[l], ffn2_w2[l], mixers=(o_a, o_b, o_c, o_d), w_out=w_out[l])
    return xf.reshape(b, t, d)
```

```python
import functools
from typing import Callable, NamedTuple

import numpy as np
import jax
import jax.numpy as jnp
from jax import lax
from jax.experimental import pallas as pl
from jax.experimental.pallas import tpu as pltpu

F32 = jnp.float32
BF16 = jnp.bfloat16
HIGHEST = lax.Precision.HIGHEST

HEAD_DIM = 64
N_HEADS = 4
W_MIX = N_HEADS * HEAD_DIM
ROPE_THETA = 500000.0
ROPE_DIM = HEAD_DIM // 4
ROPE_HALF = ROPE_DIM // 2
GMLP_CHUNK = 128
CMP_LEN = 32
CMP_STRIDE = 16
SEL_LEN = 64
SEL_TOP = 16
WINDOW = 512
POOL_SIZES = (2, 4, 8, 16)
FFN_RES_WEIGHT = 0.5
EPS = 1e-6
NEG_INF = -1e30
SEL_FORCE = 1e3
ATTN_SCALE = HEAD_DIM ** -0.5
LOG2E = 1.4426950408889634

LANES = 128
VMEM_LIMIT = 52 * 1024 * 1024

FFN_TOKENS = 1024
FFN_MIX_TOKENS = 1024
FFN_HIDDEN = 256
FOX_PREP_ROWS = 512
FOX_TILE = 1024
NSA_TILE = 512
NSA_WINDOW_TILE = 512
ONE_LANE = HEAD_DIM


def _params(*sem):
    return pltpu.CompilerParams(dimension_semantics=sem, vmem_limit_bytes=VMEM_LIMIT)


def _dot(a, b, **kw):
    return jnp.dot(a, b, preferred_element_type=F32, **kw)


def _dot_nt(a, b):
    return lax.dot_general(a, b, (((1,), (1,)), ((), ())), preferred_element_type=F32)


def _rms(x):
    return x * lax.rsqrt(jnp.mean(x * x, axis=-1, keepdims=True) + EPS)


def _group_rms(x, gmean):
    ms = _dot((x * x).astype(BF16), gmean)
    return x * lax.rsqrt(ms + EPS)


def _rope(x, c, s_up, s_dn):
    w = x.shape[-1]
    return x * c + pltpu.roll(x, w - ROPE_HALF, axis=1) * s_up + pltpu.roll(x, ROPE_HALF, axis=1) * s_dn


_GELU_C = 0.7978845608028654


def _gelu(x):
    half = 0.5 * x
    return half + half * jnp.tanh(x * (_GELU_C + (_GELU_C * 0.044715) * (x * x)))


def _log_sigmoid(x):
    return jnp.minimum(x, 0.0) - jnp.log(1.0 + jnp.exp(-jnp.abs(x)))


def _head_column(x, h):
    col = x[:, (h // 2) * LANES:(h // 2 + 1) * LANES]
    return pltpu.roll(col, HEAD_DIM, axis=1) if h % 2 else col


def _pair_heads(even, odd):
    lane = lax.broadcasted_iota(jnp.int32, even.shape, 1)
    return jnp.where(lane < HEAD_DIM, even, pltpu.roll(odd, HEAD_DIM, axis=1))


def _ones_lane(shape):
    return jnp.where(lax.broadcasted_iota(jnp.int32, shape, 1) == ONE_LANE, 1.0, 0.0)


def _ffn_kernel(x_ref, g_ref, w1_ref, w3_ref, w2_ref, *rest):
    *mix_refs, o_ref = rest
    x = x_ref[...]
    if mix_refs:
        *mixers, wout_ref = mix_refs
        for k, ref in enumerate(mixers):
            x = x + _dot(ref[...], wout_ref[k * W_MIX:(k + 1) * W_MIX, :])
    h = (_rms(x) * g_ref[...]).astype(BF16)
    acc = jnp.zeros_like(x)
    for c in range(w1_ref.shape[1] // FFN_HIDDEN):
        cs = slice(c * FFN_HIDDEN, (c + 1) * FFN_HIDDEN)
        a = _dot(h, w1_ref[:, cs])
        b = _dot(h, w3_ref[:, cs])
        acc = acc + _dot((a * jax.nn.sigmoid(a) * b).astype(BF16), w2_ref[cs, :])
    o_ref[...] = x + FFN_RES_WEIGHT * acc


def _ffn(x, g, w1, w3, w2, mixers=(), w_out=None):
    n, d = x.shape
    tm = min(FFN_MIX_TOKENS if mixers else FFN_TOKENS, n)
    resident = lambda shape: pl.BlockSpec(shape, lambda i: (0, 0), pipeline_mode=pl.Buffered(1))
    mix_specs = [pl.BlockSpec((tm, W_MIX), lambda i: (i, 0)) for _ in mixers] + ([resident(w_out.shape)] if mixers else [])
    mix_args = list(mixers) + ([w_out.astype(BF16)] if mixers else [])
    return pl.pallas_call(
        _ffn_kernel,
        grid=(n // tm,),
        in_specs=[pl.BlockSpec((tm, d), lambda i: (i, 0)), resident((1, d)),
                  resident(w1.shape), resident(w3.shape), resident(w2.shape)] + mix_specs,
        out_specs=pl.BlockSpec((tm, d), lambda i: (i, 0)),
        out_shape=jax.ShapeDtypeStruct((n, d), F32),
        compiler_params=_params("parallel"),
        name="ffn_mix" if mixers else "ffn",
    )(x, g.reshape(1, d), w1.astype(BF16), w3.astype(BF16), w2.astype(BF16), *mix_args)


_IN_GROUPS = (768, 512, 256, 256, 256, 128, 128)
_MISC_FORGET = 0
_MISC_GATE = 4


def _inproj_kernel(x_ref, g_ref, w_ref, fqkv_ref, guv_ref, nq_ref, nkv_ref, pz_ref, kc_ref, vc_ref, misc_ref, kcvc_s):
    h = (_rms(x_ref[...]) * g_ref[...]).astype(BF16)
    off = 0
    for ref, width in zip((fqkv_ref, guv_ref, nq_ref, nkv_ref, pz_ref), _IN_GROUPS[:5]):
        ref[...] = _dot(h, w_ref[:, off:off + width])
        off += width
    tail = _dot(h, w_ref[:, off:off + 2 * LANES])
    misc_ref[...] = tail[:, LANES:]
    kcvc_s[...] = tail[:, :LANES]
    groups = x_ref.shape[0] // CMP_STRIDE
    lane = lax.broadcasted_iota(jnp.int32, (groups, LANES), 1)
    for j in range(0, CMP_STRIDE, 2):
        even = kcvc_s[pl.ds(j, groups, stride=CMP_STRIDE), :]
        odd = kcvc_s[pl.ds(j + 1, groups, stride=CMP_STRIDE), :]
        col = slice((j // 2) * LANES, (j // 2 + 1) * LANES)
        kc_ref[:, col] = jnp.where(lane < HEAD_DIM, even, pltpu.roll(odd, HEAD_DIM, axis=1))
        vc_ref[:, col] = jnp.where(lane < HEAD_DIM, pltpu.roll(even, HEAD_DIM, axis=1), odd)


def _relayout_w_in(w_in):
    o = np.cumsum((0, 256, 256, 256, 4, 256, 256, 256, 64, 64, 64, 64, 64, 64, 12, 256))
    fq, ff, gu, nq, nkc, nks, ng, pz, end = o[0], o[3], o[4], o[6], o[7], o[9], o[13], o[14], o[15]
    d = w_in.shape[0]
    pad = jnp.zeros((d, LANES - 16), w_in.dtype)
    return jnp.concatenate([
        w_in[:, fq:ff], w_in[:, gu:nq], w_in[:, nq:nkc], w_in[:, nks:ng], w_in[:, pz:end],
        w_in[:, nkc:nks], w_in[:, ff:gu], w_in[:, ng:pz], pad], axis=1)


def _inproj(x, g, w_in):
    n, d = x.shape
    tm = min(FFN_TOKENS, n)
    assert tm % (8 * CMP_STRIDE) == 0
    w = _relayout_w_in(w_in).astype(BF16)
    flat = CMP_STRIDE * HEAD_DIM
    outs = [(tm, n, wd) for wd in (768, 512, 256, 256, 256)] + [(tm // CMP_STRIDE, n // CMP_STRIDE, flat)] * 2 + [(tm, n, LANES)]
    return pl.pallas_call(
        _inproj_kernel,
        grid=(n // tm,),
        in_specs=[
            pl.BlockSpec((tm, d), lambda i: (i, 0)),
            pl.BlockSpec((1, d), lambda i: (0, 0)),
            pl.BlockSpec(w.shape, lambda i: (0, 0)),
        ],
        out_specs=[pl.BlockSpec((rows, wd), lambda i: (i, 0)) for rows, _, wd in outs],
        out_shape=[jax.ShapeDtypeStruct((total, wd), F32) for _, total, wd in outs],
        scratch_shapes=[pltpu.VMEM((tm, LANES), F32)],
        compiler_params=_params("parallel"),
        name="inproj",
    )(x, g.reshape(1, d), w)


def _bf16_pieces(x):
    p1 = x.astype(BF16).astype(F32)
    r1 = x - p1
    p2 = r1.astype(BF16).astype(F32)
    p3 = (r1 - p2).astype(BF16).astype(F32)
    return p1, p2, p3


_N_PIECES = 3
_N_EXTRA = _N_PIECES * N_HEADS
_PACK_ONE = _N_EXTRA


def _fox_placement():
    pq = np.zeros((LANES, LANES), np.float32)
    pk = np.zeros((LANES, LANES), np.float32)
    for h in range(N_HEADS):
        for p in range(_N_PIECES):
            pq[p * N_HEADS + h, HEAD_DIM + _N_PIECES * h + p] = 1.0
            pk[p * N_HEADS + h, HEAD_DIM + _N_EXTRA + _N_PIECES * h + p] = -1.0
    pq[_PACK_ONE, HEAD_DIM + _N_EXTRA:HEAD_DIM + 2 * _N_EXTRA] = 1.0
    pk[_PACK_ONE, HEAD_DIM:HEAD_DIM + _N_EXTRA] = 1.0
    return jnp.asarray(pq, BF16), jnp.asarray(pk, BF16)


def _fox_kernel(qkv_ref, misc_ref, fbt_ref, gq_ref, gk_ref, gmean_ref, pq_ref, pk_ref, o_ref, q_s, k_s, v_s, o_s):
    t = qkv_ref.shape[0]
    pr = min(FOX_PREP_ROWS, t)
    gmean = gmean_ref[...]
    upper = (lax.broadcasted_iota(jnp.int32, (pr, pr), 0) <= lax.broadcasted_iota(jnp.int32, (pr, pr), 1)).astype(BF16)
    row8 = lax.broadcasted_iota(jnp.int32, (8, pr), 0)
    lane = lax.broadcasted_iota(jnp.int32, (pr, LANES), 1)
    extra = lane - HEAD_DIM
    own = [((extra >= _N_PIECES * h) & (extra < _N_PIECES * (h + 1)))
           | ((extra >= _N_EXTRA + _N_PIECES * h) & (extra < _N_EXTRA + _N_PIECES * (h + 1))) for h in range(N_HEADS)]
    ones_v = jnp.where(extra == 0, 1.0, 0.0)
    carry = jnp.zeros((8, pr), F32)
    for r in range(t // pr):
        sl = slice(r * pr, (r + 1) * pr)
        logits_t = misc_ref[sl, :].T[0:8, :]
        log_f = _log_sigmoid(logits_t + fbt_ref[...]) * LOG2E
        c = sum(_dot(piece.astype(BF16), upper) for piece in _bf16_pieces(log_f)) + carry
        carry = jnp.broadcast_to(c[:, pr - 1:pr], (8, pr))
        p1, p2, p3 = _bf16_pieces(c)
        lo = jnp.where(row8 < N_HEADS, p1, pltpu.roll(p2, N_HEADS, axis=0))
        hi = jnp.where(row8 < N_HEADS, p3, jnp.where(row8 == _PACK_ONE - 8, 1.0, 0.0))
        packed = jnp.concatenate([lo, hi, jnp.zeros((LANES - 16, pr), F32)], axis=0).T.astype(BF16)
        q_extra = _dot(packed, pq_ref[...])
        k_extra = _dot(packed, pk_ref[...])
        qn = _group_rms(qkv_ref[sl, 0:W_MIX], gmean) * gq_ref[...] * (ATTN_SCALE * LOG2E)
        kn = _group_rms(qkv_ref[sl, W_MIX:2 * W_MIX], gmean) * gk_ref[...]
        v = qkv_ref[sl, 2 * W_MIX:3 * W_MIX]
        for h in range(N_HEADS):
            q_s[h, sl, :] = jnp.where(extra < 0, _head_column(qn, h), q_extra).astype(BF16)
            k_s[h, sl, :] = jnp.where(extra < 0, _head_column(kn, h), jnp.where(own[h], k_extra, 0.0)).astype(BF16)
            v_aug = jnp.where(extra < 0, _head_column(v, h), ones_v)
            v_s[h, :, sl] = v_aug.T.astype(BF16)

    tq = min(FOX_TILE, t)
    causal = lax.broadcasted_iota(jnp.int32, (tq, tq), 0) <= lax.broadcasted_iota(jnp.int32, (tq, tq), 1)

    def one_head(h):
        for qi in range(t // tq):
            dsl = slice(qi * tq, (qi + 1) * tq)
            q = q_s[h, dsl, :]
            s_d = jnp.where(causal, _dot_nt(k_s[h, dsl, :], q), NEG_INF)
            m = jnp.max(s_d, axis=0, keepdims=True)
            if qi:
                s_o = _dot_nt(k_s[h, 0:qi * tq, :], q)
                m = jnp.maximum(m, jnp.max(s_o, axis=0, keepdims=True))
            acc = _dot(v_s[h, :, dsl], jnp.exp2(s_d - m).astype(BF16))
            if qi:
                acc = acc + _dot(v_s[h, :, 0:qi * tq], jnp.exp2(s_o - m).astype(BF16))
            o_s[h, dsl, :] = (acc / acc[ONE_LANE:ONE_LANE + 1, :]).T

    def head_pair(hp, _):
        one_head(2 * hp)
        one_head(2 * hp + 1)
        return 0

    lax.fori_loop(0, N_HEADS // 2, head_pair, 0)
    for hp in range(N_HEADS // 2):
        o_ref[:, hp * LANES:(hp + 1) * LANES] = _pair_heads(o_s[2 * hp], o_s[2 * hp + 1]).astype(o_ref.dtype)


def _group_mean_matrix(width):
    g = np.kron(np.eye(width // HEAD_DIM, dtype=np.float32), np.full((HEAD_DIM, HEAD_DIM), 1.0 / HEAD_DIM, np.float32))
    return jnp.asarray(g, BF16)


def _fox(fqkv, misc, f_bias, g_q, g_k, b, t):
    assert t % min(FOX_TILE, t) == 0 and t % min(FOX_PREP_ROWS, t) == 0
    pr = min(FOX_PREP_ROWS, t)
    assert _MISC_FORGET == 0 and N_HEADS <= 4 and pr % LANES == 0
    fbt = jnp.broadcast_to(jnp.zeros((8,), F32).at[:N_HEADS].set(f_bias)[:, None], (8, pr))
    gq = jnp.tile(g_q, N_HEADS).reshape(1, W_MIX)
    gk = jnp.tile(g_k, N_HEADS).reshape(1, W_MIX)
    const = lambda shape: pl.BlockSpec(shape, lambda i: (0,) * len(shape))
    head_scratch = lambda dtype: pltpu.VMEM((N_HEADS, t, LANES), dtype)
    return pl.pallas_call(
        _fox_kernel,
        grid=(b,),
        in_specs=[pl.BlockSpec((t, 3 * W_MIX), lambda i: (i, 0)), pl.BlockSpec((t, LANES), lambda i: (i, 0)),
                  const((8, pr)), const((1, W_MIX)), const((1, W_MIX)), const((W_MIX, W_MIX)),
                  const((LANES, LANES)), const((LANES, LANES))],
        out_specs=pl.BlockSpec((t, W_MIX), lambda i: (i, 0)),
        out_shape=jax.ShapeDtypeStruct((b * t, W_MIX), BF16),
        scratch_shapes=[head_scratch(BF16), head_scratch(BF16), pltpu.VMEM((N_HEADS, LANES, t), BF16), head_scratch(F32)],
        compiler_params=_params("parallel"),
        name="fox",
    )(fqkv, misc, fbt, gq, gk, _group_mean_matrix(W_MIX), *_fox_placement())


def _gmlp_kernel(uv_ref, gv_ref, gmean_ref, w_ref, bias_ref, o_ref):
    c = GMLP_CHUNK
    rows = lax.broadcasted_iota(jnp.int32, (c, c), 0)
    cols = lax.broadcasted_iota(jnp.int32, (c, c), 1)
    lane_group = lax.broadcasted_iota(jnp.int32, (c, W_MIX), 1) // HEAD_DIM
    w_cat = jnp.concatenate([jnp.where(cols <= rows, w_ref[g], 0.0).astype(BF16) for g in range(N_HEADS)], axis=1)
    for r in range(uv_ref.shape[0] // c):
        sl = slice(r * c, (r + 1) * c)
        u = _gelu(uv_ref[sl, 0:W_MIX])
        v = (_group_rms(_gelu(uv_ref[sl, W_MIX:2 * W_MIX]), gmean_ref[...]) * gv_ref[...]).astype(BF16)
        v_stack = jnp.concatenate([jnp.where(lane_group == g, v, jnp.zeros_like(v)) for g in range(N_HEADS)], axis=0)
        o_ref[sl, :] = (u * (bias_ref[...] + _dot(w_cat, v_stack))).astype(o_ref.dtype)


class _Part(NamedTuple):
    body: Callable
    args: list
    in_specs: list
    out_specs: list
    out_shape: list


def _const_spec(shape):
    return pl.BlockSpec(shape, lambda i: (0,) * len(shape))


def _run_parts(parts, b, name):
    n_in = [len(p.in_specs) for p in parts]
    n_out = [len(p.out_specs) for p in parts]

    def body(*refs):
        ins, outs = refs[:sum(n_in)], refs[sum(n_in):]
        for p, i0, o0 in zip(parts, np.cumsum([0] + n_in), np.cumsum([0] + n_out)):
            p.body(*ins[i0:i0 + len(p.in_specs)], *outs[o0:o0 + len(p.out_specs)])

    outs = pl.pallas_call(
        body,
        grid=(b,),
        in_specs=[s for p in parts for s in p.in_specs],
        out_specs=[s for p in parts for s in p.out_specs],
        out_shape=[s for p in parts for s in p.out_shape],
        compiler_params=_params("parallel"),
        name=name,
    )(*[a for p in parts for a in p.args])
    return [outs[o0:o0 + k] for o0, k in zip(np.cumsum([0] + n_out), n_out)]


def _gmlp(guv, g_v, w_s, b_s, b, t):
    bias = jnp.repeat(b_s.T, HEAD_DIM, axis=1)
    return _Part(
        _gmlp_kernel,
        [guv, g_v.reshape(1, W_MIX), _group_mean_matrix(W_MIX), w_s, bias],
        [pl.BlockSpec((t, 2 * W_MIX), lambda i: (i, 0)), _const_spec((1, W_MIX)), _const_spec((W_MIX, W_MIX)),
         _const_spec(w_s.shape), _const_spec(bias.shape)],
        [pl.BlockSpec((t, W_MIX), lambda i: (i, 0))],
        [jax.ShapeDtypeStruct((b * t, W_MIX), BF16)])


def _pool_kernel(z_ref, inv_cnt_ref, w_ref, scale_ref, o_ref):
    t = z_ref.shape[0]
    z = z_ref[...]
    row = lax.broadcasted_iota(jnp.int32, (t, W_MIX), 0)
    lane_group = lax.broadcasted_iota(jnp.int32, (t, W_MIX), 1) // HEAD_DIM
    sums = {}
    s = z
    k = 1
    while k < max(POOL_SIZES):
        s = s + jnp.where(row >= k, pltpu.roll(s, k, axis=0), 0.0)
        k *= 2
        sums[k] = s
    win_sum = sums[POOL_SIZES[-1]]
    for g in range(len(POOL_SIZES) - 2, -1, -1):
        win_sum = jnp.where(lane_group == g, sums[POOL_SIZES[g]], win_sum)
    pooled = win_sum * inv_cnt_ref[...] - z
    o_ref[...] = (_dot(pooled.astype(BF16), w_ref[...]) * scale_ref[...]).astype(o_ref.dtype)


def _pool(pz, w_p, scale, b, t):
    w_bd = jax.scipy.linalg.block_diag(*[w_p[g] for g in range(N_HEADS)]).astype(BF16)
    win = jnp.repeat(jnp.array(POOL_SIZES, jnp.int32), HEAD_DIM)
    inv_cnt = 1.0 / jnp.minimum(jnp.arange(t)[:, None] + 1, win[None, :]).astype(F32)
    return _Part(
        _pool_kernel,
        [pz, inv_cnt, w_bd, scale.reshape(1, W_MIX)],
        [pl.BlockSpec((t, W_MIX), lambda i: (i, 0)), _const_spec((t, W_MIX)), _const_spec((W_MIX, W_MIX)),
         _const_spec((1, W_MIX))],
        [pl.BlockSpec((t, W_MIX), lambda i: (i, 0))],
        [jax.ShapeDtypeStruct((b * t, W_MIX), BF16)])


def _nsa_prep_kernel(kv_ref, kc_ref, vc_ref, posk_ref, posv_ref, kw1_ref, vw1_ref, kw2_ref, vw2_ref,
                     gkv_ref, nmask_ref, gmean_ref, c_ref, su_ref, sd_ref, gc_ref, cc_ref, csu_ref, csd_ref,
                     ks_ref, vs_ref, kw_ref, vw_ref, kcmp_ref, vcmp_ref):
    t = kv_ref.shape[0]
    x = kv_ref[...]
    normed = _group_rms(x, gmean_ref[...]) * gkv_ref[...]
    y = _rope(jnp.where(nmask_ref[...] > 0.5, normed, x), c_ref[...], su_ref[...], sd_ref[...])
    lane = lax.broadcasted_iota(jnp.int32, (t, LANES), 1)
    block = lax.broadcasted_iota(jnp.int32, (t, LANES), 0) // SEL_LEN
    key = lane < HEAD_DIM
    ks_v, kw_v = y[:, 0:LANES], y[:, LANES:2 * LANES]
    ks_ref[...] = jnp.where(key, ks_v, jnp.where(lane - HEAD_DIM == block, 1.0, 0.0)).astype(BF16)
    vs_ref[...] = jnp.where(key, pltpu.roll(ks_v, HEAD_DIM, axis=1), _ones_lane((t, LANES))).astype(BF16)
    kw_ref[...] = jnp.where(key, kw_v, 0.0).astype(BF16)
    vw_ref[...] = jnp.where(key, pltpu.roll(kw_v, HEAD_DIM, axis=1), _ones_lane((t, LANES))).astype(BF16)

    half = kc_ref.shape[1]
    nb = kc_ref.shape[0]

    def hidden(x_ref, pos_ref, w1_ref):
        top = _dot((x_ref[...] + pos_ref[:, 0:half]).astype(BF16), w1_ref[0:half, :])
        bot = _dot((x_ref[...] + pos_ref[:, half:2 * half]).astype(BF16), w1_ref[half:2 * half, :])
        return _gelu(top + pltpu.roll(bot, nb - 1, axis=0)).astype(BF16)

    kv_cmp = _dot(hidden(kc_ref, posk_ref, kw1_ref), kw2_ref[...]) + _dot(hidden(vc_ref, posv_ref, vw1_ref), vw2_ref[...])
    key = lax.broadcasted_iota(jnp.int32, kv_cmp.shape, 1) < HEAD_DIM
    normed = _group_rms(kv_cmp, gmean_ref[0:LANES, 0:LANES]) * gc_ref[...]
    y = _rope(jnp.where(key, normed, kv_cmp), cc_ref[...], csu_ref[...], csd_ref[...])
    kcmp_ref[...] = jnp.where(key, y, 0.0).astype(BF16)
    vcmp_ref[...] = jnp.where(key, pltpu.roll(y, HEAD_DIM, axis=1), _ones_lane(y.shape)).T.astype(BF16)


def _rope_tables(pos):
    inv = ROPE_THETA ** (-jnp.arange(ROPE_HALF, dtype=F32) * 2.0 / ROPE_DIM)
    ang = pos.astype(F32)[:, None] * inv[None, :]
    cos, sin = jnp.cos(ang), jnp.sin(ang)
    n = pos.shape[0]
    zero = jnp.zeros((n, ROPE_HALF), F32)
    rest0 = jnp.zeros((n, HEAD_DIM - ROPE_DIM), F32)
    c = jnp.concatenate([cos, cos, rest0 + 1.0], axis=1)
    s_up = jnp.concatenate([-sin, zero, rest0], axis=1)
    s_dn = jnp.concatenate([zero, sin, rest0], axis=1)
    return c, s_up, s_dn


def _identity_tables(n):
    return jnp.ones((n, HEAD_DIM), F32), jnp.zeros((n, HEAD_DIM), F32), jnp.zeros((n, HEAD_DIM), F32)


def _nsa_prep(nkv, kc2, vc2, g_kc, g_ks, g_kw, pos_k, k_w1, k_w2, pos_v, v_w1, v_w2, b, t):
    nb = t // CMP_STRIDE
    assert HEAD_DIM + t // SEL_LEN <= LANES
    flat = CMP_STRIDE * HEAD_DIM
    rope_t = _rope_tables(jnp.arange(t))
    iden_t = _identity_tables(t)
    tabs = [jnp.concatenate([r, i, r, i], axis=1) for r, i in zip(rope_t, iden_t)]
    rope_c = _rope_tables(jnp.arange(nb) * CMP_STRIDE + CMP_LEN - 1)
    iden_c = _identity_tables(nb)
    tabs_c = [jnp.concatenate([r, i], axis=1) for r, i in zip(rope_c, iden_c)]
    one = jnp.ones((HEAD_DIM,), F32)
    gkv = jnp.concatenate([g_ks, one, g_kw, one]).reshape(1, W_MIX)
    nmask = jnp.concatenate([one, 0 * one, one, 0 * one]).reshape(1, W_MIX)
    gc = jnp.concatenate([g_kc, one]).reshape(1, LANES)
    zpad = jnp.zeros_like(k_w2)
    kw2 = jnp.concatenate([k_w2, zpad], axis=1).astype(BF16)
    vw2 = jnp.concatenate([zpad, v_w2], axis=1).astype(BF16)
    const = _const_spec
    seq = lambda width: pl.BlockSpec((t, width), lambda i: (i, 0))
    cmp_in = pl.BlockSpec((nb, flat), lambda i: (i, 0))
    cmp_out = pl.BlockSpec((nb, LANES), lambda i: (i, 0))
    return _Part(
        _nsa_prep_kernel,
        [nkv, kc2, vc2, pos_k.reshape(1, 2 * flat), pos_v.reshape(1, 2 * flat),
         k_w1.astype(BF16), v_w1.astype(BF16), kw2, vw2, gkv, nmask, _group_mean_matrix(W_MIX), *tabs, gc, *tabs_c],
        [seq(W_MIX), cmp_in, cmp_in, const((1, 2 * flat)), const((1, 2 * flat)),
         const(k_w1.shape), const(v_w1.shape), const(kw2.shape), const(vw2.shape),
         const((1, W_MIX)), const((1, W_MIX)), const((W_MIX, W_MIX)),
         const((t, W_MIX)), const((t, W_MIX)), const((t, W_MIX)),
         const((1, LANES)), const((nb, LANES)), const((nb, LANES)), const((nb, LANES))],
        [seq(LANES)] * 4 + [cmp_out, pl.BlockSpec((LANES, nb), lambda i: (i, 0))],
        [jax.ShapeDtypeStruct((b * t, LANES), BF16)] * 4
        + [jax.ShapeDtypeStruct((b * nb, LANES), BF16), jax.ShapeDtypeStruct((b * LANES, nb), BF16)])


def _nsa_kernel(q_ref, misc_ref, gb_ref, gq_ref, gmean_ref, c_ref, su_ref, sd_ref, ovt_ref,
                kcmp_ref, vcmp_ref, ks_ref, vs_ref, kw_ref, vw_ref, o_ref, *, n_top):
    tq = q_ref.shape[0]
    qi = pl.program_id(1)
    t0 = qi * tq
    nb = kcmp_ref.shape[0]
    nsel = ks_ref.shape[0] // SEL_LEN
    rows = N_HEADS * tq

    qn = _rope(_group_rms(q_ref[...], gmean_ref[...]) * gq_ref[...], c_ref[...], su_ref[...], sd_ref[...]) * (ATTN_SCALE * LOG2E)
    q_cols = [_head_column(qn, h) for h in range(N_HEADS)]
    lane_q = lax.broadcasted_iota(jnp.int32, (tq, LANES), 1)

    q0 = jnp.concatenate([jnp.where(lane_q < HEAD_DIM, col, 0.0) for col in q_cols], axis=0).astype(BF16)
    cmp_end = lax.broadcasted_iota(jnp.int32, (nb, rows), 0) * CMP_STRIDE + (CMP_LEN - 1)
    qpos_c = (lax.broadcasted_iota(jnp.int32, (nb, rows), 1) & (tq - 1)) + t0
    m_cmp = cmp_end <= qpos_c
    s = jnp.where(m_cmp, _dot_nt(kcmp_ref[...], q0), NEG_INF)
    e = jnp.where(m_cmp, jnp.exp2(s - jnp.max(s, axis=0, keepdims=True)), 0.0)
    denom = jnp.sum(e, axis=0, keepdims=True)
    inv = 1.0 / jnp.where(denom > 0.0, denom, 1.0)
    p_cmp = e * inv
    o_cmp = (_dot(vcmp_ref[...], e.astype(BF16)) * inv).T

    p_sum = p_cmp[:, 0:tq] + p_cmp[:, tq:2 * tq] + p_cmp[:, 2 * tq:3 * tq] + p_cmp[:, 3 * tq:4 * tq]
    nsp = -(-nsel // 8) * 8
    imp = _dot(ovt_ref[0:nsp, :], p_sum, precision=HIGHEST)
    blk = lax.broadcasted_iota(jnp.int32, (nsp, tq), 0)
    cur = (lax.broadcasted_iota(jnp.int32, (nsp, tq), 1) + t0) // SEL_LEN
    valid = blk <= cur
    forced = ((blk == 0) | (blk == cur) | (blk == cur - 1)).astype(F32)
    imp = jnp.where(valid, imp + SEL_FORCE * forced, NEG_INF)
    cur_row = (lax.broadcasted_iota(jnp.int32, (1, tq), 1) + t0) // SEL_LEN
    outranked = jnp.zeros((nsp, tq), F32)
    for j in range(nsel):
        rival = imp[j:j + 1, :]
        wins = (rival > imp) | ((rival == imp) & (blk > j))
        outranked = outranked + jnp.where(wins & (cur_row >= j), 1.0, 0.0)
    keep = (outranked < n_top) & valid
    bias_t = jnp.where(keep | (blk >= nsel), 0.0, NEG_INF)
    if nsp < LANES:
        bias_t = jnp.concatenate([bias_t, jnp.zeros((LANES - nsp, tq), F32)], axis=0)
    bias = jnp.concatenate([bias_t[:, c * LANES:(c + 1) * LANES].T for c in range(tq // LANES)], axis=0)
    bias = pltpu.roll(bias, HEAD_DIM, axis=1)
    q = jnp.concatenate([jnp.where(lane_q < HEAD_DIM, col, bias) for col in q_cols], axis=0).astype(BF16)

    trow = lax.broadcasted_iota(jnp.int32, (rows, tq), 0) & (tq - 1)
    tcol = lax.broadcasted_iota(jnp.int32, (rows, tq), 1)
    causal_bias = jnp.where(tcol <= trow, 0.0, NEG_INF)

    def tile(j):
        return pl.ds(pl.multiple_of(j * tq, tq), tq)

    def sel_step(kj, extra_bias, st):
        m, acc = st
        s = _dot_nt(q, ks_ref[tile(kj), :])
        if extra_bias is not None:
            s = s + extra_bias
        m_new = jnp.maximum(m, jnp.max(s, axis=-1, keepdims=True))
        acc = jnp.exp2(m - m_new) * acc + _dot(jnp.exp2(s - m_new).astype(BF16), vs_ref[tile(kj), :])
        return m_new, acc

    st = (jnp.full((rows, 1), NEG_INF, F32), jnp.zeros((rows, LANES), F32))
    st = lax.fori_loop(0, qi, lambda kj, st: sel_step(kj, None, st), st)
    _, acc = sel_step(qi, causal_bias, st)
    o_sel = acc / acc[:, ONE_LANE:ONE_LANE + 1]

    wt = min(NSA_WINDOW_TILE, tq)
    span = WINDOW // wt
    wrow = lax.broadcasted_iota(jnp.int32, (N_HEADS * wt, wt), 0) & (wt - 1)
    wcol = lax.broadcasted_iota(jnp.int32, (N_HEADS * wt, wt), 1)
    o_win = []
    for sub in range(tq // wt):
        q_sub = jnp.concatenate([q[h * tq + sub * wt:h * tq + (sub + 1) * wt] for h in range(N_HEADS)], axis=0)
        diag = qi * (tq // wt) + sub
        parts = []
        for d in range(span, -1, -1):
            ksl = pl.ds(pl.multiple_of(jnp.maximum(diag - d, 0) * wt, wt), wt)
            if d == 0:
                bias_w = jnp.where(wcol <= wrow, 0.0, NEG_INF)
            elif d == span:
                bias_w = jnp.where((wcol > wrow) & (diag >= d), 0.0, NEG_INF)
            else:
                bias_w = jnp.where(diag >= d, 0.0, NEG_INF)
            parts.append((_dot_nt(q_sub, kw_ref[ksl, :]) + bias_w, ksl))
        m = functools.reduce(jnp.maximum, [jnp.max(s, axis=-1, keepdims=True) for s, _ in parts])
        acc = sum(_dot(jnp.exp2(s - m).astype(BF16), vw_ref[ksl, :]) for s, ksl in parts)
        o_win.append(acc / acc[:, ONE_LANE:ONE_LANE + 1])

    gate = jax.nn.sigmoid(misc_ref[...] + gb_ref[...])
    outs = []
    for h in range(N_HEADS):
        hr = slice(h * tq, (h + 1) * tq)
        g0 = _MISC_GATE + 3 * h
        o_win_h = jnp.concatenate([part[h * wt:(h + 1) * wt] for part in o_win], axis=0)
        outs.append(gate[:, g0:g0 + 1] * o_cmp[hr] + gate[:, g0 + 1:g0 + 2] * o_sel[hr] + gate[:, g0 + 2:g0 + 3] * o_win_h)
    for hp in range(N_HEADS // 2):
        o_ref[:, hp * LANES:(hp + 1) * LANES] = _pair_heads(outs[2 * hp], outs[2 * hp + 1]).astype(o_ref.dtype)


def _cmp_to_sel_overlap_t(nb, nsel):
    nc = nb - 1
    cs = np.arange(nc) * CMP_STRIDE
    ss = np.arange(nsel) * SEL_LEN
    ov = np.clip(np.minimum(cs[:, None] + CMP_LEN, ss[None, :] + SEL_LEN) - np.maximum(cs[:, None], ss[None, :]), 0, None)
    out = np.zeros((LANES, nb), np.float32)
    out[:nsel, :nc] = (ov / CMP_LEN).T
    return jnp.asarray(out)


def _nsa(nq, misc, gate_b, g_q, ks, vs, kw, vw, kcmp, vcmp, b, t):
    tq = min(NSA_TILE, t)
    nb = t // CMP_STRIDE
    nsel = t // SEL_LEN
    wt = min(NSA_WINDOW_TILE, tq)
    assert HEAD_DIM + nsel <= LANES and nb % 8 == 0 and tq & (tq - 1) == 0 and tq % LANES == 0
    assert WINDOW % wt == 0 and tq % wt == 0 and wt & (wt - 1) == 0
    gb = jnp.zeros((1, LANES), F32).at[0, _MISC_GATE:_MISC_GATE + 3 * N_HEADS].set(gate_b)
    gq = jnp.tile(g_q, N_HEADS).reshape(1, W_MIX)
    tabs = [jnp.tile(x, (1, N_HEADS)) for x in _rope_tables(jnp.arange(t))]
    nq_tiles = t // tq
    const = lambda shape: pl.BlockSpec(shape, lambda i, j: (0,) * len(shape))
    qtile = lambda width: pl.BlockSpec((tq, width), lambda i, j: (i * nq_tiles + j, 0))
    ptile = pl.BlockSpec((tq, W_MIX), lambda i, j: (j, 0))
    seq = pl.BlockSpec((t, LANES), lambda i, j: (i, 0))
    cmp = pl.BlockSpec((nb, LANES), lambda i, j: (i, 0))
    return pl.pallas_call(
        functools.partial(_nsa_kernel, n_top=min(SEL_TOP, nsel)),
        grid=(b, nq_tiles),
        in_specs=[qtile(W_MIX), qtile(LANES), const((1, LANES)), const((1, W_MIX)), const((W_MIX, W_MIX)),
                  ptile, ptile, ptile, const((LANES, nb)), cmp, pl.BlockSpec((LANES, nb), lambda i, j: (i, 0)),
                  seq, seq, seq, seq],
        out_specs=qtile(W_MIX),
        out_shape=jax.ShapeDtypeStruct((b * t, W_MIX), BF16),
        compiler_params=_params("parallel", "arbitrary"),
        name="nsa",
    )(nq, misc, gb, gq, _group_mean_matrix(W_MIX), *tabs, _cmp_to_sel_overlap_t(nb, nsel),
      kcmp, vcmp, ks, vs, kw, vw)


def kernel(x, ffn1_norm, ffn1_w1, ffn1_w3, ffn1_w2, mix_norm, w_in, w_out, fox_f_bias, fox_q_norm, fox_k_norm, gmlp_v_norm, gmlp_w_s, gmlp_b_s, nsa_q_norm, nsa_kc_norm, nsa_ks_norm, nsa_kw_norm, nsa_cmp_pos_k, nsa_cmp_k_w1, nsa_cmp_k_w2, nsa_cmp_pos_v, nsa_cmp_v_w1, nsa_cmp_v_w2, nsa_gate_bias, pool_w, pool_scale, ffn2_norm, ffn2_w1, ffn2_w3, ffn2_w2):
    b, t, d = x.shape
    assert t % GMLP_CHUNK == 0 and t >= CMP_LEN
    n = b * t
    xf = x.reshape(n, d)
    for l in range(ffn1_norm.shape[0]):
        xf = _ffn(xf, ffn1_norm[l], ffn1_w1[l], ffn1_w3[l], ffn1_w2[l])
        fqkv, guv, nq, nkv, pz, kc, vc, misc = _inproj(xf, mix_norm[l], w_in[l])
        o_a = _fox(fqkv, misc, fox_f_bias[l], fox_q_norm[l], fox_k_norm[l], b, t)
        (o_b,), (o_d,), (ks, vs, kw, vw, kcmp, vcmp) = _run_parts([
            _gmlp(guv, gmlp_v_norm[l], gmlp_w_s[l], gmlp_b_s[l], b, t),
            _pool(pz, pool_w[l], pool_scale[l], b, t),
            _nsa_prep(nkv, kc, vc, nsa_kc_norm[l], nsa_ks_norm[l], nsa_kw_norm[l],
                      nsa_cmp_pos_k[l], nsa_cmp_k_w1[l], nsa_cmp_k_w2[l],
                      nsa_cmp_pos_v[l], nsa_cmp_v_w1[l], nsa_cmp_v_w2[l], b, t)], b, "seq_mixers")
        o_c = _nsa(nq, misc, nsa_gate_bias[l], nsa_q_norm[l], ks, vs, kw, vw, kcmp, vcmp, b, t)
        xf = _ffn(xf, ffn2_norm[l], ffn2_w1[l], ffn2_w3[l], ffn2_w2[l], mixers=(o_a, o_b, o_c, o_d), w_out=w_out[l])
    return xf.reshape(b, t, d)
```

```python
import functools
from typing import Callable, NamedTuple

import numpy as np
import jax
import jax.numpy as jnp
from jax import lax
from jax.experimental import pallas as pl
from jax.experimental.pallas import tpu as pltpu

F32 = jnp.float32
BF16 = jnp.bfloat16
HIGHEST = lax.Precision.HIGHEST

HEAD_DIM = 64
N_HEADS = 4
W_MIX = N_HEADS * HEAD_DIM
ROPE_THETA = 500000.0
ROPE_DIM = HEAD_DIM // 4
ROPE_HALF = ROPE_DIM // 2
GMLP_CHUNK = 128
CMP_LEN = 32
CMP_STRIDE = 16
SEL_LEN = 64
SEL_TOP = 16
WINDOW = 512
POOL_SIZES = (2, 4, 8, 16)
FFN_RES_WEIGHT = 0.5
EPS = 1e-6
NEG_INF = -1e30
SEL_FORCE = 1e3
ATTN_SCALE = HEAD_DIM ** -0.5
LOG2E = 1.4426950408889634

LANES = 128
VMEM_LIMIT = 52 * 1024 * 1024
FFN_TOKENS = 1024
FFN_MIX_TOKENS = 1024
FFN_HIDDEN = 256
FOX_PREP_ROWS = 512
FOX_TILE = 1024
NSA_TILE = 512
NSA_WINDOW_TILE = 512
ONE_LANE = HEAD_DIM


def _params(*sem):
    return pltpu.CompilerParams(dimension_semantics=sem, vmem_limit_bytes=VMEM_LIMIT)


def _dot(a, b, **kw):
    return jnp.dot(a, b, preferred_element_type=F32, **kw)


def _dot_nt(a, b):
    return lax.dot_general(a, b, (((1,), (1,)), ((), ())), preferred_element_type=F32)


def _rms(x):
    return x * lax.rsqrt(jnp.mean(x * x, axis=-1, keepdims=True) + EPS)


def _group_rms(x, gmean):
    ms = _dot((x * x).astype(BF16), gmean)
    return x * lax.rsqrt(ms + EPS)


def _rope(x, c, s_up, s_dn):
    w = x.shape[-1]
    return x * c + pltpu.roll(x, w - ROPE_HALF, axis=1) * s_up + pltpu.roll(x, ROPE_HALF, axis=1) * s_dn


_GELU_C = 0.7978845608028654


def _gelu(x):
    half = 0.5 * x
    return half + half * jnp.tanh(x * (_GELU_C + (_GELU_C * 0.044715) * (x * x)))


def _log_sigmoid(x):
    return jnp.minimum(x, 0.0) - jnp.log(1.0 + jnp.exp(-jnp.abs(x)))


def _head_column(x, h):
    col = x[:, (h // 2) * LANES:(h // 2 + 1) * LANES]
    return pltpu.roll(col, HEAD_DIM, axis=1) if h % 2 else col


def _pair_heads(even, odd):
    lane = lax.broadcasted_iota(jnp.int32, even.shape, 1)
    return jnp.where(lane < HEAD_DIM, even, pltpu.roll(odd, HEAD_DIM, axis=1))


def _ones_lane(shape):
    return jnp.where(lax.broadcasted_iota(jnp.int32, shape, 1) == ONE_LANE, 1.0, 0.0)


def _ffn_kernel(x_ref, g_ref, w1_ref, w3_ref, w2_ref, *rest):
    *mix_refs, o_ref = rest
    x = x_ref[...]
    if mix_refs:
        *mixers, wout_ref = mix_refs
        for k, ref in enumerate(mixers):
            x = x + _dot(ref[...], wout_ref[k * W_MIX:(k + 1) * W_MIX, :])
    h = (_rms(x) * g_ref[...]).astype(BF16)
    acc = jnp.zeros_like(x)
    for c in range(w1_ref.shape[1] // FFN_HIDDEN):
        cs = slice(c * FFN_HIDDEN, (c + 1) * FFN_HIDDEN)
        a = _dot(h, w1_ref[:, cs])
        b = _dot(h, w3_ref[:, cs])
        acc = acc + _dot((a * jax.nn.sigmoid(a) * b).astype(BF16), w2_ref[cs, :])
    o_ref[...] = x + FFN_RES_WEIGHT * acc


def _ffn(x, g, w1, w3, w2, mixers=(), w_out=None):
    n, d = x.shape
    tm = min(FFN_MIX_TOKENS if mixers else FFN_TOKENS, n)
    resident = lambda shape: pl.BlockSpec(shape, lambda i: (0, 0), pipeline_mode=pl.Buffered(1))
    mix_specs = [pl.BlockSpec((tm, W_MIX), lambda i: (i, 0)) for _ in mixers] + ([resident(w_out.shape)] if mixers else [])
    mix_args = list(mixers) + ([w_out.astype(BF16)] if mixers else [])
    return pl.pallas_call(
        _ffn_kernel,
        grid=(n // tm,),
        in_specs=[pl.BlockSpec((tm, d), lambda i: (i, 0)), resident((1, d)),
                  resident(w1.shape), resident(w3.shape), resident(w2.shape)] + mix_specs,
        out_specs=pl.BlockSpec((tm, d), lambda i: (i, 0)),
        out_shape=jax.ShapeDtypeStruct((n, d), F32),
        compiler_params=_params("parallel"),
        name="ffn_mix" if mixers else "ffn",
    )(x, g.reshape(1, d), w1.astype(BF16), w3.astype(BF16), w2.astype(BF16), *mix_args)


_IN_GROUPS = (768, 512, 256, 256, 256, 128, 128)
_MISC_FORGET = 0
_MISC_GATE = 4


def _inproj_kernel(x_ref, g_ref, w_ref, fqkv_ref, guv_ref, nq_ref, nkv_ref, pz_ref, kc_ref, vc_ref, misc_ref, kcvc_s):
    h = (_rms(x_ref[...]) * g_ref[...]).astype(BF16)
    off = 0
    for ref, width in zip((fqkv_ref, guv_ref, nq_ref, nkv_ref, pz_ref), _IN_GROUPS[:5]):
        ref[...] = _dot(h, w_ref[:, off:off + width])
        off += width
    tail = _dot(h, w_ref[:, off:off + 2 * LANES])
    misc_ref[...] = tail[:, LANES:]
    kcvc_s[...] = tail[:, :LANES]
    groups = x_ref.shape[0] // CMP_STRIDE
    lane = lax.broadcasted_iota(jnp.int32, (groups, LANES), 1)
    for j in range(0, CMP_STRIDE, 2):
        even = kcvc_s[pl.ds(j, groups, stride=CMP_STRIDE), :]
        odd = kcvc_s[pl.ds(j + 1, groups, stride=CMP_STRIDE), :]
        col = slice((j // 2) * LANES, (j // 2 + 1) * LANES)
        kc_ref[:, col] = jnp.where(lane < HEAD_DIM, even, pltpu.roll(odd, HEAD_DIM, axis=1))
        vc_ref[:, col] = jnp.where(lane < HEAD_DIM, pltpu.roll(even, HEAD_DIM, axis=1), odd)


def _relayout_w_in(w_in):
    o = np.cumsum((0, 256, 256, 256, 4, 256, 256, 256, 64, 64, 64, 64, 64, 64, 12, 256))
    fq, ff, gu, nq, nkc, nks, ng, pz, end = o[0], o[3], o[4], o[6], o[7], o[9], o[13], o[14], o[15]
    d = w_in.shape[0]
    pad = jnp.zeros((d, LANES - 16), w_in.dtype)
    return jnp.concatenate([
        w_in[:, fq:ff], w_in[:, gu:nq], w_in[:, nq:nkc], w_in[:, nks:ng], w_in[:, pz:end],
        w_in[:, nkc:nks], w_in[:, ff:gu], w_in[:, ng:pz], pad], axis=1)


def _inproj(x, g, w_in):
    n, d = x.shape
    tm = min(FFN_TOKENS, n)
    assert tm % (8 * CMP_STRIDE) == 0
    w = _relayout_w_in(w_in).astype(BF16)
    flat = CMP_STRIDE * HEAD_DIM
    outs = [(tm, n, wd) for wd in (768, 512, 256, 256, 256)] + [(tm // CMP_STRIDE, n // CMP_STRIDE, flat)] * 2 + [(tm, n, LANES)]
    return pl.pallas_call(
        _inproj_kernel,
        grid=(n // tm,),
        in_specs=[
            pl.BlockSpec((tm, d), lambda i: (i, 0)),
            pl.BlockSpec((1, d), lambda i: (0, 0)),
            pl.BlockSpec(w.shape, lambda i: (0, 0)),
        ],
        out_specs=[pl.BlockSpec((rows, wd), lambda i: (i, 0)) for rows, _, wd in outs],
        out_shape=[jax.ShapeDtypeStruct((total, wd), F32) for _, total, wd in outs],
        scratch_shapes=[pltpu.VMEM((tm, LANES), F32)],
        compiler_params=_params("parallel"),
        name="inproj",
    )(x, g.reshape(1, d), w)


def _bf16_pieces(x):
    p1 = x.astype(BF16).astype(F32)
    r1 = x - p1
    p2 = r1.astype(BF16).astype(F32)
    p3 = (r1 - p2).astype(BF16).astype(F32)
    return p1, p2, p3


_N_PIECES = 3
_N_EXTRA = _N_PIECES * N_HEADS
_PACK_ONE = _N_EXTRA


def _fox_placement():
    pq = np.zeros((LANES, LANES), np.float32)
    pk = np.zeros((LANES, LANES), np.float32)
    for h in range(N_HEADS):
        for p in range(_N_PIECES):
            pq[p * N_HEADS + h, HEAD_DIM + _N_PIECES * h + p] = 1.0
            pk[p * N_HEADS + h, HEAD_DIM + _N_EXTRA + _N_PIECES * h + p] = -1.0
    pq[_PACK_ONE, HEAD_DIM + _N_EXTRA:HEAD_DIM + 2 * _N_EXTRA] = 1.0
    pk[_PACK_ONE, HEAD_DIM:HEAD_DIM + _N_EXTRA] = 1.0
    return jnp.asarray(np.concatenate([pq, pk], axis=1), BF16)


def _fox_kernel(qkv_ref, misc_ref, fbt_ref, gq_ref, gk_ref, gmean_ref, pqk_ref, o_ref, q_s, k_s, v_s, o_s):
    t = qkv_ref.shape[0]
    pr = min(FOX_PREP_ROWS, t)
    gmean = gmean_ref[...]
    upper = (lax.broadcasted_iota(jnp.int32, (pr, pr), 0) <= lax.broadcasted_iota(jnp.int32, (pr, pr), 1)).astype(BF16)
    row8 = lax.broadcasted_iota(jnp.int32, (8, pr), 0)
    lane = lax.broadcasted_iota(jnp.int32, (pr, LANES), 1)
    extra = lane - HEAD_DIM
    own = [((extra >= _N_PIECES * h) & (extra < _N_PIECES * (h + 1)))
           | ((extra >= _N_EXTRA + _N_PIECES * h) & (extra < _N_EXTRA + _N_PIECES * (h + 1))) for h in range(N_HEADS)]
    ones_v = jnp.where(extra == 0, 1.0, 0.0)
    carry = jnp.zeros((8, pr), F32)
    for r in range(t // pr):
        sl = slice(r * pr, (r + 1) * pr)
        logits_t = misc_ref[sl, :].T[0:8, :]
        log_f = _log_sigmoid(logits_t + fbt_ref[...]) * LOG2E
        c = sum(_dot(piece.astype(BF16), upper) for piece in _bf16_pieces(log_f)) + carry
        carry = jnp.broadcast_to(c[:, pr - 1:pr], (8, pr))
        p1, p2, p3 = _bf16_pieces(c)
        lo = jnp.where(row8 < N_HEADS, p1, pltpu.roll(p2, N_HEADS, axis=0))
        hi = jnp.where(row8 < N_HEADS, p3, jnp.where(row8 == _PACK_ONE - 8, 1.0, 0.0))
        packed = jnp.concatenate([lo, hi, jnp.zeros((LANES - 16, pr), F32)], axis=0).T.astype(BF16)
        q_extra = _dot(packed, pqk_ref[:, :LANES])
        k_extra = _dot(packed, pqk_ref[:, LANES:])
        qn = _group_rms(qkv_ref[sl, 0:W_MIX], gmean) * gq_ref[...] * (ATTN_SCALE * LOG2E)
        kn = _group_rms(qkv_ref[sl, W_MIX:2 * W_MIX], gmean) * gk_ref[...]
        v = qkv_ref[sl, 2 * W_MIX:3 * W_MIX]
        for h in range(N_HEADS):
            q_s[h, sl, :] = jnp.where(extra < 0, _head_column(qn, h), q_extra).astype(BF16)
            k_s[h, sl, :] = jnp.where(extra < 0, _head_column(kn, h), jnp.where(own[h], k_extra, 0.0)).astype(BF16)
            v_aug = jnp.where(extra < 0, _head_column(v, h), ones_v)
            v_s[h, :, sl] = v_aug.T.astype(BF16)

    tq = min(FOX_TILE, t)
    causal = lax.broadcasted_iota(jnp.int32, (tq, tq), 0) <= lax.broadcasted_iota(jnp.int32, (tq, tq), 1)

    def one_head(h):
        for qi in range(t // tq):
            dsl = slice(qi * tq, (qi + 1) * tq)
            q = q_s[h, dsl, :]
            s_d = jnp.where(causal, _dot_nt(k_s[h, dsl, :], q), NEG_INF)
            m = jnp.max(s_d, axis=0, keepdims=True)
            if qi:
                s_o = _dot_nt(k_s[h, 0:qi * tq, :], q)
                m = jnp.maximum(m, jnp.max(s_o, axis=0, keepdims=True))
            acc = _dot(v_s[h, :, dsl], jnp.exp2(s_d - m).astype(BF16))
            if qi:
                acc = acc + _dot(v_s[h, :, 0:qi * tq], jnp.exp2(s_o - m).astype(BF16))
            o_s[h, dsl, :] = (acc / acc[ONE_LANE:ONE_LANE + 1, :]).T

    def head_pair(hp, _):
        one_head(2 * hp)
        one_head(2 * hp + 1)
        return 0

    lax.fori_loop(0, N_HEADS // 2, head_pair, 0)
    for hp in range(N_HEADS // 2):
        o_ref[:, hp * LANES:(hp + 1) * LANES] = _pair_heads(o_s[2 * hp], o_s[2 * hp + 1]).astype(o_ref.dtype)


def _group_mean_matrix(width):
    g = np.kron(np.eye(width // HEAD_DIM, dtype=np.float32), np.full((HEAD_DIM, HEAD_DIM), 1.0 / HEAD_DIM, np.float32))
    return jnp.asarray(g, BF16)


def _fox(fqkv, misc, f_bias, g_q, g_k, b, t):
    assert t % min(FOX_TILE, t) == 0 and t % min(FOX_PREP_ROWS, t) == 0
    pr = min(FOX_PREP_ROWS, t)
    assert _MISC_FORGET == 0 and N_HEADS <= 4 and pr % LANES == 0
    fbt = jnp.broadcast_to(jnp.zeros((8,), F32).at[:N_HEADS].set(f_bias)[:, None], (8, pr))
    gq = jnp.tile(g_q, N_HEADS).reshape(1, W_MIX)
    gk = jnp.tile(g_k, N_HEADS).reshape(1, W_MIX)
    const = lambda shape: pl.BlockSpec(shape, lambda i: (0,) * len(shape))
    head_scratch = lambda dtype: pltpu.VMEM((N_HEADS, t, LANES), dtype)
    return pl.pallas_call(
        _fox_kernel,
        grid=(b,),
        in_specs=[pl.BlockSpec((t, 3 * W_MIX), lambda i: (i, 0)), pl.BlockSpec((t, LANES), lambda i: (i, 0)),
                  const((8, pr)), const((1, W_MIX)), const((1, W_MIX)), const((W_MIX, W_MIX)),
                  const((LANES, 2 * LANES))],
        out_specs=pl.BlockSpec((t, W_MIX), lambda i: (i, 0)),
        out_shape=jax.ShapeDtypeStruct((b * t, W_MIX), BF16),
        scratch_shapes=[head_scratch(BF16), head_scratch(BF16), pltpu.VMEM((N_HEADS, LANES, t), BF16), head_scratch(F32)],
        compiler_params=_params("parallel"),
        name="fox",
    )(fqkv, misc, fbt, gq, gk, _group_mean_matrix(W_MIX), _fox_placement())


def _gmlp_kernel(uv_ref, gv_ref, gmean_ref, w_ref, bias_ref, o_ref):
    c = GMLP_CHUNK
    rows = lax.broadcasted_iota(jnp.int32, (c, c), 0)
    cols = lax.broadcasted_iota(jnp.int32, (c, c), 1)
    lane_group = lax.broadcasted_iota(jnp.int32, (c, W_MIX), 1) // HEAD_DIM
    w_cat = jnp.concatenate([jnp.where(cols <= rows, w_ref[g], 0.0).astype(BF16) for g in range(N_HEADS)], axis=1)
    v_all = (_group_rms(_gelu(uv_ref[:, W_MIX:2 * W_MIX]), gmean_ref[...]) * gv_ref[...]).astype(BF16)
    for r in range(uv_ref.shape[0] // c):
        sl = slice(r * c, (r + 1) * c)
        u = _gelu(uv_ref[sl, 0:W_MIX])
        v = v_all[sl]
        v_stack = jnp.concatenate([jnp.where(lane_group == g, v, jnp.zeros_like(v)) for g in range(N_HEADS)], axis=0)
        o_ref[sl, :] = (u * (bias_ref[...] + _dot(w_cat, v_stack))).astype(o_ref.dtype)


class _Part(NamedTuple):
    body: Callable
    args: list
    in_specs: list
    out_specs: list
    out_shape: list


def _const_spec(shape):
    return pl.BlockSpec(shape, lambda i: (0,) * len(shape))


def _run_parts(parts, b, name):
    n_in = [len(p.in_specs) for p in parts]
    n_out = [len(p.out_specs) for p in parts]

    def body(*refs):
        ins, outs = refs[:sum(n_in)], refs[sum(n_in):]
        for p, i0, o0 in zip(parts, np.cumsum([0] + n_in), np.cumsum([0] + n_out)):
            p.body(*ins[i0:i0 + len(p.in_specs)], *outs[o0:o0 + len(p.out_specs)])

    outs = pl.pallas_call(
        body,
        grid=(b,),
        in_specs=[s for p in parts for s in p.in_specs],
        out_specs=[s for p in parts for s in p.out_specs],
        out_shape=[s for p in parts for s in p.out_shape],
        compiler_params=_params("parallel"),
        name=name,
    )(*[a for p in parts for a in p.args])
    return [outs[o0:o0 + k] for o0, k in zip(np.cumsum([0] + n_out), n_out)]


def _gmlp(guv, g_v, w_s, b_s, b, t):
    bias = jnp.repeat(b_s.T, HEAD_DIM, axis=1)
    return _Part(
        _gmlp_kernel,
        [guv, g_v.reshape(1, W_MIX), _group_mean_matrix(W_MIX), w_s, bias],
        [pl.BlockSpec((t, 2 * W_MIX), lambda i: (i, 0)), _const_spec((1, W_MIX)), _const_spec((W_MIX, W_MIX)),
         _const_spec(w_s.shape), _const_spec(bias.shape)],
        [pl.BlockSpec((t, W_MIX), lambda i: (i, 0))],
        [jax.ShapeDtypeStruct((b * t, W_MIX), BF16)])


def _pool_kernel(z_ref, inv_cnt_ref, w_ref, scale_ref, o_ref):
    t = z_ref.shape[0]
    z = z_ref[...]
    row = lax.broadcasted_iota(jnp.int32, (t, W_MIX), 0)
    lane_group = lax.broadcasted_iota(jnp.int32, (t, W_MIX), 1) // HEAD_DIM
    sums = {}
    s = z
    k = 1
    while k < max(POOL_SIZES):
        s = s + jnp.where(row >= k, pltpu.roll(s, k, axis=0), 0.0)
        k *= 2
        sums[k] = s
    win_sum = sums[POOL_SIZES[-1]]
    for g in range(len(POOL_SIZES) - 2, -1, -1):
        win_sum = jnp.where(lane_group == g, sums[POOL_SIZES[g]], win_sum)
    pooled = win_sum * inv_cnt_ref[...] - z
    o_ref[...] = (_dot(pooled.astype(BF16), w_ref[...]) * scale_ref[...]).astype(o_ref.dtype)


def _pool(pz, w_p, scale, b, t):
    w_bd = jax.scipy.linalg.block_diag(*[w_p[g] for g in range(N_HEADS)]).astype(BF16)
    win = jnp.repeat(jnp.array(POOL_SIZES, jnp.int32), HEAD_DIM)
    inv_cnt = 1.0 / jnp.minimum(jnp.arange(t)[:, None] + 1, win[None, :]).astype(F32)
    return _Part(
        _pool_kernel,
        [pz, inv_cnt, w_bd, scale.reshape(1, W_MIX)],
        [pl.BlockSpec((t, W_MIX), lambda i: (i, 0)), _const_spec((t, W_MIX)), _const_spec((W_MIX, W_MIX)),
         _const_spec((1, W_MIX))],
        [pl.BlockSpec((t, W_MIX), lambda i: (i, 0))],
        [jax.ShapeDtypeStruct((b * t, W_MIX), BF16)])


def _nsa_prep_kernel(kv_ref, kc_ref, vc_ref, posk_ref, posv_ref, kw1_ref, vw1_ref, kw2_ref, vw2_ref,
                     gkv_ref, nmask_ref, gmean_ref, c_ref, su_ref, sd_ref, gc_ref, cc_ref, csu_ref, csd_ref,
                     ks_ref, vs_ref, kw_ref, vw_ref, kcmp_ref, vcmp_ref):
    t = kv_ref.shape[0]
    x = kv_ref[...]
    normed = _group_rms(x, gmean_ref[...]) * gkv_ref[...]
    y = _rope(jnp.where(nmask_ref[...] > 0.5, normed, x), c_ref[...], su_ref[...], sd_ref[...])
    lane = lax.broadcasted_iota(jnp.int32, (t, LANES), 1)
    block = lax.broadcasted_iota(jnp.int32, (t, LANES), 0) // SEL_LEN
    key = lane < HEAD_DIM
    ks_v, kw_v = y[:, 0:LANES], y[:, LANES:2 * LANES]
    ks_ref[...] = jnp.where(key, ks_v, jnp.where(lane - HEAD_DIM == block, 1.0, 0.0)).astype(BF16)
    vs_ref[...] = jnp.where(key, pltpu.roll(ks_v, HEAD_DIM, axis=1), _ones_lane((t, LANES))).astype(BF16)
    kw_ref[...] = jnp.where(key, kw_v, 0.0).astype(BF16)
    vw_ref[...] = jnp.where(key, pltpu.roll(kw_v, HEAD_DIM, axis=1), _ones_lane((t, LANES))).astype(BF16)

    half = kc_ref.shape[1]
    nb = kc_ref.shape[0]

    def hidden(x_ref, pos_ref, w1_ref):
        top = _dot((x_ref[...] + pos_ref[:, 0:half]).astype(BF16), w1_ref[0:half, :])
        bot = _dot((x_ref[...] + pos_ref[:, half:2 * half]).astype(BF16), w1_ref[half:2 * half, :])
        return _gelu(top + pltpu.roll(bot, nb - 1, axis=0)).astype(BF16)

    kv_cmp = _dot(hidden(kc_ref, posk_ref, kw1_ref), kw2_ref[...]) + _dot(hidden(vc_ref, posv_ref, vw1_ref), vw2_ref[...])
    key = lax.broadcasted_iota(jnp.int32, kv_cmp.shape, 1) < HEAD_DIM
    normed = _group_rms(kv_cmp, gmean_ref[0:LANES, 0:LANES]) * gc_ref[...]
    y = _rope(jnp.where(key, normed, kv_cmp), cc_ref[...], csu_ref[...], csd_ref[...])
    kcmp_ref[...] = jnp.where(key, y, 0.0).astype(BF16)
    vcmp_ref[...] = jnp.where(key, pltpu.roll(y, HEAD_DIM, axis=1), _ones_lane(y.shape)).T.astype(BF16)


def _rope_tables(pos):
    inv = ROPE_THETA ** (-jnp.arange(ROPE_HALF, dtype=F32) * 2.0 / ROPE_DIM)
    ang = pos.astype(F32)[:, None] * inv[None, :]
    cos, sin = jnp.cos(ang), jnp.sin(ang)
    n = pos.shape[0]
    zero = jnp.zeros((n, ROPE_HALF), F32)
    rest0 = jnp.zeros((n, HEAD_DIM - ROPE_DIM), F32)
    c = jnp.concatenate([cos, cos, rest0 + 1.0], axis=1)
    s_up = jnp.concatenate([-sin, zero, rest0], axis=1)
    s_dn = jnp.concatenate([zero, sin, rest0], axis=1)
    return c, s_up, s_dn


def _identity_tables(n):
    return jnp.ones((n, HEAD_DIM), F32), jnp.zeros((n, HEAD_DIM), F32), jnp.zeros((n, HEAD_DIM), F32)


def _nsa_prep(nkv, kc2, vc2, g_kc, g_ks, g_kw, pos_k, k_w1, k_w2, pos_v, v_w1, v_w2, b, t):
    nb = t // CMP_STRIDE
    assert HEAD_DIM + t // SEL_LEN <= LANES
    flat = CMP_STRIDE * HEAD_DIM
    rope_t = _rope_tables(jnp.arange(t))
    iden_t = _identity_tables(t)
    tabs = [jnp.concatenate([r, i, r, i], axis=1) for r, i in zip(rope_t, iden_t)]
    rope_c = _rope_tables(jnp.arange(nb) * CMP_STRIDE + CMP_LEN - 1)
    iden_c = _identity_tables(nb)
    tabs_c = [jnp.concatenate([r, i], axis=1) for r, i in zip(rope_c, iden_c)]
    one = jnp.ones((HEAD_DIM,), F32)
    gkv = jnp.concatenate([g_ks, one, g_kw, one]).reshape(1, W_MIX)
    nmask = jnp.concatenate([one, 0 * one, one, 0 * one]).reshape(1, W_MIX)
    gc = jnp.concatenate([g_kc, one]).reshape(1, LANES)
    zpad = jnp.zeros_like(k_w2)
    kw2 = jnp.concatenate([k_w2, zpad], axis=1).astype(BF16)
    vw2 = jnp.concatenate([zpad, v_w2], axis=1).astype(BF16)
    const = _const_spec
    seq = lambda width: pl.BlockSpec((t, width), lambda i: (i, 0))
    cmp_in = pl.BlockSpec((nb, flat), lambda i: (i, 0))
    cmp_out = pl.BlockSpec((nb, LANES), lambda i: (i, 0))
    return _Part(
        _nsa_prep_kernel,
        [nkv, kc2, vc2, pos_k.reshape(1, 2 * flat), pos_v.reshape(1, 2 * flat),
         k_w1.astype(BF16), v_w1.astype(BF16), kw2, vw2, gkv, nmask, _group_mean_matrix(W_MIX), *tabs, gc, *tabs_c],
        [seq(W_MIX), cmp_in, cmp_in, const((1, 2 * flat)), const((1, 2 * flat)),
         const(k_w1.shape), const(v_w1.shape), const(kw2.shape), const(vw2.shape),
         const((1, W_MIX)), const((1, W_MIX)), const((W_MIX, W_MIX)),
         const((t, W_MIX)), const((t, W_MIX)), const((t, W_MIX)),
         const((1, LANES)), const((nb, LANES)), const((nb, LANES)), const((nb, LANES))],
        [seq(LANES)] * 4 + [cmp_out, pl.BlockSpec((LANES, nb), lambda i: (i, 0))],
        [jax.ShapeDtypeStruct((b * t, LANES), BF16)] * 4
        + [jax.ShapeDtypeStruct((b * nb, LANES), BF16), jax.ShapeDtypeStruct((b * LANES, nb), BF16)])


def _nsa_kernel(q_ref, misc_ref, gb_ref, gq_ref, gmean_ref, c_ref, su_ref, sd_ref, ovt_ref,
                kcmp_ref, vcmp_ref, ks_ref, vs_ref, kw_ref, vw_ref, o_ref, *, n_top):
    tq = q_ref.shape[0]
    qi = pl.program_id(1)
    t0 = qi * tq
    nb = kcmp_ref.shape[0]
    nsel = ks_ref.shape[0] // SEL_LEN
    rows = N_HEADS * tq

    qn = _rope(_group_rms(q_ref[...], gmean_ref[...]) * gq_ref[...], c_ref[...], su_ref[...], sd_ref[...]) * (ATTN_SCALE * LOG2E)
    q_cols = [_head_column(qn, h) for h in range(N_HEADS)]
    lane_q = lax.broadcasted_iota(jnp.int32, (tq, LANES), 1)

    q0 = jnp.concatenate([jnp.where(lane_q < HEAD_DIM, col, 0.0) for col in q_cols], axis=0).astype(BF16)
    cmp_end = lax.broadcasted_iota(jnp.int32, (nb, rows), 0) * CMP_STRIDE + (CMP_LEN - 1)
    qpos_c = (lax.broadcasted_iota(jnp.int32, (nb, rows), 1) & (tq - 1)) + t0
    m_cmp = cmp_end <= qpos_c
    s = jnp.where(m_cmp, _dot_nt(kcmp_ref[...], q0), NEG_INF)
    e = jnp.where(m_cmp, jnp.exp2(s - jnp.max(s, axis=0, keepdims=True)), 0.0)
    denom = jnp.sum(e, axis=0, keepdims=True)
    inv = 1.0 / jnp.where(denom > 0.0, denom, 1.0)
    p_cmp = e * inv
    o_cmp = (_dot(vcmp_ref[...], e.astype(BF16)) * inv).T

    p_sum = p_cmp[:, 0:tq] + p_cmp[:, tq:2 * tq] + p_cmp[:, 2 * tq:3 * tq] + p_cmp[:, 3 * tq:4 * tq]
    nsp = -(-nsel // 8) * 8
    imp = _dot(ovt_ref[0:nsp, :], p_sum, precision=HIGHEST)
    blk = lax.broadcasted_iota(jnp.int32, (nsp, tq), 0)
    cur = (lax.broadcasted_iota(jnp.int32, (nsp, tq), 1) + t0) // SEL_LEN
    valid = blk <= cur
    forced = ((blk == 0) | (blk == cur) | (blk == cur - 1)).astype(F32)
    imp = jnp.where(valid, imp + SEL_FORCE * forced, NEG_INF)
    cur_row = (lax.broadcasted_iota(jnp.int32, (1, tq), 1) + t0) // SEL_LEN
    outranked = jnp.zeros((nsp, tq), F32)
    for j in range(nsel):
        rival = imp[j:j + 1, :]
        wins = (rival > imp) | ((rival == imp) & (blk > j))
        outranked = outranked + jnp.where(wins & (cur_row >= j), 1.0, 0.0)
    keep = (outranked < n_top) & valid
    bias_t = jnp.where(keep | (blk >= nsel), 0.0, NEG_INF)
    if nsp < LANES:
        bias_t = jnp.concatenate([bias_t, jnp.zeros((LANES - nsp, tq), F32)], axis=0)
    bias = jnp.concatenate([bias_t[:, c * LANES:(c + 1) * LANES].T for c in range(tq // LANES)], axis=0)
    bias = pltpu.roll(bias, HEAD_DIM, axis=1)
    q = jnp.concatenate([jnp.where(lane_q < HEAD_DIM, col, bias) for col in q_cols], axis=0).astype(BF16)

    trow = lax.broadcasted_iota(jnp.int32, (rows, tq), 0) & (tq - 1)
    tcol = lax.broadcasted_iota(jnp.int32, (rows, tq), 1)
    causal_bias = jnp.where(tcol <= trow, 0.0, NEG_INF)

    def tile(j):
        return pl.ds(pl.multiple_of(j * tq, tq), tq)

    def sel_step(kj, extra_bias, st):
        m, acc = st
        s = _dot_nt(q, ks_ref[tile(kj), :])
        if extra_bias is not None:
            s = s + extra_bias
        m_new = jnp.maximum(m, jnp.max(s, axis=-1, keepdims=True))
        acc = jnp.exp2(m - m_new) * acc + _dot(jnp.exp2(s - m_new).astype(BF16), vs_ref[tile(kj), :])
        return m_new, acc

    st = (jnp.full((rows, 1), NEG_INF, F32), jnp.zeros((rows, LANES), F32))
    st = lax.fori_loop(0, qi, lambda kj, st: sel_step(kj, None, st), st)
    _, acc = sel_step(qi, causal_bias, st)
    o_sel = acc / acc[:, ONE_LANE:ONE_LANE + 1]

    wt = min(NSA_WINDOW_TILE, tq)
    span = WINDOW // wt
    wrow = lax.broadcasted_iota(jnp.int32, (N_HEADS * wt, wt), 0) & (wt - 1)
    wcol = lax.broadcasted_iota(jnp.int32, (N_HEADS * wt, wt), 1)
    o_win = []
    for sub in range(tq // wt):
        q_sub = jnp.concatenate([q[h * tq + sub * wt:h * tq + (sub + 1) * wt] for h in range(N_HEADS)], axis=0)
        diag = qi * (tq // wt) + sub
        parts = []
        for d in range(span, -1, -1):
            ksl = pl.ds(pl.multiple_of(jnp.maximum(diag - d, 0) * wt, wt), wt)
            if d == 0:
                bias_w = jnp.where(wcol <= wrow, 0.0, NEG_INF)
            elif d == span:
                bias_w = jnp.where((wcol > wrow) & (diag >= d), 0.0, NEG_INF)
            else:
                bias_w = jnp.where(diag >= d, 0.0, NEG_INF)
            parts.append((_dot_nt(q_sub, kw_ref[ksl, :]) + bias_w, ksl))
        m = functools.reduce(jnp.maximum, [jnp.max(s, axis=-1, keepdims=True) for s, _ in parts])
        acc = sum(_dot(jnp.exp2(s - m).astype(BF16), vw_ref[ksl, :]) for s, ksl in parts)
        o_win.append(acc / acc[:, ONE_LANE:ONE_LANE + 1])

    gate = jax.nn.sigmoid(misc_ref[...] + gb_ref[...])
    outs = []
    for h in range(N_HEADS):
        hr = slice(h * tq, (h + 1) * tq)
        g0 = _MISC_GATE + 3 * h
        o_win_h = jnp.concatenate([part[h * wt:(h + 1) * wt] for part in o_win], axis=0)
        outs.append(gate[:, g0:g0 + 1] * o_cmp[hr] + gate[:, g0 + 1:g0 + 2] * o_sel[hr] + gate[:, g0 + 2:g0 + 3] * o_win_h)
    for hp in range(N_HEADS // 2):
        o_ref[:, hp * LANES:(hp + 1) * LANES] = _pair_heads(outs[2 * hp], outs[2 * hp + 1]).astype(o_ref.dtype)


def _cmp_to_sel_overlap_t(nb, nsel):
    nc = nb - 1
    cs = np.arange(nc) * CMP_STRIDE
    ss = np.arange(nsel) * SEL_LEN
    ov = np.clip(np.minimum(cs[:, None] + CMP_LEN, ss[None, :] + SEL_LEN) - np.maximum(cs[:, None], ss[None, :]), 0, None)
    out = np.zeros((LANES, nb), np.float32)
    out[:nsel, :nc] = (ov / CMP_LEN).T
    return jnp.asarray(out)


def _nsa(nq, misc, gate_b, g_q, ks, vs, kw, vw, kcmp, vcmp, b, t):
    tq = min(NSA_TILE, t)
    nb = t // CMP_STRIDE
    nsel = t // SEL_LEN
    wt = min(NSA_WINDOW_TILE, tq)
    assert HEAD_DIM + nsel <= LANES and nb % 8 == 0 and tq & (tq - 1) == 0 and tq % LANES == 0
    assert WINDOW % wt == 0 and tq % wt == 0 and wt & (wt - 1) == 0
    gb = jnp.zeros((1, LANES), F32).at[0, _MISC_GATE:_MISC_GATE + 3 * N_HEADS].set(gate_b)
    gq = jnp.tile(g_q, N_HEADS).reshape(1, W_MIX)
    tabs = [jnp.tile(x, (1, N_HEADS)) for x in _rope_tables(jnp.arange(t))]
    nq_tiles = t // tq
    const = lambda shape: pl.BlockSpec(shape, lambda i, j: (0,) * len(shape))
    qtile = lambda width: pl.BlockSpec((tq, width), lambda i, j: (i * nq_tiles + j, 0))
    ptile = pl.BlockSpec((tq, W_MIX), lambda i, j: (j, 0))
    seq = pl.BlockSpec((t, LANES), lambda i, j: (i, 0))
    cmp = pl.BlockSpec((nb, LANES), lambda i, j: (i, 0))
    return pl.pallas_call(
        functools.partial(_nsa_kernel, n_top=min(SEL_TOP, nsel)),
        grid=(b, nq_tiles),
        in_specs=[qtile(W_MIX), qtile(LANES), const((1, LANES)), const((1, W_MIX)), const((W_MIX, W_MIX)),
                  ptile, ptile, ptile, const((LANES, nb)), cmp, pl.BlockSpec((LANES, nb), lambda i, j: (i, 0)),
                  seq, seq, seq, seq],
        out_specs=qtile(W_MIX),
        out_shape=jax.ShapeDtypeStruct((b * t, W_MIX), BF16),
        compiler_params=_params("parallel", "arbitrary"),
        name="nsa",
    )(nq, misc, gb, gq, _group_mean_matrix(W_MIX), *tabs, _cmp_to_sel_overlap_t(nb, nsel),
      kcmp, vcmp, ks, vs, kw, vw)


def kernel(x, ffn1_norm, ffn1_w1, ffn1_w3, ffn1_w2, mix_norm, w_in, w_out, fox_f_bias, fox_q_norm, fox_k_norm, gmlp_v_norm, gmlp_w_s, gmlp_b_s, nsa_q_norm, nsa_kc_norm, nsa_ks_norm, nsa_kw_norm, nsa_cmp_pos_k, nsa_cmp_k_w1, nsa_cmp_k_w2, nsa_cmp_pos_v, nsa_cmp_v_w1, nsa_cmp_v_w2, nsa_gate_bias, pool_w, pool_scale, ffn2_norm, ffn2_w1, ffn2_w3, ffn2_w2):
    b, t, d = x.shape
    assert t % GMLP_CHUNK == 0 and t >= CMP_LEN
    n = b * t
    xf = x.reshape(n, d)
    for l in range(ffn1_norm.shape[0]):
        xf = _ffn(xf, ffn1_norm[l], ffn1_w1[l], ffn1_w3[l], ffn1_w2[l])
        fqkv, guv, nq, nkv, pz, kc, vc, misc = _inproj(xf, mix_norm[l], w_in[l])
        o_a = _fox(fqkv, misc, fox_f_bias[l], fox_q_norm[l], fox_k_norm[l], b, t)
        (o_b,), (o_d,), (ks, vs, kw, vw, kcmp, vcmp) = _run_parts([
            _gmlp(guv, gmlp_v_norm[l], gmlp_w_s[l], gmlp_b_s[l], b, t),
            _pool(pz, pool_w[l], pool_scale[l], b, t),
            _nsa_prep(nkv, kc, vc, nsa_kc_norm[l], nsa_ks_norm[l], nsa_kw_norm[l],
                      nsa_cmp_pos_k[l], nsa_cmp_k_w1[l], nsa_cmp_k_w2[l],
                      nsa_cmp_pos_v[l], nsa_cmp_v_w1[l], nsa_cmp_v_w2[l], b, t)], b, "seq_mixers")
        o_c = _nsa(nq, misc, nsa_gate_bias[l], nsa_q_norm[l], ks, vs, kw, vw, kcmp, vcmp, b, t)
        xf = _ffn(xf, ffn2_norm[l], ffn2_w1[l], ffn2_w3[l], ffn2_w2[l], mixers=(o_a, o_b, o_c, o_d), w_out=w_out[l])
    return xf.reshape(b, t, d)
```

```python
import functools
from typing import Callable, NamedTuple

import numpy as np
import jax
import jax.numpy as jnp
from jax import lax
from jax.experimental import pallas as pl
from jax.experimental.pallas import tpu as pltpu

F32 = jnp.float32
BF16 = jnp.bfloat16
HIGHEST = lax.Precision.HIGHEST

HEAD_DIM = 64
N_HEADS = 4
W_MIX = N_HEADS * HEAD_DIM
ROPE_THETA = 500000.0
ROPE_DIM = HEAD_DIM // 4
ROPE_HALF = ROPE_DIM // 2
GMLP_CHUNK = 128
CMP_LEN = 32
CMP_STRIDE = 16
SEL_LEN = 64
SEL_TOP = 16
WINDOW = 512
POOL_SIZES = (2, 4, 8, 16)
FFN_RES_WEIGHT = 0.5
EPS = 1e-6
NEG_INF = -1e30
SEL_FORCE = 1e3
ATTN_SCALE = HEAD_DIM ** -0.5
LOG2E = 1.4426950408889634

LANES = 128
VMEM_LIMIT = 52 * 1024 * 1024
FFN_TOKENS = 1024
FFN_MIX_TOKENS = 1024
FFN_HIDDEN = 256
FOX_PREP_ROWS = 512
FOX_TILE = 1024
NSA_TILE = 512
NSA_WINDOW_TILE = 512
ONE_LANE = HEAD_DIM


def _params(*sem):
    return pltpu.CompilerParams(dimension_semantics=sem, vmem_limit_bytes=VMEM_LIMIT)


def _dot(a, b, **kw):
    return jnp.dot(a, b, preferred_element_type=F32, **kw)


def _dot_nt(a, b):
    return lax.dot_general(a, b, (((1,), (1,)), ((), ())), preferred_element_type=F32)


def _rms(x):
    return x * lax.rsqrt(jnp.mean(x * x, axis=-1, keepdims=True) + EPS)


def _group_rms(x, gmean):
    ms = _dot((x * x).astype(BF16), gmean)
    return x * lax.rsqrt(ms + EPS)


def _rope(x, c, s_up, s_dn):
    w = x.shape[-1]
    return x * c + pltpu.roll(x, w - ROPE_HALF, axis=1) * s_up + pltpu.roll(x, ROPE_HALF, axis=1) * s_dn


_GELU_C = 0.7978845608028654


def _gelu(x):
    half = 0.5 * x
    return half + half * jnp.tanh(x * (_GELU_C + (_GELU_C * 0.044715) * (x * x)))


def _log_sigmoid(x):
    return jnp.minimum(x, 0.0) - jnp.log(1.0 + jnp.exp(-jnp.abs(x)))


def _head_column(x, h):
    col = x[:, (h // 2) * LANES:(h // 2 + 1) * LANES]
    return pltpu.roll(col, HEAD_DIM, axis=1) if h % 2 else col


def _pair_heads(even, odd):
    lane = lax.broadcasted_iota(jnp.int32, even.shape, 1)
    return jnp.where(lane < HEAD_DIM, even, pltpu.roll(odd, HEAD_DIM, axis=1))


def _ones_lane(shape):
    return jnp.where(lax.broadcasted_iota(jnp.int32, shape, 1) == ONE_LANE, 1.0, 0.0)


def _ffn_kernel(x_ref, g_ref, w1_ref, w3_ref, w2_ref, *rest):
    *mix_refs, o_ref = rest
    x = x_ref[...]
    if mix_refs:
        *mixers, wout_ref = mix_refs
        for k, ref in enumerate(mixers):
            x = x + _dot(ref[...], wout_ref[k * W_MIX:(k + 1) * W_MIX, :])
    h = (_rms(x) * g_ref[...]).astype(BF16)
    acc = jnp.zeros_like(x)
    for c in range(w1_ref.shape[1] // FFN_HIDDEN):
        cs = slice(c * FFN_HIDDEN, (c + 1) * FFN_HIDDEN)
        a = _dot(h, w1_ref[:, cs])
        b = _dot(h, w3_ref[:, cs])
        acc = acc + _dot((a * jax.nn.sigmoid(a) * b).astype(BF16), w2_ref[cs, :])
    o_ref[...] = x + FFN_RES_WEIGHT * acc


def _ffn(x, g, w1, w3, w2, mixers=(), w_out=None):
    n, d = x.shape
    tm = min(FFN_MIX_TOKENS if mixers else FFN_TOKENS, n)
    resident = lambda shape: pl.BlockSpec(shape, lambda i: (0, 0), pipeline_mode=pl.Buffered(1))
    mix_specs = [pl.BlockSpec((tm, W_MIX), lambda i: (i, 0)) for _ in mixers] + ([resident(w_out.shape)] if mixers else [])
    mix_args = list(mixers) + ([w_out.astype(BF16)] if mixers else [])
    return pl.pallas_call(
        _ffn_kernel,
        grid=(n // tm,),
        in_specs=[pl.BlockSpec((tm, d), lambda i: (i, 0)), resident((1, d)),
                  resident(w1.shape), resident(w3.shape), resident(w2.shape)] + mix_specs,
        out_specs=pl.BlockSpec((tm, d), lambda i: (i, 0)),
        out_shape=jax.ShapeDtypeStruct((n, d), F32),
        compiler_params=_params("parallel"),
        name="ffn_mix" if mixers else "ffn",
    )(x, g.reshape(1, d), w1.astype(BF16), w3.astype(BF16), w2.astype(BF16), *mix_args)


_IN_GROUPS = (768, 512, 256, 256, 256, 128, 128)
_MISC_FORGET = 0
_MISC_GATE = 4


def _inproj_kernel(x_ref, g_ref, w_ref, fqkv_ref, guv_ref, nq_ref, nkv_ref, pz_ref, kc_ref, vc_ref, misc_ref, kcvc_s):
    h = (_rms(x_ref[...]) * g_ref[...]).astype(BF16)
    off = 0
    for ref, width in zip((fqkv_ref, guv_ref, nq_ref, nkv_ref, pz_ref), _IN_GROUPS[:5]):
        ref[...] = _dot(h, w_ref[:, off:off + width])
        off += width
    tail = _dot(h, w_ref[:, off:off + 2 * LANES])
    misc_ref[...] = tail[:, LANES:]
    kcvc_s[...] = tail[:, :LANES]
    groups = x_ref.shape[0] // CMP_STRIDE
    lane = lax.broadcasted_iota(jnp.int32, (groups, LANES), 1)
    for j in range(0, CMP_STRIDE, 2):
        even = kcvc_s[pl.ds(j, groups, stride=CMP_STRIDE), :]
        odd = kcvc_s[pl.ds(j + 1, groups, stride=CMP_STRIDE), :]
        col = slice((j // 2) * LANES, (j // 2 + 1) * LANES)
        kc_ref[:, col] = jnp.where(lane < HEAD_DIM, even, pltpu.roll(odd, HEAD_DIM, axis=1))
        vc_ref[:, col] = jnp.where(lane < HEAD_DIM, pltpu.roll(even, HEAD_DIM, axis=1), odd)


def _relayout_w_in(w_in):
    o = np.cumsum((0, 256, 256, 256, 4, 256, 256, 256, 64, 64, 64, 64, 64, 64, 12, 256))
    fq, ff, gu, nq, nkc, nks, ng, pz, end = o[0], o[3], o[4], o[6], o[7], o[9], o[13], o[14], o[15]
    d = w_in.shape[0]
    pad = jnp.zeros((d, LANES - 16), w_in.dtype)
    return jnp.concatenate([
        w_in[:, fq:ff], w_in[:, gu:nq], w_in[:, nq:nkc], w_in[:, nks:ng], w_in[:, pz:end],
        w_in[:, nkc:nks], w_in[:, ff:gu], w_in[:, ng:pz], pad], axis=1)


def _inproj(x, g, w_in):
    n, d = x.shape
    tm = min(FFN_TOKENS, n)
    assert tm % (8 * CMP_STRIDE) == 0
    w = _relayout_w_in(w_in).astype(BF16)
    flat = CMP_STRIDE * HEAD_DIM
    outs = [(tm, n, wd) for wd in (768, 512, 256, 256, 256)] + [(tm // CMP_STRIDE, n // CMP_STRIDE, flat)] * 2 + [(tm, n, LANES)]
    return pl.pallas_call(
        _inproj_kernel,
        grid=(n // tm,),
        in_specs=[
            pl.BlockSpec((tm, d), lambda i: (i, 0)),
            pl.BlockSpec((1, d), lambda i: (0, 0)),
            pl.BlockSpec(w.shape, lambda i: (0, 0)),
        ],
        out_specs=[pl.BlockSpec((rows, wd), lambda i: (i, 0)) for rows, _, wd in outs],
        out_shape=[jax.ShapeDtypeStruct((total, wd), F32) for _, total, wd in outs],
        scratch_shapes=[pltpu.VMEM((tm, LANES), F32)],
        compiler_params=_params("parallel"),
        name="inproj",
    )(x, g.reshape(1, d), w)


def _bf16_pieces(x):
    p1 = x.astype(BF16).astype(F32)
    r1 = x - p1
    p2 = r1.astype(BF16).astype(F32)
    p3 = (r1 - p2).astype(BF16).astype(F32)
    return p1, p2, p3


_N_PIECES = 3
_N_EXTRA = _N_PIECES * N_HEADS
_PACK_ONE = _N_EXTRA


def _fox_placement():
    pq = np.zeros((LANES, LANES), np.float32)
    pk = np.zeros((LANES, LANES), np.float32)
    for h in range(N_HEADS):
        for p in range(_N_PIECES):
            pq[p * N_HEADS + h, HEAD_DIM + _N_PIECES * h + p] = 1.0
            pk[p * N_HEADS + h, HEAD_DIM + _N_EXTRA + _N_PIECES * h + p] = -1.0
    pq[_PACK_ONE, HEAD_DIM + _N_EXTRA:HEAD_DIM + 2 * _N_EXTRA] = 1.0
    pk[_PACK_ONE, HEAD_DIM:HEAD_DIM + _N_EXTRA] = 1.0
    return jnp.asarray(np.concatenate([pq, pk], axis=1), BF16)


def _fox_kernel(qkv_ref, misc_ref, fbt_ref, gq_ref, gk_ref, gmean_ref, pqk_ref, o_ref, q_s, k_s, v_s, o_s):
    t = qkv_ref.shape[0]
    pr = min(FOX_PREP_ROWS, t)
    gmean = gmean_ref[...]
    upper = (lax.broadcasted_iota(jnp.int32, (pr, pr), 0) <= lax.broadcasted_iota(jnp.int32, (pr, pr), 1)).astype(BF16)
    row8 = lax.broadcasted_iota(jnp.int32, (8, pr), 0)
    lane = lax.broadcasted_iota(jnp.int32, (pr, LANES), 1)
    extra = lane - HEAD_DIM
    own = [((extra >= _N_PIECES * h) & (extra < _N_PIECES * (h + 1)))
           | ((extra >= _N_EXTRA + _N_PIECES * h) & (extra < _N_EXTRA + _N_PIECES * (h + 1))) for h in range(N_HEADS)]
    ones_v = jnp.where(extra == 0, 1.0, 0.0)
    carry = jnp.zeros((8, pr), F32)
    for r in range(t // pr):
        sl = slice(r * pr, (r + 1) * pr)
        logits_t = misc_ref[sl, :].T[0:8, :]
        log_f = _log_sigmoid(logits_t + fbt_ref[...]) * LOG2E
        c = sum(_dot(piece.astype(BF16), upper) for piece in _bf16_pieces(log_f)) + carry
        carry = jnp.broadcast_to(c[:, pr - 1:pr], (8, pr))
        p1, p2, p3 = _bf16_pieces(c)
        lo = jnp.where(row8 < N_HEADS, p1, pltpu.roll(p2, N_HEADS, axis=0))
        hi = jnp.where(row8 < N_HEADS, p3, jnp.where(row8 == _PACK_ONE - 8, 1.0, 0.0))
        packed = jnp.concatenate([lo, hi, jnp.zeros((LANES - 16, pr), F32)], axis=0).T.astype(BF16)
        q_extra = _dot(packed, pqk_ref[:, :LANES])
        k_extra = _dot(packed, pqk_ref[:, LANES:])
        qn = _group_rms(qkv_ref[sl, 0:W_MIX], gmean) * gq_ref[...] * (ATTN_SCALE * LOG2E)
        kn = _group_rms(qkv_ref[sl, W_MIX:2 * W_MIX], gmean) * gk_ref[...]
        v = qkv_ref[sl, 2 * W_MIX:3 * W_MIX]
        for h in range(N_HEADS):
            q_s[h, sl, :] = jnp.where(extra < 0, _head_column(qn, h), q_extra).astype(BF16)
            k_s[h, sl, :] = jnp.where(extra < 0, _head_column(kn, h), jnp.where(own[h], k_extra, 0.0)).astype(BF16)
            v_aug = jnp.where(extra < 0, _head_column(v, h), ones_v)
            v_s[h, :, sl] = v_aug.T.astype(BF16)

    tq = min(FOX_TILE, t)
    causal = lax.broadcasted_iota(jnp.int32, (tq, tq), 0) <= lax.broadcasted_iota(jnp.int32, (tq, tq), 1)

    def one_head(h):
        for qi in range(t // tq):
            dsl = slice(qi * tq, (qi + 1) * tq)
            q = q_s[h, dsl, :]
            s_d = jnp.where(causal, _dot_nt(k_s[h, dsl, :], q), NEG_INF)
            m = jnp.max(s_d, axis=0, keepdims=True)
            if qi:
                s_o = _dot_nt(k_s[h, 0:qi * tq, :], q)
                m = jnp.maximum(m, jnp.max(s_o, axis=0, keepdims=True))
            acc = _dot(v_s[h, :, dsl], jnp.exp2(s_d - m).astype(BF16))
            if qi:
                acc = acc + _dot(v_s[h, :, 0:qi * tq], jnp.exp2(s_o - m).astype(BF16))
            o_s[h, dsl, :] = (acc / acc[ONE_LANE:ONE_LANE + 1, :]).T

    for h in range(N_HEADS):
        one_head(h)
    for hp in range(N_HEADS // 2):
        o_ref[:, hp * LANES:(hp + 1) * LANES] = _pair_heads(o_s[2 * hp], o_s[2 * hp + 1]).astype(o_ref.dtype)


def _group_mean_matrix(width):
    g = np.kron(np.eye(width // HEAD_DIM, dtype=np.float32), np.full((HEAD_DIM, HEAD_DIM), 1.0 / HEAD_DIM, np.float32))
    return jnp.asarray(g, BF16)


def _fox(fqkv, misc, f_bias, g_q, g_k, b, t):
    assert t % min(FOX_TILE, t) == 0 and t % min(FOX_PREP_ROWS, t) == 0
    pr = min(FOX_PREP_ROWS, t)
    assert _MISC_FORGET == 0 and N_HEADS <= 4 and pr % LANES == 0
    fbt = jnp.broadcast_to(jnp.zeros((8,), F32).at[:N_HEADS].set(f_bias)[:, None], (8, pr))
    gq = jnp.tile(g_q, N_HEADS).reshape(1, W_MIX)
    gk = jnp.tile(g_k, N_HEADS).reshape(1, W_MIX)
    const = _const_spec
    head_scratch = lambda dtype: pltpu.VMEM((N_HEADS, t, LANES), dtype)
    return _Part(
        _fox_kernel,
        [fqkv, misc, fbt, gq, gk, _group_mean_matrix(W_MIX), _fox_placement()],
        [pl.BlockSpec((t, 3 * W_MIX), lambda i: (i, 0)), pl.BlockSpec((t, LANES), lambda i: (i, 0)),
         const((8, pr)), const((1, W_MIX)), const((1, W_MIX)), const((W_MIX, W_MIX)), const((LANES, 2 * LANES))],
        [pl.BlockSpec((t, W_MIX), lambda i: (i, 0))],
        [jax.ShapeDtypeStruct((b * t, W_MIX), BF16)],
        (head_scratch(BF16), head_scratch(BF16), pltpu.VMEM((N_HEADS, LANES, t), BF16), head_scratch(F32)))


def _gmlp_kernel(uv_ref, gv_ref, gmean_ref, w_ref, bias_ref, o_ref):
    c = GMLP_CHUNK
    rows = lax.broadcasted_iota(jnp.int32, (c, c), 0)
    cols = lax.broadcasted_iota(jnp.int32, (c, c), 1)
    lane_group = lax.broadcasted_iota(jnp.int32, (c, W_MIX), 1) // HEAD_DIM
    w_cat = jnp.concatenate([jnp.where(cols <= rows, w_ref[g], 0.0).astype(BF16) for g in range(N_HEADS)], axis=1)
    v_all = (_group_rms(_gelu(uv_ref[:, W_MIX:2 * W_MIX]), gmean_ref[...]) * gv_ref[...]).astype(BF16)
    for r in range(uv_ref.shape[0] // c):
        sl = slice(r * c, (r + 1) * c)
        u = _gelu(uv_ref[sl, 0:W_MIX])
        v = v_all[sl]
        v_stack = jnp.concatenate([jnp.where(lane_group == g, v, jnp.zeros_like(v)) for g in range(N_HEADS)], axis=0)
        o_ref[sl, :] = (u * (bias_ref[...] + _dot(w_cat, v_stack))).astype(o_ref.dtype)


class _Part(NamedTuple):
    body: Callable
    args: list
    in_specs: list
    out_specs: list
    out_shape: list
    scratch_shapes: tuple = ()


def _const_spec(shape):
    return pl.BlockSpec(shape, lambda i: (0,) * len(shape))


def _run_parts(parts, b, name):
    n_in = [len(p.in_specs) for p in parts]
    n_out = [len(p.out_specs) for p in parts]
    n_scr = [len(p.scratch_shapes) for p in parts]

    def body(*refs):
        ins, outs, scr = refs[:sum(n_in)], refs[sum(n_in):sum(n_in) + sum(n_out)], refs[sum(n_in) + sum(n_out):]
        for p, i0, o0, s0 in zip(parts, np.cumsum([0] + n_in), np.cumsum([0] + n_out), np.cumsum([0] + n_scr)):
            p.body(*ins[i0:i0 + len(p.in_specs)], *outs[o0:o0 + len(p.out_specs)], *scr[s0:s0 + len(p.scratch_shapes)])

    outs = pl.pallas_call(
        body,
        grid=(b,),
        in_specs=[s for p in parts for s in p.in_specs],
        out_specs=[s for p in parts for s in p.out_specs],
        out_shape=[s for p in parts for s in p.out_shape],
        scratch_shapes=[s for p in parts for s in p.scratch_shapes],
        compiler_params=_params("parallel"),
        name=name,
    )(*[a for p in parts for a in p.args])
    return [outs[o0:o0 + k] for o0, k in zip(np.cumsum([0] + n_out), n_out)]


def _gmlp(guv, g_v, w_s, b_s, b, t):
    bias = jnp.repeat(b_s.T, HEAD_DIM, axis=1)
    return _Part(
        _gmlp_kernel,
        [guv, g_v.reshape(1, W_MIX), _group_mean_matrix(W_MIX), w_s, bias],
        [pl.BlockSpec((t, 2 * W_MIX), lambda i: (i, 0)), _const_spec((1, W_MIX)), _const_spec((W_MIX, W_MIX)),
         _const_spec(w_s.shape), _const_spec(bias.shape)],
        [pl.BlockSpec((t, W_MIX), lambda i: (i, 0))],
        [jax.ShapeDtypeStruct((b * t, W_MIX), BF16)])


def _pool_kernel(z_ref, inv_cnt_ref, w_ref, scale_ref, o_ref):
    t = z_ref.shape[0]
    z = z_ref[...]
    row = lax.broadcasted_iota(jnp.int32, (t, W_MIX), 0)
    lane_group = lax.broadcasted_iota(jnp.int32, (t, W_MIX), 1) // HEAD_DIM
    sums = {}
    s = z
    k = 1
    while k < max(POOL_SIZES):
        s = s + jnp.where(row >= k, pltpu.roll(s, k, axis=0), 0.0)
        k *= 2
        sums[k] = s
    win_sum = sums[POOL_SIZES[-1]]
    for g in range(len(POOL_SIZES) - 2, -1, -1):
        win_sum = jnp.where(lane_group == g, sums[POOL_SIZES[g]], win_sum)
    pooled = win_sum * inv_cnt_ref[...] - z
    o_ref[...] = (_dot(pooled.astype(BF16), w_ref[...]) * scale_ref[...]).astype(o_ref.dtype)


def _pool(pz, w_p, scale, b, t):
    w_bd = jax.scipy.linalg.block_diag(*[w_p[g] for g in range(N_HEADS)]).astype(BF16)
    win = jnp.repeat(jnp.array(POOL_SIZES, jnp.int32), HEAD_DIM)
    inv_cnt = 1.0 / jnp.minimum(jnp.arange(t)[:, None] + 1, win[None, :]).astype(F32)
    return _Part(
        _pool_kernel,
        [pz, inv_cnt, w_bd, scale.reshape(1, W_MIX)],
        [pl.BlockSpec((t, W_MIX), lambda i: (i, 0)), _const_spec((t, W_MIX)), _const_spec((W_MIX, W_MIX)),
         _const_spec((1, W_MIX))],
        [pl.BlockSpec((t, W_MIX), lambda i: (i, 0))],
        [jax.ShapeDtypeStruct((b * t, W_MIX), BF16)])


def _nsa_prep_kernel(kv_ref, kc_ref, vc_ref, posk_ref, posv_ref, kw1_ref, vw1_ref, kw2_ref, vw2_ref,
                     gkv_ref, nmask_ref, gmean_ref, c_ref, su_ref, sd_ref, gc_ref, cc_ref, csu_ref, csd_ref,
                     ks_ref, vs_ref, kw_ref, vw_ref, kcmp_ref, vcmp_ref):
    t = kv_ref.shape[0]
    x = kv_ref[...]
    normed = _group_rms(x, gmean_ref[...]) * gkv_ref[...]
    y = _rope(jnp.where(nmask_ref[...] > 0.5, normed, x), c_ref[...], su_ref[...], sd_ref[...])
    lane = lax.broadcasted_iota(jnp.int32, (t, LANES), 1)
    block = lax.broadcasted_iota(jnp.int32, (t, LANES), 0) // SEL_LEN
    key = lane < HEAD_DIM
    ks_v, kw_v = y[:, 0:LANES], y[:, LANES:2 * LANES]
    ks_ref[...] = jnp.where(key, ks_v, jnp.where(lane - HEAD_DIM == block, 1.0, 0.0)).astype(BF16)
    vs_ref[...] = jnp.where(key, pltpu.roll(ks_v, HEAD_DIM, axis=1), _ones_lane((t, LANES))).astype(BF16)
    kw_ref[...] = jnp.where(key, kw_v, 0.0).astype(BF16)
    vw_ref[...] = jnp.where(key, pltpu.roll(kw_v, HEAD_DIM, axis=1), _ones_lane((t, LANES))).astype(BF16)

    half = kc_ref.shape[1]
    nb = kc_ref.shape[0]

    def hidden(x_ref, pos_ref, w1_ref):
        top = _dot((x_ref[...] + pos_ref[:, 0:half]).astype(BF16), w1_ref[0:half, :])
        bot = _dot((x_ref[...] + pos_ref[:, half:2 * half]).astype(BF16), w1_ref[half:2 * half, :])
        return _gelu(top + pltpu.roll(bot, nb - 1, axis=0)).astype(BF16)

    kv_cmp = _dot(hidden(kc_ref, posk_ref, kw1_ref), kw2_ref[...]) + _dot(hidden(vc_ref, posv_ref, vw1_ref), vw2_ref[...])
    key = lax.broadcasted_iota(jnp.int32, kv_cmp.shape, 1) < HEAD_DIM
    normed = _group_rms(kv_cmp, gmean_ref[0:LANES, 0:LANES]) * gc_ref[...]
    y = _rope(jnp.where(key, normed, kv_cmp), cc_ref[...], csu_ref[...], csd_ref[...])
    kcmp_ref[...] = jnp.where(key, y, 0.0).astype(BF16)
    vcmp_ref[...] = jnp.where(key, pltpu.roll(y, HEAD_DIM, axis=1), _ones_lane(y.shape)).T.astype(BF16)


def _rope_tables(pos):
    inv = ROPE_THETA ** (-jnp.arange(ROPE_HALF, dtype=F32) * 2.0 / ROPE_DIM)
    ang = pos.astype(F32)[:, None] * inv[None, :]
    cos, sin = jnp.cos(ang), jnp.sin(ang)
    n = pos.shape[0]
    zero = jnp.zeros((n, ROPE_HALF), F32)
    rest0 = jnp.zeros((n, HEAD_DIM - ROPE_DIM), F32)
    c = jnp.concatenate([cos, cos, rest0 + 1.0], axis=1)
    s_up = jnp.concatenate([-sin, zero, rest0], axis=1)
    s_dn = jnp.concatenate([zero, sin, rest0], axis=1)
    return c, s_up, s_dn


def _identity_tables(n):
    return jnp.ones((n, HEAD_DIM), F32), jnp.zeros((n, HEAD_DIM), F32), jnp.zeros((n, HEAD_DIM), F32)


def _nsa_prep(nkv, kc2, vc2, g_kc, g_ks, g_kw, pos_k, k_w1, k_w2, pos_v, v_w1, v_w2, b, t):
    nb = t // CMP_STRIDE
    assert HEAD_DIM + t // SEL_LEN <= LANES
    flat = CMP_STRIDE * HEAD_DIM
    rope_t = _rope_tables(jnp.arange(t))
    iden_t = _identity_tables(t)
    tabs = [jnp.concatenate([r, i, r, i], axis=1) for r, i in zip(rope_t, iden_t)]
    rope_c = _rope_tables(jnp.arange(nb) * CMP_STRIDE + CMP_LEN - 1)
    iden_c = _identity_tables(nb)
    tabs_c = [jnp.concatenate([r, i], axis=1) for r, i in zip(rope_c, iden_c)]
    one = jnp.ones((HEAD_DIM,), F32)
    gkv = jnp.concatenate([g_ks, one, g_kw, one]).reshape(1, W_MIX)
    nmask = jnp.concatenate([one, 0 * one, one, 0 * one]).reshape(1, W_MIX)
    gc = jnp.concatenate([g_kc, one]).reshape(1, LANES)
    zpad = jnp.zeros_like(k_w2)
    kw2 = jnp.concatenate([k_w2, zpad], axis=1).astype(BF16)
    vw2 = jnp.concatenate([zpad, v_w2], axis=1).astype(BF16)
    const = _const_spec
    seq = lambda width: pl.BlockSpec((t, width), lambda i: (i, 0))
    cmp_in = pl.BlockSpec((nb, flat), lambda i: (i, 0))
    cmp_out = pl.BlockSpec((nb, LANES), lambda i: (i, 0))
    return _Part(
        _nsa_prep_kernel,
        [nkv, kc2, vc2, pos_k.reshape(1, 2 * flat), pos_v.reshape(1, 2 * flat),
         k_w1.astype(BF16), v_w1.astype(BF16), kw2, vw2, gkv, nmask, _group_mean_matrix(W_MIX), *tabs, gc, *tabs_c],
        [seq(W_MIX), cmp_in, cmp_in, const((1, 2 * flat)), const((1, 2 * flat)),
         const(k_w1.shape), const(v_w1.shape), const(kw2.shape), const(vw2.shape),
         const((1, W_MIX)), const((1, W_MIX)), const((W_MIX, W_MIX)),
         const((t, W_MIX)), const((t, W_MIX)), const((t, W_MIX)),
         const((1, LANES)), const((nb, LANES)), const((nb, LANES)), const((nb, LANES))],
        [seq(LANES)] * 4 + [cmp_out, pl.BlockSpec((LANES, nb), lambda i: (i, 0))],
        [jax.ShapeDtypeStruct((b * t, LANES), BF16)] * 4
        + [jax.ShapeDtypeStruct((b * nb, LANES), BF16), jax.ShapeDtypeStruct((b * LANES, nb), BF16)])


def _nsa_kernel(*refs, n_top):
    q_ref, ks_ref = refs[0], refs[11]
    variants = [functools.partial(_nsa_tile, *refs, n_top=n_top, qi=k) for k in range(ks_ref.shape[0] // q_ref.shape[0])]
    lax.switch(pl.program_id(1), variants)


def _nsa_tile(q_ref, misc_ref, gb_ref, gq_ref, gmean_ref, c_ref, su_ref, sd_ref, ovt_ref,
              kcmp_ref, vcmp_ref, ks_ref, vs_ref, kw_ref, vw_ref, o_ref, *, n_top, qi):
    tq = q_ref.shape[0]
    t0 = qi * tq
    nb = kcmp_ref.shape[0]
    nsel = ks_ref.shape[0] // SEL_LEN
    rows = N_HEADS * tq

    qn = _rope(_group_rms(q_ref[...], gmean_ref[...]) * gq_ref[...], c_ref[...], su_ref[...], sd_ref[...]) * (ATTN_SCALE * LOG2E)
    q_cols = [_head_column(qn, h) for h in range(N_HEADS)]
    lane_q = lax.broadcasted_iota(jnp.int32, (tq, LANES), 1)

    q0 = jnp.concatenate([jnp.where(lane_q < HEAD_DIM, col, 0.0) for col in q_cols], axis=0).astype(BF16)
    cmp_end = lax.broadcasted_iota(jnp.int32, (nb, rows), 0) * CMP_STRIDE + (CMP_LEN - 1)
    qpos_c = (lax.broadcasted_iota(jnp.int32, (nb, rows), 1) & (tq - 1)) + t0
    m_cmp = cmp_end <= qpos_c
    s = jnp.where(m_cmp, _dot_nt(kcmp_ref[...], q0), NEG_INF)
    e = jnp.where(m_cmp, jnp.exp2(s - jnp.max(s, axis=0, keepdims=True)), 0.0)
    denom = jnp.sum(e, axis=0, keepdims=True)
    inv = 1.0 / jnp.where(denom > 0.0, denom, 1.0)
    p_cmp = e * inv
    o_cmp = (_dot(vcmp_ref[...], e.astype(BF16)) * inv).T

    p_sum = p_cmp[:, 0:tq] + p_cmp[:, tq:2 * tq] + p_cmp[:, 2 * tq:3 * tq] + p_cmp[:, 3 * tq:4 * tq]
    nsp = -(-nsel // 8) * 8
    imp = _dot(ovt_ref[0:nsp, :], p_sum, precision=HIGHEST)
    blk = lax.broadcasted_iota(jnp.int32, (nsp, tq), 0)
    cur = (lax.broadcasted_iota(jnp.int32, (nsp, tq), 1) + t0) // SEL_LEN
    valid = blk <= cur
    forced = ((blk == 0) | (blk == cur) | (blk == cur - 1)).astype(F32)
    imp = jnp.where(valid, imp + SEL_FORCE * forced, NEG_INF)
    cur_row = (lax.broadcasted_iota(jnp.int32, (1, tq), 1) + t0) // SEL_LEN
    outranked = jnp.zeros((nsp, tq), F32)
    for j in range(nsel):
        rival = imp[j:j + 1, :]
        wins = (rival > imp) | ((rival == imp) & (blk > j))
        outranked = outranked + jnp.where(wins & (cur_row >= j), 1.0, 0.0)
    keep = (outranked < n_top) & valid
    bias_t = jnp.where(keep | (blk >= nsel), 0.0, NEG_INF)
    if nsp < LANES:
        bias_t = jnp.concatenate([bias_t, jnp.zeros((LANES - nsp, tq), F32)], axis=0)
    bias = jnp.concatenate([bias_t[:, c * LANES:(c + 1) * LANES].T for c in range(tq // LANES)], axis=0)
    bias = pltpu.roll(bias, HEAD_DIM, axis=1)
    q = jnp.concatenate([jnp.where(lane_q < HEAD_DIM, col, bias) for col in q_cols], axis=0).astype(BF16)

    trow = lax.broadcasted_iota(jnp.int32, (rows, tq), 0) & (tq - 1)
    tcol = lax.broadcasted_iota(jnp.int32, (rows, tq), 1)
    causal_bias = jnp.where(tcol <= trow, 0.0, NEG_INF)

    def tile(j):
        return slice(j * tq, (j + 1) * tq)

    def sel_step(kj, extra_bias, st):
        m, acc = st
        s = _dot_nt(q, ks_ref[tile(kj), :])
        if extra_bias is not None:
            s = s + extra_bias
        m_new = jnp.maximum(m, jnp.max(s, axis=-1, keepdims=True))
        acc = jnp.exp2(m - m_new) * acc + _dot(jnp.exp2(s - m_new).astype(BF16), vs_ref[tile(kj), :])
        return m_new, acc

    st = (jnp.full((rows, 1), NEG_INF, F32), jnp.zeros((rows, LANES), F32))
    for kj in range(qi):
        st = sel_step(kj, None, st)
    _, acc = sel_step(qi, causal_bias, st)
    o_sel = acc / acc[:, ONE_LANE:ONE_LANE + 1]

    wt = min(NSA_WINDOW_TILE, tq)
    span = WINDOW // wt
    wrow = lax.broadcasted_iota(jnp.int32, (N_HEADS * wt, wt), 0) & (wt - 1)
    wcol = lax.broadcasted_iota(jnp.int32, (N_HEADS * wt, wt), 1)
    o_win = []
    for sub in range(tq // wt):
        q_sub = jnp.concatenate([q[h * tq + sub * wt:h * tq + (sub + 1) * wt] for h in range(N_HEADS)], axis=0)
        diag = qi * (tq // wt) + sub
        parts = []
        for d in range(min(span, diag), -1, -1):
            ksl = slice((diag - d) * wt, (diag - d + 1) * wt)
            s = _dot_nt(q_sub, kw_ref[ksl, :])
            if d == 0:
                s = s + jnp.where(wcol <= wrow, 0.0, NEG_INF)
            elif d == span:
                s = s + jnp.where(wcol > wrow, 0.0, NEG_INF)
            parts.append((s, ksl))
        m = functools.reduce(jnp.maximum, [jnp.max(s, axis=-1, keepdims=True) for s, _ in parts])
        acc = sum(_dot(jnp.exp2(s - m).astype(BF16), vw_ref[ksl, :]) for s, ksl in parts)
        o_win.append(acc / acc[:, ONE_LANE:ONE_LANE + 1])

    gate = jax.nn.sigmoid(misc_ref[...] + gb_ref[...])
    outs = []
    for h in range(N_HEADS):
        hr = slice(h * tq, (h + 1) * tq)
        g0 = _MISC_GATE + 3 * h
        o_win_h = jnp.concatenate([part[h * wt:(h + 1) * wt] for part in o_win], axis=0)
        outs.append(gate[:, g0:g0 + 1] * o_cmp[hr] + gate[:, g0 + 1:g0 + 2] * o_sel[hr] + gate[:, g0 + 2:g0 + 3] * o_win_h)
    for hp in range(N_HEADS // 2):
        o_ref[:, hp * LANES:(hp + 1) * LANES] = _pair_heads(outs[2 * hp], outs[2 * hp + 1]).astype(o_ref.dtype)


def _cmp_to_sel_overlap_t(nb, nsel):
    nc = nb - 1
    cs = np.arange(nc) * CMP_STRIDE
    ss = np.arange(nsel) * SEL_LEN
    ov = np.clip(np.minimum(cs[:, None] + CMP_LEN, ss[None, :] + SEL_LEN) - np.maximum(cs[:, None], ss[None, :]), 0, None)
    out = np.zeros((LANES, nb), np.float32)
    out[:nsel, :nc] = (ov / CMP_LEN).T
    return jnp.asarray(out)


def _nsa(nq, misc, gate_b, g_q, ks, vs, kw, vw, kcmp, vcmp, b, t):
    tq = min(NSA_TILE, t)
    nb = t // CMP_STRIDE
    nsel = t // SEL_LEN
    wt = min(NSA_WINDOW_TILE, tq)
    assert HEAD_DIM + nsel <= LANES and nb % 8 == 0 and tq & (tq - 1) == 0 and tq % LANES == 0
    assert WINDOW % wt == 0 and tq % wt == 0 and wt & (wt - 1) == 0
    gb = jnp.zeros((1, LANES), F32).at[0, _MISC_GATE:_MISC_GATE + 3 * N_HEADS].set(gate_b)
    gq = jnp.tile(g_q, N_HEADS).reshape(1, W_MIX)
    tabs = [jnp.tile(x, (1, N_HEADS)) for x in _rope_tables(jnp.arange(t))]
    nq_tiles = t // tq
    const = lambda shape: pl.BlockSpec(shape, lambda i, j: (0,) * len(shape))
    qtile = lambda width: pl.BlockSpec((tq, width), lambda i, j: (i * nq_tiles + j, 0))
    ptile = pl.BlockSpec((tq, W_MIX), lambda i, j: (j, 0))
    seq = pl.BlockSpec((t, LANES), lambda i, j: (i, 0))
    cmp = pl.BlockSpec((nb, LANES), lambda i, j: (i, 0))
    return pl.pallas_call(
        functools.partial(_nsa_kernel, n_top=min(SEL_TOP, nsel)),
        grid=(b, nq_tiles),
        in_specs=[qtile(W_MIX), qtile(LANES), const((1, LANES)), const((1, W_MIX)), const((W_MIX, W_MIX)),
                  ptile, ptile, ptile, const((LANES, nb)), cmp, pl.BlockSpec((LANES, nb), lambda i, j: (i, 0)),
                  seq, seq, seq, seq],
        out_specs=qtile(W_MIX),
        out_shape=jax.ShapeDtypeStruct((b * t, W_MIX), BF16),
        compiler_params=_params("parallel", "arbitrary"),
        name="nsa",
    )(nq, misc, gb, gq, _group_mean_matrix(W_MIX), *tabs, _cmp_to_sel_overlap_t(nb, nsel),
      kcmp, vcmp, ks, vs, kw, vw)


def kernel(x, ffn1_norm, ffn1_w1, ffn1_w3, ffn1_w2, mix_norm, w_in, w_out, fox_f_bias, fox_q_norm, fox_k_norm, gmlp_v_norm, gmlp_w_s, gmlp_b_s, nsa_q_norm, nsa_kc_norm, nsa_ks_norm, nsa_kw_norm, nsa_cmp_pos_k, nsa_cmp_k_w1, nsa_cmp_k_w2, nsa_cmp_pos_v, nsa_cmp_v_w1, nsa_cmp_v_w2, nsa_gate_bias, pool_w, pool_scale, ffn2_norm, ffn2_w1, ffn2_w3, ffn2_w2):
    b, t, d = x.shape
    assert t % GMLP_CHUNK == 0 and t >= CMP_LEN
    n = b * t
    xf = x.reshape(n, d)
    for l in range(ffn1_norm.shape[0]):
        xf = _ffn(xf, ffn1_norm[l], ffn1_w1[l], ffn1_w3[l], ffn1_w2[l])
        fqkv, guv, nq, nkv, pz, kc, vc, misc = _inproj(xf, mix_norm[l], w_in[l])
        ((o_a,),) = _run_parts([_fox(fqkv, misc, fox_f_bias[l], fox_q_norm[l], fox_k_norm[l], b, t)], b, "fox")
        (o_b,), (o_d,), (ks, vs, kw, vw, kcmp, vcmp) = _run_parts([
            _gmlp(guv, gmlp_v_norm[l], gmlp_w_s[l], gmlp_b_s[l], b, t),
            _pool(pz, pool_w[l], pool_scale[l], b, t),
            _nsa_prep(nkv, kc, vc, nsa_kc_norm[l], nsa_ks_norm[l], nsa_kw_norm[l],
                      nsa_cmp_pos_k[l], nsa_cmp_k_w1[l], nsa_cmp_k_w2[l],
                      nsa_cmp_pos_v[l], nsa_cmp_v_w1[l], nsa_cmp_v_w2[l], b, t)], b, "seq_mixers")
        o_c = _nsa(nq, misc, nsa_gate_bias[l], nsa_q_norm[l], ks, vs, kw, vw, kcmp, vcmp, b, t)
        xf = _ffn(xf, ffn2_norm[l], ffn2_w1[l], ffn2_w3[l], ffn2_w2[l], mixers=(o_a, o_b, o_c, o_d), w_out=w_out[l])
    return xf.reshape(b, t, d)
```

```python
import functools
from typing import Callable, NamedTuple

import numpy as np
import jax
import jax.numpy as jnp
from jax import lax
from jax.experimental import pallas as pl
from jax.experimental.pallas import tpu as pltpu

F32 = jnp.float32
BF16 = jnp.bfloat16
HIGHEST = lax.Precision.HIGHEST

HEAD_DIM = 64
N_HEADS = 4
W_MIX = N_HEADS * HEAD_DIM
ROPE_THETA = 500000.0
ROPE_DIM = HEAD_DIM // 4
ROPE_HALF = ROPE_DIM // 2
GMLP_CHUNK = 128
CMP_LEN = 32
CMP_STRIDE = 16
SEL_LEN = 64
SEL_TOP = 16
WINDOW = 512
POOL_SIZES = (2, 4, 8, 16)
FFN_RES_WEIGHT = 0.5
EPS = 1e-6
NEG_INF = -1e30
SEL_FORCE = 1e3
ATTN_SCALE = HEAD_DIM ** -0.5
LOG2E = 1.4426950408889634

LANES = 128
VMEM_LIMIT = 52 * 1024 * 1024
FFN_TOKENS = 1024
FFN_MIX_TOKENS = 1024
FFN_HIDDEN = 256
FOX_PREP_ROWS = 512
FOX_TILE = 1024
NSA_TILE = 512
NSA_WINDOW_TILE = 512
ONE_LANE = HEAD_DIM


def _params(*sem):
    return pltpu.CompilerParams(dimension_semantics=sem, vmem_limit_bytes=VMEM_LIMIT)


def _dot(a, b, **kw):
    return jnp.dot(a, b, preferred_element_type=F32, **kw)


def _dot_nt(a, b):
    return lax.dot_general(a, b, (((1,), (1,)), ((), ())), preferred_element_type=F32)


def _rms(x):
    return x * lax.rsqrt(jnp.mean(x * x, axis=-1, keepdims=True) + EPS)


def _group_rms(x, gmean):
    ms = _dot((x * x).astype(BF16), gmean)
    return x * lax.rsqrt(ms + EPS)


def _rope(x, c, s_up, s_dn):
    w = x.shape[-1]
    return x * c + pltpu.roll(x, w - ROPE_HALF, axis=1) * s_up + pltpu.roll(x, ROPE_HALF, axis=1) * s_dn


_GELU_C = 0.7978845608028654


def _gelu(x):
    half = 0.5 * x
    return half + half * jnp.tanh(x * (_GELU_C + (_GELU_C * 0.044715) * (x * x)))


def _log_sigmoid(x):
    return jnp.minimum(x, 0.0) - jnp.log(1.0 + jnp.exp(-jnp.abs(x)))


def _head_column(x, h):
    col = x[:, (h // 2) * LANES:(h // 2 + 1) * LANES]
    return pltpu.roll(col, HEAD_DIM, axis=1) if h % 2 else col


def _pair_heads(even, odd):
    lane = lax.broadcasted_iota(jnp.int32, even.shape, 1)
    return jnp.where(lane < HEAD_DIM, even, pltpu.roll(odd, HEAD_DIM, axis=1))


def _ones_lane(shape):
    return jnp.where(lax.broadcasted_iota(jnp.int32, shape, 1) == ONE_LANE, 1.0, 0.0)


def _ffn_kernel(x_ref, g_ref, w1_ref, w3_ref, w2_ref, *rest):
    *mix_refs, o_ref = rest
    x = x_ref[...]
    if mix_refs:
        *mixers, wout_ref = mix_refs
        for k, ref in enumerate(mixers):
            x = x + _dot(ref[...], wout_ref[k * W_MIX:(k + 1) * W_MIX, :])
    h = (_rms(x) * g_ref[...]).astype(BF16)
    acc = jnp.zeros_like(x)
    for c in range(w1_ref.shape[1] // FFN_HIDDEN):
        cs = slice(c * FFN_HIDDEN, (c + 1) * FFN_HIDDEN)
        a = _dot(h, w1_ref[:, cs])
        b = _dot(h, w3_ref[:, cs])
        acc = acc + _dot((a * jax.nn.sigmoid(a) * b).astype(BF16), w2_ref[cs, :])
    o_ref[...] = x + FFN_RES_WEIGHT * acc


def _ffn(x, g, w1, w3, w2, mixers=(), w_out=None):
    n, d = x.shape
    tm = min(FFN_MIX_TOKENS if mixers else FFN_TOKENS, n)
    resident = lambda shape: pl.BlockSpec(shape, lambda i: (0, 0), pipeline_mode=pl.Buffered(1))
    mix_specs = [pl.BlockSpec((tm, W_MIX), lambda i: (i, 0)) for _ in mixers] + ([resident(w_out.shape)] if mixers else [])
    mix_args = list(mixers) + ([w_out.astype(BF16)] if mixers else [])
    return pl.pallas_call(
        _ffn_kernel,
        grid=(n // tm,),
        in_specs=[pl.BlockSpec((tm, d), lambda i: (i, 0)), resident((1, d)),
                  resident(w1.shape), resident(w3.shape), resident(w2.shape)] + mix_specs,
        out_specs=pl.BlockSpec((tm, d), lambda i: (i, 0)),
        out_shape=jax.ShapeDtypeStruct((n, d), F32),
        compiler_params=_params("parallel"),
        name="ffn_mix" if mixers else "ffn",
    )(x, g.reshape(1, d), w1.astype(BF16), w3.astype(BF16), w2.astype(BF16), *mix_args)


_IN_GROUPS = (768, 512, 256, 256, 256, 128, 128)
_MISC_FORGET = 0
_MISC_GATE = 4


def _inproj_kernel(x_ref, g_ref, w_ref, fqkv_ref, guv_ref, nq_ref, nkv_ref, pz_ref, kc_ref, vc_ref, misc_ref, kcvc_s):
    h = (_rms(x_ref[...]) * g_ref[...]).astype(BF16)
    off = 0
    for ref, width in zip((fqkv_ref, guv_ref, nq_ref, nkv_ref, pz_ref), _IN_GROUPS[:5]):
        ref[...] = _dot(h, w_ref[:, off:off + width])
        off += width
    tail = _dot(h, w_ref[:, off:off + 2 * LANES])
    misc_ref[...] = tail[:, LANES:]
    kcvc_s[...] = tail[:, :LANES]
    groups = x_ref.shape[0] // CMP_STRIDE
    lane = lax.broadcasted_iota(jnp.int32, (groups, LANES), 1)
    for j in range(0, CMP_STRIDE, 2):
        even = kcvc_s[pl.ds(j, groups, stride=CMP_STRIDE), :]
        odd = kcvc_s[pl.ds(j + 1, groups, stride=CMP_STRIDE), :]
        col = slice((j // 2) * LANES, (j // 2 + 1) * LANES)
        kc_ref[:, col] = jnp.where(lane < HEAD_DIM, even, pltpu.roll(odd, HEAD_DIM, axis=1))
        vc_ref[:, col] = jnp.where(lane < HEAD_DIM, pltpu.roll(even, HEAD_DIM, axis=1), odd)


def _relayout_w_in(w_in):
    o = np.cumsum((0, 256, 256, 256, 4, 256, 256, 256, 64, 64, 64, 64, 64, 64, 12, 256))
    fq, ff, gu, nq, nkc, nks, ng, pz, end = o[0], o[3], o[4], o[6], o[7], o[9], o[13], o[14], o[15]
    d = w_in.shape[0]
    pad = jnp.zeros((d, LANES - 16), w_in.dtype)
    return jnp.concatenate([
        w_in[:, fq:ff], w_in[:, gu:nq], w_in[:, nq:nkc], w_in[:, nks:ng], w_in[:, pz:end],
        w_in[:, nkc:nks], w_in[:, ff:gu], w_in[:, ng:pz], pad], axis=1)


def _inproj(x, g, w_in):
    n, d = x.shape
    tm = min(FFN_TOKENS, n)
    assert tm % (8 * CMP_STRIDE) == 0
    w = _relayout_w_in(w_in).astype(BF16)
    flat = CMP_STRIDE * HEAD_DIM
    outs = [(tm, n, wd) for wd in (768, 512, 256, 256, 256)] + [(tm // CMP_STRIDE, n // CMP_STRIDE, flat)] * 2 + [(tm, n, LANES)]
    return pl.pallas_call(
        _inproj_kernel,
        grid=(n // tm,),
        in_specs=[
            pl.BlockSpec((tm, d), lambda i: (i, 0)),
            pl.BlockSpec((1, d), lambda i: (0, 0)),
            pl.BlockSpec(w.shape, lambda i: (0, 0)),
        ],
        out_specs=[pl.BlockSpec((rows, wd), lambda i: (i, 0)) for rows, _, wd in outs],
        out_shape=[jax.ShapeDtypeStruct((total, wd), F32) for _, total, wd in outs],
        scratch_shapes=[pltpu.VMEM((tm, LANES), F32)],
        compiler_params=_params("parallel"),
        name="inproj",
    )(x, g.reshape(1, d), w)


def _bf16_pieces(x):
    p1 = x.astype(BF16).astype(F32)
    r1 = x - p1
    p2 = r1.astype(BF16).astype(F32)
    p3 = (r1 - p2).astype(BF16).astype(F32)
    return p1, p2, p3


_N_PIECES = 3
_N_EXTRA = _N_PIECES * N_HEADS
_PACK_ONE = _N_EXTRA


def _fox_placement():
    pq = np.zeros((LANES, LANES), np.float32)
    pk = np.zeros((LANES, LANES), np.float32)
    for h in range(N_HEADS):
        for p in range(_N_PIECES):
            pq[p * N_HEADS + h, HEAD_DIM + _N_PIECES * h + p] = 1.0
            pk[p * N_HEADS + h, HEAD_DIM + _N_EXTRA + _N_PIECES * h + p] = -1.0
    pq[_PACK_ONE, HEAD_DIM + _N_EXTRA:HEAD_DIM + 2 * _N_EXTRA] = 1.0
    pk[_PACK_ONE, HEAD_DIM:HEAD_DIM + _N_EXTRA] = 1.0
    return jnp.asarray(np.concatenate([pq, pk], axis=1), BF16)


def _fox_kernel(qkv_ref, misc_ref, fbt_ref, gq_ref, gk_ref, gmean_ref, pqk_ref, o_ref, q_s, k_s, v_s, o_s):
    t = qkv_ref.shape[0]
    pr = min(FOX_PREP_ROWS, t)
    gmean = gmean_ref[...]
    upper = (lax.broadcasted_iota(jnp.int32, (pr, pr), 0) <= lax.broadcasted_iota(jnp.int32, (pr, pr), 1)).astype(BF16)
    row8 = lax.broadcasted_iota(jnp.int32, (8, pr), 0)
    lane = lax.broadcasted_iota(jnp.int32, (pr, LANES), 1)
    extra = lane - HEAD_DIM
    own = [((extra >= _N_PIECES * h) & (extra < _N_PIECES * (h + 1)))
           | ((extra >= _N_EXTRA + _N_PIECES * h) & (extra < _N_EXTRA + _N_PIECES * (h + 1))) for h in range(N_HEADS)]
    ones_v = jnp.where(extra == 0, 1.0, 0.0)
    carry = jnp.zeros((8, pr), F32)
    for r in range(t // pr):
        sl = slice(r * pr, (r + 1) * pr)
        logits_t = misc_ref[sl, :].T[0:8, :]
        log_f = _log_sigmoid(logits_t + fbt_ref[...]) * LOG2E
        c = sum(_dot(piece.astype(BF16), upper) for piece in _bf16_pieces(log_f)) + carry
        carry = jnp.broadcast_to(c[:, pr - 1:pr], (8, pr))
        p1, p2, p3 = _bf16_pieces(c)
        lo = jnp.where(row8 < N_HEADS, p1, pltpu.roll(p2, N_HEADS, axis=0))
        hi = jnp.where(row8 < N_HEADS, p3, jnp.where(row8 == _PACK_ONE - 8, 1.0, 0.0))
        packed = jnp.concatenate([lo, hi, jnp.zeros((LANES - 16, pr), F32)], axis=0).T.astype(BF16)
        q_extra = _dot(packed, pqk_ref[:, :LANES])
        k_extra = _dot(packed, pqk_ref[:, LANES:])
        qn = _group_rms(qkv_ref[sl, 0:W_MIX], gmean) * gq_ref[...] * (ATTN_SCALE * LOG2E)
        kn = _group_rms(qkv_ref[sl, W_MIX:2 * W_MIX], gmean) * gk_ref[...]
        v = qkv_ref[sl, 2 * W_MIX:3 * W_MIX]
        for h in range(N_HEADS):
            q_s[h, sl, :] = jnp.where(extra < 0, _head_column(qn, h), q_extra).astype(BF16)
            k_s[h, sl, :] = jnp.where(extra < 0, _head_column(kn, h), jnp.where(own[h], k_extra, 0.0)).astype(BF16)
            v_aug = jnp.where(extra < 0, _head_column(v, h), ones_v)
            v_s[h, :, sl] = v_aug.T.astype(BF16)

    tq = min(FOX_TILE, t)
    causal = lax.broadcasted_iota(jnp.int32, (tq, tq), 0) <= lax.broadcasted_iota(jnp.int32, (tq, tq), 1)

    def one_head(h):
        for qi in range(t // tq):
            dsl = slice(qi * tq, (qi + 1) * tq)
            q = q_s[h, dsl, :]
            s_d = jnp.where(causal, _dot_nt(k_s[h, dsl, :], q), NEG_INF)
            m = jnp.max(s_d, axis=0, keepdims=True)
            if qi:
                s_o = _dot_nt(k_s[h, 0:qi * tq, :], q)
                m = jnp.maximum(m, jnp.max(s_o, axis=0, keepdims=True))
            acc = _dot(v_s[h, :, dsl], jnp.exp2(s_d - m).astype(BF16))
            if qi:
                acc = acc + _dot(v_s[h, :, 0:qi * tq], jnp.exp2(s_o - m).astype(BF16))
            o_s[h, dsl, :] = (acc / acc[ONE_LANE:ONE_LANE + 1, :]).T

    for h in range(N_HEADS):
        one_head(h)
    for hp in range(N_HEADS // 2):
        o_ref[:, hp * LANES:(hp + 1) * LANES] = _pair_heads(o_s[2 * hp], o_s[2 * hp + 1]).astype(o_ref.dtype)


def _group_mean_matrix(width):
    g = np.kron(np.eye(width // HEAD_DIM, dtype=np.float32), np.full((HEAD_DIM, HEAD_DIM), 1.0 / HEAD_DIM, np.float32))
    return jnp.asarray(g, BF16)


def _fox(fqkv, misc, f_bias, g_q, g_k, b, t):
    assert t % min(FOX_TILE, t) == 0 and t % min(FOX_PREP_ROWS, t) == 0
    pr = min(FOX_PREP_ROWS, t)
    assert _MISC_FORGET == 0 and N_HEADS <= 4 and pr % LANES == 0
    fbt = jnp.broadcast_to(jnp.zeros((8,), F32).at[:N_HEADS].set(f_bias)[:, None], (8, pr))
    gq = jnp.tile(g_q, N_HEADS).reshape(1, W_MIX)
    gk = jnp.tile(g_k, N_HEADS).reshape(1, W_MIX)
    const = _const_spec
    head_scratch = lambda dtype: pltpu.VMEM((N_HEADS, t, LANES), dtype)
    return _Part(
        _fox_kernel,
        [fqkv, misc, fbt, gq, gk, _group_mean_matrix(W_MIX), _fox_placement()],
        [pl.BlockSpec((t, 3 * W_MIX), lambda i: (i, 0)), pl.BlockSpec((t, LANES), lambda i: (i, 0)),
         const((8, pr)), const((1, W_MIX)), const((1, W_MIX)), const((W_MIX, W_MIX)), const((LANES, 2 * LANES))],
        [pl.BlockSpec((t, W_MIX), lambda i: (i, 0))],
        [jax.ShapeDtypeStruct((b * t, W_MIX), BF16)],
        (head_scratch(BF16), head_scratch(BF16), pltpu.VMEM((N_HEADS, LANES, t), BF16), head_scratch(F32)))


def _gmlp_kernel(uv_ref, gv_ref, gmean_ref, w_ref, bias_ref, o_ref):
    c = GMLP_CHUNK
    rows = lax.broadcasted_iota(jnp.int32, (c, c), 0)
    cols = lax.broadcasted_iota(jnp.int32, (c, c), 1)
    lane_group = lax.broadcasted_iota(jnp.int32, (c, W_MIX), 1) // HEAD_DIM
    w_cat = jnp.concatenate([jnp.where(cols <= rows, w_ref[g], 0.0).astype(BF16) for g in range(N_HEADS)], axis=1)
    v_all = (_group_rms(_gelu(uv_ref[:, W_MIX:2 * W_MIX]), gmean_ref[...]) * gv_ref[...]).astype(BF16)
    for r in range(uv_ref.shape[0] // c):
        sl = slice(r * c, (r + 1) * c)
        u = _gelu(uv_ref[sl, 0:W_MIX])
        v = v_all[sl]
        v_stack = jnp.concatenate([jnp.where(lane_group == g, v, jnp.zeros_like(v)) for g in range(N_HEADS)], axis=0)
        o_ref[sl, :] = (u * (bias_ref[...] + _dot(w_cat, v_stack))).astype(o_ref.dtype)


class _Part(NamedTuple):
    body: Callable
    args: list
    in_specs: list
    out_specs: list
    out_shape: list
    scratch_shapes: tuple = ()


def _const_spec(shape):
    return pl.BlockSpec(shape, lambda i: (0,) * len(shape))


def _run_parts(parts, b, name):
    n_in = [len(p.in_specs) for p in parts]
    n_out = [len(p.out_specs) for p in parts]
    n_scr = [len(p.scratch_shapes) for p in parts]

    def body(*refs):
        ins, outs, scr = refs[:sum(n_in)], refs[sum(n_in):sum(n_in) + sum(n_out)], refs[sum(n_in) + sum(n_out):]
        for p, i0, o0, s0 in zip(parts, np.cumsum([0] + n_in), np.cumsum([0] + n_out), np.cumsum([0] + n_scr)):
            p.body(*ins[i0:i0 + len(p.in_specs)], *outs[o0:o0 + len(p.out_specs)], *scr[s0:s0 + len(p.scratch_shapes)])

    outs = pl.pallas_call(
        body,
        grid=(b,),
        in_specs=[s for p in parts for s in p.in_specs],
        out_specs=[s for p in parts for s in p.out_specs],
        out_shape=[s for p in parts for s in p.out_shape],
        scratch_shapes=[s for p in parts for s in p.scratch_shapes],
        compiler_params=_params("parallel"),
        name=name,
    )(*[a for p in parts for a in p.args])
    return [outs[o0:o0 + k] for o0, k in zip(np.cumsum([0] + n_out), n_out)]


def _gmlp(guv, g_v, w_s, b_s, b, t):
    bias = jnp.repeat(b_s.T, HEAD_DIM, axis=1)
    return _Part(
        _gmlp_kernel,
        [guv, g_v.reshape(1, W_MIX), _group_mean_matrix(W_MIX), w_s, bias],
        [pl.BlockSpec((t, 2 * W_MIX), lambda i: (i, 0)), _const_spec((1, W_MIX)), _const_spec((W_MIX, W_MIX)),
         _const_spec(w_s.shape), _const_spec(bias.shape)],
        [pl.BlockSpec((t, W_MIX), lambda i: (i, 0))],
        [jax.ShapeDtypeStruct((b * t, W_MIX), BF16)])


def _pool_kernel(z_ref, inv_cnt_ref, w_ref, scale_ref, o_ref):
    t = z_ref.shape[0]
    z = z_ref[...]
    row = lax.broadcasted_iota(jnp.int32, (t, W_MIX), 0)
    lane_group = lax.broadcasted_iota(jnp.int32, (t, W_MIX), 1) // HEAD_DIM
    sums = {}
    s = z
    k = 1
    while k < max(POOL_SIZES):
        s = s + jnp.where(row >= k, pltpu.roll(s, k, axis=0), 0.0)
        k *= 2
        sums[k] = s
    win_sum = sums[POOL_SIZES[-1]]
    for g in range(len(POOL_SIZES) - 2, -1, -1):
        win_sum = jnp.where(lane_group == g, sums[POOL_SIZES[g]], win_sum)
    pooled = win_sum * inv_cnt_ref[...] - z
    o_ref[...] = (_dot(pooled.astype(BF16), w_ref[...]) * scale_ref[...]).astype(o_ref.dtype)


def _pool(pz, w_p, scale, b, t):
    w_bd = jax.scipy.linalg.block_diag(*[w_p[g] for g in range(N_HEADS)]).astype(BF16)
    win = jnp.repeat(jnp.array(POOL_SIZES, jnp.int32), HEAD_DIM)
    inv_cnt = 1.0 / jnp.minimum(jnp.arange(t)[:, None] + 1, win[None, :]).astype(F32)
    return _Part(
        _pool_kernel,
        [pz, inv_cnt, w_bd, scale.reshape(1, W_MIX)],
        [pl.BlockSpec((t, W_MIX), lambda i: (i, 0)), _const_spec((t, W_MIX)), _const_spec((W_MIX, W_MIX)),
         _const_spec((1, W_MIX))],
        [pl.BlockSpec((t, W_MIX), lambda i: (i, 0))],
        [jax.ShapeDtypeStruct((b * t, W_MIX), BF16)])


def _nsa_prep_kernel(kv_ref, kc_ref, vc_ref, posk_ref, posv_ref, kw1_ref, vw1_ref, kw2_ref, vw2_ref,
                     gkv_ref, nmask_ref, gmean_ref, c_ref, su_ref, sd_ref, gc_ref, cc_ref, csu_ref, csd_ref,
                     ks_ref, vs_ref, kw_ref, vw_ref, kcmp_ref, vcmp_ref):
    t = kv_ref.shape[0]
    x = kv_ref[...]
    normed = _group_rms(x, gmean_ref[...]) * gkv_ref[...]
    y = _rope(jnp.where(nmask_ref[...] > 0.5, normed, x), c_ref[...], su_ref[...], sd_ref[...])
    lane = lax.broadcasted_iota(jnp.int32, (t, LANES), 1)
    block = lax.broadcasted_iota(jnp.int32, (t, LANES), 0) // SEL_LEN
    key = lane < HEAD_DIM
    ks_v, kw_v = y[:, 0:LANES], y[:, LANES:2 * LANES]
    ks_ref[...] = jnp.where(key, ks_v, jnp.where(lane - HEAD_DIM == block, 1.0, 0.0)).astype(BF16)
    vs_ref[...] = jnp.where(key, pltpu.roll(ks_v, HEAD_DIM, axis=1), _ones_lane((t, LANES))).astype(BF16)
    kw_ref[...] = jnp.where(key, kw_v, 0.0).astype(BF16)
    vw_ref[...] = jnp.where(key, pltpu.roll(kw_v, HEAD_DIM, axis=1), _ones_lane((t, LANES))).astype(BF16)

    half = kc_ref.shape[1]
    nb = kc_ref.shape[0]

    def hidden(x_ref, pos_ref, w1_ref):
        top = _dot((x_ref[...] + pos_ref[:, 0:half]).astype(BF16), w1_ref[0:half, :])
        bot = _dot((x_ref[...] + pos_ref[:, half:2 * half]).astype(BF16), w1_ref[half:2 * half, :])
        return _gelu(top + pltpu.roll(bot, nb - 1, axis=0)).astype(BF16)

    kv_cmp = _dot(hidden(kc_ref, posk_ref, kw1_ref), kw2_ref[...]) + _dot(hidden(vc_ref, posv_ref, vw1_ref), vw2_ref[...])
    key = lax.broadcasted_iota(jnp.int32, kv_cmp.shape, 1) < HEAD_DIM
    normed = _group_rms(kv_cmp, gmean_ref[0:LANES, 0:LANES]) * gc_ref[...]
    y = _rope(jnp.where(key, normed, kv_cmp), cc_ref[...], csu_ref[...], csd_ref[...])
    kcmp_ref[...] = jnp.where(key, y, 0.0).astype(BF16)
    vcmp_ref[...] = jnp.where(key, pltpu.roll(y, HEAD_DIM, axis=1), _ones_lane(y.shape)).T.astype(BF16)


def _rope_tables(pos):
    inv = ROPE_THETA ** (-jnp.arange(ROPE_HALF, dtype=F32) * 2.0 / ROPE_DIM)
    ang = pos.astype(F32)[:, None] * inv[None, :]
    cos, sin = jnp.cos(ang), jnp.sin(ang)
    n = pos.shape[0]
    zero = jnp.zeros((n, ROPE_HALF), F32)
    rest0 = jnp.zeros((n, HEAD_DIM - ROPE_DIM), F32)
    c = jnp.concatenate([cos, cos, rest0 + 1.0], axis=1)
    s_up = jnp.concatenate([-sin, zero, rest0], axis=1)
    s_dn = jnp.concatenate([zero, sin, rest0], axis=1)
    return c, s_up, s_dn


def _identity_tables(n):
    return jnp.ones((n, HEAD_DIM), F32), jnp.zeros((n, HEAD_DIM), F32), jnp.zeros((n, HEAD_DIM), F32)


def _nsa_prep(nkv, kc2, vc2, g_kc, g_ks, g_kw, pos_k, k_w1, k_w2, pos_v, v_w1, v_w2, b, t):
    nb = t // CMP_STRIDE
    assert HEAD_DIM + t // SEL_LEN <= LANES
    flat = CMP_STRIDE * HEAD_DIM
    rope_t = _rope_tables(jnp.arange(t))
    iden_t = _identity_tables(t)
    tabs = [jnp.concatenate([r, i, r, i], axis=1) for r, i in zip(rope_t, iden_t)]
    rope_c = _rope_tables(jnp.arange(nb) * CMP_STRIDE + CMP_LEN - 1)
    iden_c = _identity_tables(nb)
    tabs_c = [jnp.concatenate([r, i], axis=1) for r, i in zip(rope_c, iden_c)]
    one = jnp.ones((HEAD_DIM,), F32)
    gkv = jnp.concatenate([g_ks, one, g_kw, one]).reshape(1, W_MIX)
    nmask = jnp.concatenate([one, 0 * one, one, 0 * one]).reshape(1, W_MIX)
    gc = jnp.concatenate([g_kc, one]).reshape(1, LANES)
    zpad = jnp.zeros_like(k_w2)
    kw2 = jnp.concatenate([k_w2, zpad], axis=1).astype(BF16)
    vw2 = jnp.concatenate([zpad, v_w2], axis=1).astype(BF16)
    const = _const_spec
    seq = lambda width: pl.BlockSpec((t, width), lambda i: (i, 0))
    cmp_in = pl.BlockSpec((nb, flat), lambda i: (i, 0))
    cmp_out = pl.BlockSpec((nb, LANES), lambda i: (i, 0))
    return _Part(
        _nsa_prep_kernel,
        [nkv, kc2, vc2, pos_k.reshape(1, 2 * flat), pos_v.reshape(1, 2 * flat),
         k_w1.astype(BF16), v_w1.astype(BF16), kw2, vw2, gkv, nmask, _group_mean_matrix(W_MIX), *tabs, gc, *tabs_c],
        [seq(W_MIX), cmp_in, cmp_in, const((1, 2 * flat)), const((1, 2 * flat)),
         const(k_w1.shape), const(v_w1.shape), const(kw2.shape), const(vw2.shape),
         const((1, W_MIX)), const((1, W_MIX)), const((W_MIX, W_MIX)),
         const((t, W_MIX)), const((t, W_MIX)), const((t, W_MIX)),
         const((1, LANES)), const((nb, LANES)), const((nb, LANES)), const((nb, LANES))],
        [seq(LANES)] * 4 + [cmp_out, pl.BlockSpec((LANES, nb), lambda i: (i, 0))],
        [jax.ShapeDtypeStruct((b * t, LANES), BF16)] * 4
        + [jax.ShapeDtypeStruct((b * nb, LANES), BF16), jax.ShapeDtypeStruct((b * LANES, nb), BF16)])


def _nsa_kernel(*refs, n_top):
    q_ref, ks_ref = refs[0], refs[11]
    variants = [functools.partial(_nsa_tile, *refs, n_top=n_top, qi=k) for k in range(ks_ref.shape[0] // q_ref.shape[0])]
    lax.switch(pl.program_id(1), variants)


def _nsa_tile(q_ref, misc_ref, gb_ref, gq_ref, gmean_ref, c_ref, su_ref, sd_ref, ovt_ref,
              kcmp_ref, vcmp_ref, ks_ref, vs_ref, kw_ref, vw_ref, o_ref, *, n_top, qi):
    tq = q_ref.shape[0]
    t0 = qi * tq
    nb = kcmp_ref.shape[0]
    nsel = ks_ref.shape[0] // SEL_LEN
    rows = N_HEADS * tq

    qn = _rope(_group_rms(q_ref[...], gmean_ref[...]) * gq_ref[...], c_ref[...], su_ref[...], sd_ref[...]) * (ATTN_SCALE * LOG2E)
    q_cols = [_head_column(qn, h) for h in range(N_HEADS)]
    lane_q = lax.broadcasted_iota(jnp.int32, (tq, LANES), 1)

    q0 = jnp.concatenate([jnp.where(lane_q < HEAD_DIM, col, 0.0) for col in q_cols], axis=0).astype(BF16)
    cmp_end = lax.broadcasted_iota(jnp.int32, (nb, rows), 0) * CMP_STRIDE + (CMP_LEN - 1)
    qpos_c = (lax.broadcasted_iota(jnp.int32, (nb, rows), 1) & (tq - 1)) + t0
    m_cmp = cmp_end <= qpos_c
    s = jnp.where(m_cmp, _dot_nt(kcmp_ref[...], q0), NEG_INF)
    e = jnp.where(m_cmp, jnp.exp2(s - jnp.max(s, axis=0, keepdims=True)), 0.0)
    denom = jnp.sum(e, axis=0, keepdims=True)
    inv = 1.0 / jnp.where(denom > 0.0, denom, 1.0)
    p_cmp = e * inv
    o_cmp = (_dot(vcmp_ref[...], e.astype(BF16)) * inv).T

    nsp = -(-nsel // 8) * 8
    blk = lax.broadcasted_iota(jnp.int32, (nsp, tq), 0)
    cur = (lax.broadcasted_iota(jnp.int32, (nsp, tq), 1) + t0) // SEL_LEN
    valid = blk <= cur
    n_reachable = (t0 + tq - 1) // SEL_LEN + 1
    if n_reachable <= n_top:
        keep = valid
    else:
        p_sum = p_cmp[:, 0:tq] + p_cmp[:, tq:2 * tq] + p_cmp[:, 2 * tq:3 * tq] + p_cmp[:, 3 * tq:4 * tq]
        imp = _dot(ovt_ref[0:nsp, :], p_sum, precision=HIGHEST)
        forced = ((blk == 0) | (blk == cur) | (blk == cur - 1)).astype(F32)
        imp = jnp.where(valid, imp + SEL_FORCE * forced, NEG_INF)
        cur_row = (lax.broadcasted_iota(jnp.int32, (1, tq), 1) + t0) // SEL_LEN
        outranked = jnp.zeros((nsp, tq), F32)
        for j in range(min(nsel, n_reachable)):
            rival = imp[j:j + 1, :]
            wins = (rival > imp) | ((rival == imp) & (blk > j))
            outranked = outranked + jnp.where(wins & (cur_row >= j), 1.0, 0.0)
        keep = (outranked < n_top) & valid
    bias_t = jnp.where(keep | (blk >= nsel), 0.0, NEG_INF)
    if nsp < LANES:
        bias_t = jnp.concatenate([bias_t, jnp.zeros((LANES - nsp, tq), F32)], axis=0)
    bias = jnp.concatenate([bias_t[:, c * LANES:(c + 1) * LANES].T for c in range(tq // LANES)], axis=0)
    bias = pltpu.roll(bias, HEAD_DIM, axis=1)
    q = jnp.concatenate([jnp.where(lane_q < HEAD_DIM, col, bias) for col in q_cols], axis=0).astype(BF16)

    trow = lax.broadcasted_iota(jnp.int32, (rows, tq), 0) & (tq - 1)
    tcol = lax.broadcasted_iota(jnp.int32, (rows, tq), 1)
    causal_bias = jnp.where(tcol <= trow, 0.0, NEG_INF)

    def tile(j):
        return slice(j * tq, (j + 1) * tq)

    def sel_step(kj, extra_bias, st):
        m, acc = st
        s = _dot_nt(q, ks_ref[tile(kj), :])
        if extra_bias is not None:
            s = s + extra_bias
        m_new = jnp.maximum(m, jnp.max(s, axis=-1, keepdims=True))
        acc = jnp.exp2(m - m_new) * acc + _dot(jnp.exp2(s - m_new).astype(BF16), vs_ref[tile(kj), :])
        return m_new, acc

    st = (jnp.full((rows, 1), NEG_INF, F32), jnp.zeros((rows, LANES), F32))
    for kj in range(qi):
        st = sel_step(kj, None, st)
    _, acc = sel_step(qi, causal_bias, st)
    o_sel = acc / acc[:, ONE_LANE:ONE_LANE + 1]

    wt = min(NSA_WINDOW_TILE, tq)
    span = WINDOW // wt
    wrow = lax.broadcasted_iota(jnp.int32, (N_HEADS * wt, wt), 0) & (wt - 1)
    wcol = lax.broadcasted_iota(jnp.int32, (N_HEADS * wt, wt), 1)
    o_win = []
    for sub in range(tq // wt):
        q_sub = jnp.concatenate([q[h * tq + sub * wt:h * tq + (sub + 1) * wt] for h in range(N_HEADS)], axis=0)
        diag = qi * (tq // wt) + sub
        parts = []
        for d in range(min(span, diag), -1, -1):
            ksl = slice((diag - d) * wt, (diag - d + 1) * wt)
            s = _dot_nt(q_sub, kw_ref[ksl, :])
            if d == 0:
                s = s + jnp.where(wcol <= wrow, 0.0, NEG_INF)
            elif d == span:
                s = s + jnp.where(wcol > wrow, 0.0, NEG_INF)
            parts.append((s, ksl))
        m = functools.reduce(jnp.maximum, [jnp.max(s, axis=-1, keepdims=True) for s, _ in parts])
        acc = sum(_dot(jnp.exp2(s - m).astype(BF16), vw_ref[ksl, :]) for s, ksl in parts)
        o_win.append(acc / acc[:, ONE_LANE:ONE_LANE + 1])

    gate = jax.nn.sigmoid(misc_ref[...] + gb_ref[...])
    outs = []
    for h in range(N_HEADS):
        hr = slice(h * tq, (h + 1) * tq)
        g0 = _MISC_GATE + 3 * h
        o_win_h = jnp.concatenate([part[h * wt:(h + 1) * wt] for part in o_win], axis=0)
        outs.append(gate[:, g0:g0 + 1] * o_cmp[hr] + gate[:, g0 + 1:g0 + 2] * o_sel[hr] + gate[:, g0 + 2:g0 + 3] * o_win_h)
    for hp in range(N_HEADS // 2):
        o_ref[:, hp * LANES:(hp + 1) * LANES] = _pair_heads(outs[2 * hp], outs[2 * hp + 1]).astype(o_ref.dtype)


def _cmp_to_sel_overlap_t(nb, nsel):
    nc = nb - 1
    cs = np.arange(nc) * CMP_STRIDE
    ss = np.arange(nsel) * SEL_LEN
    ov = np.clip(np.minimum(cs[:, None] + CMP_LEN, ss[None, :] + SEL_LEN) - np.maximum(cs[:, None], ss[None, :]), 0, None)
    out = np.zeros((LANES, nb), np.float32)
    out[:nsel, :nc] = (ov / CMP_LEN).T
    return jnp.asarray(out)


def _nsa(nq, misc, gate_b, g_q, ks, vs, kw, vw, kcmp, vcmp, b, t):
    tq = min(NSA_TILE, t)
    nb = t // CMP_STRIDE
    nsel = t // SEL_LEN
    wt = min(NSA_WINDOW_TILE, tq)
    assert HEAD_DIM + nsel <= LANES and nb % 8 == 0 and tq & (tq - 1) == 0 and tq % LANES == 0
    assert WINDOW % wt == 0 and tq % wt == 0 and wt & (wt - 1) == 0
    gb = jnp.zeros((1, LANES), F32).at[0, _MISC_GATE:_MISC_GATE + 3 * N_HEADS].set(gate_b)
    gq = jnp.tile(g_q, N_HEADS).reshape(1, W_MIX)
    tabs = [jnp.tile(x, (1, N_HEADS)) for x in _rope_tables(jnp.arange(t))]
    nq_tiles = t // tq
    const = lambda shape: pl.BlockSpec(shape, lambda i, j: (0,) * len(shape))
    qtile = lambda width: pl.BlockSpec((tq, width), lambda i, j: (i * nq_tiles + j, 0))
    ptile = pl.BlockSpec((tq, W_MIX), lambda i, j: (j, 0))
    seq = pl.BlockSpec((t, LANES), lambda i, j: (i, 0))
    cmp = pl.BlockSpec((nb, LANES), lambda i, j: (i, 0))
    return pl.pallas_call(
        functools.partial(_nsa_kernel, n_top=min(SEL_TOP, nsel)),
        grid=(b, nq_tiles),
        in_specs=[qtile(W_MIX), qtile(LANES), const((1, LANES)), const((1, W_MIX)), const((W_MIX, W_MIX)),
                  ptile, ptile, ptile, const((LANES, nb)), cmp, pl.BlockSpec((LANES, nb), lambda i, j: (i, 0)),
                  seq, seq, seq, seq],
        out_specs=qtile(W_MIX),
        out_shape=jax.ShapeDtypeStruct((b * t, W_MIX), BF16),
        compiler_params=_params("parallel", "arbitrary"),
        name="nsa",
    )(nq, misc, gb, gq, _group_mean_matrix(W_MIX), *tabs, _cmp_to_sel_overlap_t(nb, nsel),
      kcmp, vcmp, ks, vs, kw, vw)


def kernel(x, ffn1_norm, ffn1_w1, ffn1_w3, ffn1_w2, mix_norm, w_in, w_out, fox_f_bias, fox_q_norm, fox_k_norm, gmlp_v_norm, gmlp_w_s, gmlp_b_s, nsa_q_norm, nsa_kc_norm, nsa_ks_norm, nsa_kw_norm, nsa_cmp_pos_k, nsa_cmp_k_w1, nsa_cmp_k_w2, nsa_cmp_pos_v, nsa_cmp_v_w1, nsa_cmp_v_w2, nsa_gate_bias, pool_w, pool_scale, ffn2_norm, ffn2_w1, ffn2_w3, ffn2_w2):
    b, t, d = x.shape
    assert t % GMLP_CHUNK == 0 and t >= CMP_LEN
    n = b * t
    xf = x.reshape(n, d)
    for l in range(ffn1_norm.shape[0]):
        xf = _ffn(xf, ffn1_norm[l], ffn1_w1[l], ffn1_w3[l], ffn1_w2[l])
        fqkv, guv, nq, nkv, pz, kc, vc, misc = _inproj(xf, mix_norm[l], w_in[l])
        ((o_a,),) = _run_parts([_fox(fqkv, misc, fox_f_bias[l], fox_q_norm[l], fox_k_norm[l], b, t)], b, "fox")
        (o_b,), (o_d,), (ks, vs, kw, vw, kcmp, vcmp) = _run_parts([
            _gmlp(guv, gmlp_v_norm[l], gmlp_w_s[l], gmlp_b_s[l], b, t),
            _pool(pz, pool_w[l], pool_scale[l], b, t),
            _nsa_prep(nkv, kc, vc, nsa_kc_norm[l], nsa_ks_norm[l], nsa_kw_norm[l],
                      nsa_cmp_pos_k[l], nsa_cmp_k_w1[l], nsa_cmp_k_w2[l],
                      nsa_cmp_pos_v[l], nsa_cmp_v_w1[l], nsa_cmp_v_w2[l], b, t)], b, "seq_mixers")
        o_c = _nsa(nq, misc, nsa_gate_bias[l], nsa_q_norm[l], ks, vs, kw, vw, kcmp, vcmp, b, t)
        xf = _ffn(xf, ffn2_norm[l], ffn2_w1[l], ffn2_w3[l], ffn2_w2[l], mixers=(o_a, o_b, o_c, o_d), w_out=w_out[l])
    return xf.reshape(b, t, d)
```

```python
import functools
from typing import Callable, NamedTuple

import numpy as np
import jax
import jax.numpy as jnp
from jax import lax
from jax.experimental import pallas as pl
from jax.experimental.pallas import tpu as pltpu

F32 = jnp.float32
BF16 = jnp.bfloat16
HIGHEST = lax.Precision.HIGHEST

HEAD_DIM = 64
N_HEADS = 4
W_MIX = N_HEADS * HEAD_DIM
ROPE_THETA = 500000.0
ROPE_DIM = HEAD_DIM // 4
ROPE_HALF = ROPE_DIM // 2
GMLP_CHUNK = 128
CMP_LEN = 32
CMP_STRIDE = 16
SEL_LEN = 64
SEL_TOP = 16
WINDOW = 512
POOL_SIZES = (2, 4, 8, 16)
FFN_RES_WEIGHT = 0.5
EPS = 1e-6
NEG_INF = -1e30
SEL_FORCE = 1e3
ATTN_SCALE = HEAD_DIM ** -0.5
LOG2E = 1.4426950408889634

LANES = 128
VMEM_LIMIT = 52 * 1024 * 1024
FFN_TOKENS = 1024
FFN_MIX_TOKENS = 1024
FFN_HIDDEN = 256
FOX_PREP_ROWS = 512
FOX_TILE = 1024
NSA_TILE = 512
NSA_WINDOW_TILE = 512
NSA_TILES_PER_STEP = 2
ONE_LANE = HEAD_DIM


def _params(*sem):
    return pltpu.CompilerParams(dimension_semantics=sem, vmem_limit_bytes=VMEM_LIMIT)


def _dot(a, b, **kw):
    return jnp.dot(a, b, preferred_element_type=F32, **kw)


def _dot_nt(a, b):
    return lax.dot_general(a, b, (((1,), (1,)), ((), ())), preferred_element_type=F32)


def _rms(x):
    return x * lax.rsqrt(jnp.mean(x * x, axis=-1, keepdims=True) + EPS)


def _group_rms(x, gmean):
    ms = _dot((x * x).astype(BF16), gmean)
    return x * lax.rsqrt(ms + EPS)


def _rope(x, c, s_up, s_dn):
    w = x.shape[-1]
    return x * c + pltpu.roll(x, w - ROPE_HALF, axis=1) * s_up + pltpu.roll(x, ROPE_HALF, axis=1) * s_dn


_GELU_C = 0.7978845608028654


def _gelu(x):
    half = 0.5 * x
    return half + half * jnp.tanh(x * (_GELU_C + (_GELU_C * 0.044715) * (x * x)))


def _log_sigmoid(x):
    return jnp.minimum(x, 0.0) - jnp.log(1.0 + jnp.exp(-jnp.abs(x)))


def _head_column(x, h):
    col = x[:, (h // 2) * LANES:(h // 2 + 1) * LANES]
    return pltpu.roll(col, HEAD_DIM, axis=1) if h % 2 else col


def _pair_heads(even, odd):
    lane = lax.broadcasted_iota(jnp.int32, even.shape, 1)
    return jnp.where(lane < HEAD_DIM, even, pltpu.roll(odd, HEAD_DIM, axis=1))


def _ones_lane(shape):
    return jnp.where(lax.broadcasted_iota(jnp.int32, shape, 1) == ONE_LANE, 1.0, 0.0)


def _ffn_kernel(x_ref, g_ref, w1_ref, w3_ref, w2_ref, *rest):
    *mix_refs, o_ref = rest
    x = x_ref[...]
    if mix_refs:
        *mixers, wout_ref = mix_refs
        for k, ref in enumerate(mixers):
            x = x + _dot(ref[...], wout_ref[k * W_MIX:(k + 1) * W_MIX, :])
    h = (_rms(x) * g_ref[...]).astype(BF16)
    acc = jnp.zeros_like(x)
    for c in range(w1_ref.shape[1] // FFN_HIDDEN):
        cs = slice(c * FFN_HIDDEN, (c + 1) * FFN_HIDDEN)
        a = _dot(h, w1_ref[:, cs])
        b = _dot(h, w3_ref[:, cs])
        acc = acc + _dot((a * jax.nn.sigmoid(a) * b).astype(BF16), w2_ref[cs, :])
    o_ref[...] = x + FFN_RES_WEIGHT * acc


def _ffn(x, g, w1, w3, w2, mixers=(), w_out=None):
    n, d = x.shape
    tm = min(FFN_MIX_TOKENS if mixers else FFN_TOKENS, n)
    resident = lambda shape: pl.BlockSpec(shape, lambda i: (0, 0), pipeline_mode=pl.Buffered(1))
    mix_specs = [pl.BlockSpec((tm, W_MIX), lambda i: (i, 0)) for _ in mixers] + ([resident(w_out.shape)] if mixers else [])
    mix_args = list(mixers) + ([w_out.astype(BF16)] if mixers else [])
    return pl.pallas_call(
        _ffn_kernel,
        grid=(n // tm,),
        in_specs=[pl.BlockSpec((tm, d), lambda i: (i, 0)), resident((1, d)),
                  resident(w1.shape), resident(w3.shape), resident(w2.shape)] + mix_specs,
        out_specs=pl.BlockSpec((tm, d), lambda i: (i, 0)),
        out_shape=jax.ShapeDtypeStruct((n, d), F32),
        compiler_params=_params("parallel"),
        name="ffn_mix" if mixers else "ffn",
    )(x, g.reshape(1, d), w1.astype(BF16), w3.astype(BF16), w2.astype(BF16), *mix_args)


_IN_GROUPS = (768, 512, 256, 256, 256, 128, 128)
_MISC_FORGET = 0
_MISC_GATE = 4


def _inproj_kernel(x_ref, g_ref, w_ref, fqkv_ref, guv_ref, nq_ref, nkv_ref, pz_ref, kc_ref, vc_ref, misc_ref, kcvc_s):
    h = (_rms(x_ref[...]) * g_ref[...]).astype(BF16)
    off = 0
    for ref, width in zip((fqkv_ref, guv_ref, nq_ref, nkv_ref, pz_ref), _IN_GROUPS[:5]):
        ref[...] = _dot(h, w_ref[:, off:off + width])
        off += width
    tail = _dot(h, w_ref[:, off:off + 2 * LANES])
    misc_ref[...] = tail[:, LANES:]
    kcvc_s[...] = tail[:, :LANES]
    groups = x_ref.shape[0] // CMP_STRIDE
    lane = lax.broadcasted_iota(jnp.int32, (groups, LANES), 1)
    for j in range(0, CMP_STRIDE, 2):
        even = kcvc_s[pl.ds(j, groups, stride=CMP_STRIDE), :]
        odd = kcvc_s[pl.ds(j + 1, groups, stride=CMP_STRIDE), :]
        col = slice((j // 2) * LANES, (j // 2 + 1) * LANES)
        kc_ref[:, col] = jnp.where(lane < HEAD_DIM, even, pltpu.roll(odd, HEAD_DIM, axis=1))
        vc_ref[:, col] = jnp.where(lane < HEAD_DIM, pltpu.roll(even, HEAD_DIM, axis=1), odd)


def _relayout_w_in(w_in):
    o = np.cumsum((0, 256, 256, 256, 4, 256, 256, 256, 64, 64, 64, 64, 64, 64, 12, 256))
    fq, ff, gu, nq, nkc, nks, ng, pz, end = o[0], o[3], o[4], o[6], o[7], o[9], o[13], o[14], o[15]
    d = w_in.shape[0]
    pad = jnp.zeros((d, LANES - 16), w_in.dtype)
    return jnp.concatenate([
        w_in[:, fq:ff], w_in[:, gu:nq], w_in[:, nq:nkc], w_in[:, nks:ng], w_in[:, pz:end],
        w_in[:, nkc:nks], w_in[:, ff:gu], w_in[:, ng:pz], pad], axis=1)


def _inproj(x, g, w_in):
    n, d = x.shape
    tm = min(FFN_TOKENS, n)
    assert tm % (8 * CMP_STRIDE) == 0
    w = _relayout_w_in(w_in).astype(BF16)
    flat = CMP_STRIDE * HEAD_DIM
    outs = [(tm, n, wd) for wd in (768, 512, 256, 256, 256)] + [(tm // CMP_STRIDE, n // CMP_STRIDE, flat)] * 2 + [(tm, n, LANES)]
    return pl.pallas_call(
        _inproj_kernel,
        grid=(n // tm,),
        in_specs=[
            pl.BlockSpec((tm, d), lambda i: (i, 0)),
            pl.BlockSpec((1, d), lambda i: (0, 0)),
            pl.BlockSpec(w.shape, lambda i: (0, 0)),
        ],
        out_specs=[pl.BlockSpec((rows, wd), lambda i: (i, 0)) for rows, _, wd in outs],
        out_shape=[jax.ShapeDtypeStruct((total, wd), F32) for _, total, wd in outs],
        scratch_shapes=[pltpu.VMEM((tm, LANES), F32)],
        compiler_params=_params("parallel"),
        name="inproj",
    )(x, g.reshape(1, d), w)


def _bf16_pieces(x):
    p1 = x.astype(BF16).astype(F32)
    r1 = x - p1
    p2 = r1.astype(BF16).astype(F32)
    p3 = (r1 - p2).astype(BF16).astype(F32)
    return p1, p2, p3


_N_PIECES = 3
_N_EXTRA = _N_PIECES * N_HEADS
_PACK_ONE = _N_EXTRA


def _fox_placement():
    pq = np.zeros((LANES, LANES), np.float32)
    pk = np.zeros((LANES, LANES), np.float32)
    for h in range(N_HEADS):
        for p in range(_N_PIECES):
            pq[p * N_HEADS + h, HEAD_DIM + _N_PIECES * h + p] = 1.0
            pk[p * N_HEADS + h, HEAD_DIM + _N_EXTRA + _N_PIECES * h + p] = -1.0
    pq[_PACK_ONE, HEAD_DIM + _N_EXTRA:HEAD_DIM + 2 * _N_EXTRA] = 1.0
    pk[_PACK_ONE, HEAD_DIM:HEAD_DIM + _N_EXTRA] = 1.0
    return jnp.asarray(np.concatenate([pq, pk], axis=1), BF16)


def _fox_kernel(qkv_ref, misc_ref, fbt_ref, gq_ref, gk_ref, gmean_ref, pqk_ref, o_ref, q_s, k_s, v_s, o_s):
    t = qkv_ref.shape[0]
    pr = min(FOX_PREP_ROWS, t)
    gmean = gmean_ref[...]
    upper = (lax.broadcasted_iota(jnp.int32, (pr, pr), 0) <= lax.broadcasted_iota(jnp.int32, (pr, pr), 1)).astype(BF16)
    row8 = lax.broadcasted_iota(jnp.int32, (8, pr), 0)
    lane = lax.broadcasted_iota(jnp.int32, (pr, LANES), 1)
    extra = lane - HEAD_DIM
    own = [((extra >= _N_PIECES * h) & (extra < _N_PIECES * (h + 1)))
           | ((extra >= _N_EXTRA + _N_PIECES * h) & (extra < _N_EXTRA + _N_PIECES * (h + 1))) for h in range(N_HEADS)]
    ones_v = jnp.where(extra == 0, 1.0, 0.0)
    carry = jnp.zeros((8, pr), F32)
    for r in range(t // pr):
        sl = slice(r * pr, (r + 1) * pr)
        logits_t = misc_ref[sl, :].T[0:8, :]
        log_f = _log_sigmoid(logits_t + fbt_ref[...]) * LOG2E
        c = sum(_dot(piece.astype(BF16), upper) for piece in _bf16_pieces(log_f)) + carry
        carry = jnp.broadcast_to(c[:, pr - 1:pr], (8, pr))
        p1, p2, p3 = _bf16_pieces(c)
        lo = jnp.where(row8 < N_HEADS, p1, pltpu.roll(p2, N_HEADS, axis=0))
        hi = jnp.where(row8 < N_HEADS, p3, jnp.where(row8 == _PACK_ONE - 8, 1.0, 0.0))
        packed = jnp.concatenate([lo, hi, jnp.zeros((LANES - 16, pr), F32)], axis=0).T.astype(BF16)
        q_extra = _dot(packed, pqk_ref[:, :LANES])
        k_extra = _dot(packed, pqk_ref[:, LANES:])
        qn = _group_rms(qkv_ref[sl, 0:W_MIX], gmean) * gq_ref[...] * (ATTN_SCALE * LOG2E)
        kn = _group_rms(qkv_ref[sl, W_MIX:2 * W_MIX], gmean) * gk_ref[...]
        v = qkv_ref[sl, 2 * W_MIX:3 * W_MIX]
        for h in range(N_HEADS):
            q_s[h, sl, :] = jnp.where(extra < 0, _head_column(qn, h), q_extra).astype(BF16)
            k_s[h, sl, :] = jnp.where(extra < 0, _head_column(kn, h), jnp.where(own[h], k_extra, 0.0)).astype(BF16)
            v_aug = jnp.where(extra < 0, _head_column(v, h), ones_v)
            v_s[h, :, sl] = v_aug.T.astype(BF16)

    tq = min(FOX_TILE, t)
    causal = lax.broadcasted_iota(jnp.int32, (tq, tq), 0) <= lax.broadcasted_iota(jnp.int32, (tq, tq), 1)

    def one_head(h):
        for qi in range(t // tq):
            dsl = slice(qi * tq, (qi + 1) * tq)
            q = q_s[h, dsl, :]
            s_d = jnp.where(causal, _dot_nt(k_s[h, dsl, :], q), NEG_INF)
            m = jnp.max(s_d, axis=0, keepdims=True)
            if qi:
                s_o = _dot_nt(k_s[h, 0:qi * tq, :], q)
                m = jnp.maximum(m, jnp.max(s_o, axis=0, keepdims=True))
            acc = _dot(v_s[h, :, dsl], jnp.exp2(s_d - m).astype(BF16))
            if qi:
                acc = acc + _dot(v_s[h, :, 0:qi * tq], jnp.exp2(s_o - m).astype(BF16))
            o_s[h, dsl, :] = (acc / acc[ONE_LANE:ONE_LANE + 1, :]).T

    for h in range(N_HEADS):
        one_head(h)
    for hp in range(N_HEADS // 2):
        o_ref[:, hp * LANES:(hp + 1) * LANES] = _pair_heads(o_s[2 * hp], o_s[2 * hp + 1]).astype(o_ref.dtype)


def _group_mean_matrix(width):
    g = np.kron(np.eye(width // HEAD_DIM, dtype=np.float32), np.full((HEAD_DIM, HEAD_DIM), 1.0 / HEAD_DIM, np.float32))
    return jnp.asarray(g, BF16)


def _fox(fqkv, misc, f_bias, g_q, g_k, b, t):
    assert t % min(FOX_TILE, t) == 0 and t % min(FOX_PREP_ROWS, t) == 0
    pr = min(FOX_PREP_ROWS, t)
    assert _MISC_FORGET == 0 and N_HEADS <= 4 and pr % LANES == 0
    fbt = jnp.broadcast_to(jnp.zeros((8,), F32).at[:N_HEADS].set(f_bias)[:, None], (8, pr))
    gq = jnp.tile(g_q, N_HEADS).reshape(1, W_MIX)
    gk = jnp.tile(g_k, N_HEADS).reshape(1, W_MIX)
    const = _const_spec
    head_scratch = lambda dtype: pltpu.VMEM((N_HEADS, t, LANES), dtype)
    return _Part(
        _fox_kernel,
        [fqkv, misc, fbt, gq, gk, _group_mean_matrix(W_MIX), _fox_placement()],
        [pl.BlockSpec((t, 3 * W_MIX), lambda i: (i, 0)), pl.BlockSpec((t, LANES), lambda i: (i, 0)),
         const((8, pr)), const((1, W_MIX)), const((1, W_MIX)), const((W_MIX, W_MIX)), const((LANES, 2 * LANES))],
        [pl.BlockSpec((t, W_MIX), lambda i: (i, 0))],
        [jax.ShapeDtypeStruct((b * t, W_MIX), BF16)],
        (head_scratch(BF16), head_scratch(BF16), pltpu.VMEM((N_HEADS, LANES, t), BF16), head_scratch(F32)))


def _gmlp_kernel(uv_ref, gv_ref, gmean_ref, w_ref, bias_ref, o_ref):
    c = GMLP_CHUNK
    rows = lax.broadcasted_iota(jnp.int32, (c, c), 0)
    cols = lax.broadcasted_iota(jnp.int32, (c, c), 1)
    lane_group = lax.broadcasted_iota(jnp.int32, (c, W_MIX), 1) // HEAD_DIM
    w_cat = jnp.concatenate([jnp.where(cols <= rows, w_ref[g], 0.0).astype(BF16) for g in range(N_HEADS)], axis=1)
    v_all = (_group_rms(_gelu(uv_ref[:, W_MIX:2 * W_MIX]), gmean_ref[...]) * gv_ref[...]).astype(BF16)
    for r in range(uv_ref.shape[0] // c):
        sl = slice(r * c, (r + 1) * c)
        u = _gelu(uv_ref[sl, 0:W_MIX])
        v = v_all[sl]
        v_stack = jnp.concatenate([jnp.where(lane_group == g, v, jnp.zeros_like(v)) for g in range(N_HEADS)], axis=0)
        o_ref[sl, :] = (u * (bias_ref[...] + _dot(w_cat, v_stack))).astype(o_ref.dtype)


class _Part(NamedTuple):
    body: Callable
    args: list
    in_specs: list
    out_specs: list
    out_shape: list
    scratch_shapes: tuple = ()


def _const_spec(shape):
    return pl.BlockSpec(shape, lambda i: (0,) * len(shape))


def _run_parts(parts, b, name):
    n_in = [len(p.in_specs) for p in parts]
    n_out = [len(p.out_specs) for p in parts]
    n_scr = [len(p.scratch_shapes) for p in parts]

    def body(*refs):
        ins, outs, scr = refs[:sum(n_in)], refs[sum(n_in):sum(n_in) + sum(n_out)], refs[sum(n_in) + sum(n_out):]
        for p, i0, o0, s0 in zip(parts, np.cumsum([0] + n_in), np.cumsum([0] + n_out), np.cumsum([0] + n_scr)):
            p.body(*ins[i0:i0 + len(p.in_specs)], *outs[o0:o0 + len(p.out_specs)], *scr[s0:s0 + len(p.scratch_shapes)])

    outs = pl.pallas_call(
        body,
        grid=(b,),
        in_specs=[s for p in parts for s in p.in_specs],
        out_specs=[s for p in parts for s in p.out_specs],
        out_shape=[s for p in parts for s in p.out_shape],
        scratch_shapes=[s for p in parts for s in p.scratch_shapes],
        compiler_params=_params("parallel"),
        name=name,
    )(*[a for p in parts for a in p.args])
    return [outs[o0:o0 + k] for o0, k in zip(np.cumsum([0] + n_out), n_out)]


def _gmlp(guv, g_v, w_s, b_s, b, t):
    bias = jnp.repeat(b_s.T, HEAD_DIM, axis=1)
    return _Part(
        _gmlp_kernel,
        [guv, g_v.reshape(1, W_MIX), _group_mean_matrix(W_MIX), w_s, bias],
        [pl.BlockSpec((t, 2 * W_MIX), lambda i: (i, 0)), _const_spec((1, W_MIX)), _const_spec((W_MIX, W_MIX)),
         _const_spec(w_s.shape), _const_spec(bias.shape)],
        [pl.BlockSpec((t, W_MIX), lambda i: (i, 0))],
        [jax.ShapeDtypeStruct((b * t, W_MIX), BF16)])


def _pool_kernel(z_ref, inv_cnt_ref, w_ref, scale_ref, o_ref):
    t = z_ref.shape[0]
    z = z_ref[...]
    row = lax.broadcasted_iota(jnp.int32, (t, W_MIX), 0)
    lane_group = lax.broadcasted_iota(jnp.int32, (t, W_MIX), 1) // HEAD_DIM
    sums = {}
    s = z
    k = 1
    while k < max(POOL_SIZES):
        s = s + jnp.where(row >= k, pltpu.roll(s, k, axis=0), 0.0)
        k *= 2
        sums[k] = s
    win_sum = sums[POOL_SIZES[-1]]
    for g in range(len(POOL_SIZES) - 2, -1, -1):
        win_sum = jnp.where(lane_group == g, sums[POOL_SIZES[g]], win_sum)
    pooled = win_sum * inv_cnt_ref[...] - z
    o_ref[...] = (_dot(pooled.astype(BF16), w_ref[...]) * scale_ref[...]).astype(o_ref.dtype)


def _pool(pz, w_p, scale, b, t):
    w_bd = jax.scipy.linalg.block_diag(*[w_p[g] for g in range(N_HEADS)]).astype(BF16)
    win = jnp.repeat(jnp.array(POOL_SIZES, jnp.int32), HEAD_DIM)
    inv_cnt = 1.0 / jnp.minimum(jnp.arange(t)[:, None] + 1, win[None, :]).astype(F32)
    return _Part(
        _pool_kernel,
        [pz, inv_cnt, w_bd, scale.reshape(1, W_MIX)],
        [pl.BlockSpec((t, W_MIX), lambda i: (i, 0)), _const_spec((t, W_MIX)), _const_spec((W_MIX, W_MIX)),
         _const_spec((1, W_MIX))],
        [pl.BlockSpec((t, W_MIX), lambda i: (i, 0))],
        [jax.ShapeDtypeStruct((b * t, W_MIX), BF16)])


def _nsa_prep_kernel(kv_ref, kc_ref, vc_ref, posk_ref, posv_ref, kw1_ref, vw1_ref, kw2_ref, vw2_ref,
                     gkv_ref, nmask_ref, gmean_ref, c_ref, su_ref, sd_ref, gc_ref, cc_ref, csu_ref, csd_ref,
                     ks_ref, vs_ref, kw_ref, vw_ref, kcmp_ref, vcmp_ref):
    t = kv_ref.shape[0]
    x = kv_ref[...]
    normed = _group_rms(x, gmean_ref[...]) * gkv_ref[...]
    y = _rope(jnp.where(nmask_ref[...] > 0.5, normed, x), c_ref[...], su_ref[...], sd_ref[...])
    lane = lax.broadcasted_iota(jnp.int32, (t, LANES), 1)
    block = lax.broadcasted_iota(jnp.int32, (t, LANES), 0) // SEL_LEN
    key = lane < HEAD_DIM
    ks_v, kw_v = y[:, 0:LANES], y[:, LANES:2 * LANES]
    ks_ref[...] = jnp.where(key, ks_v, jnp.where(lane - HEAD_DIM == block, 1.0, 0.0)).astype(BF16)
    vs_ref[...] = jnp.where(key, pltpu.roll(ks_v, HEAD_DIM, axis=1), _ones_lane((t, LANES))).astype(BF16)
    kw_ref[...] = jnp.where(key, kw_v, 0.0).astype(BF16)
    vw_ref[...] = jnp.where(key, pltpu.roll(kw_v, HEAD_DIM, axis=1), _ones_lane((t, LANES))).astype(BF16)

    half = kc_ref.shape[1]
    nb = kc_ref.shape[0]

    def hidden(x_ref, pos_ref, w1_ref):
        top = _dot((x_ref[...] + pos_ref[:, 0:half]).astype(BF16), w1_ref[0:half, :])
        bot = _dot((x_ref[...] + pos_ref[:, half:2 * half]).astype(BF16), w1_ref[half:2 * half, :])
        return _gelu(top + pltpu.roll(bot, nb - 1, axis=0)).astype(BF16)

    kv_cmp = _dot(hidden(kc_ref, posk_ref, kw1_ref), kw2_ref[...]) + _dot(hidden(vc_ref, posv_ref, vw1_ref), vw2_ref[...])
    key = lax.broadcasted_iota(jnp.int32, kv_cmp.shape, 1) < HEAD_DIM
    normed = _group_rms(kv_cmp, gmean_ref[0:LANES, 0:LANES]) * gc_ref[...]
    y = _rope(jnp.where(key, normed, kv_cmp), cc_ref[...], csu_ref[...], csd_ref[...])
    kcmp_ref[...] = jnp.where(key, y, 0.0).astype(BF16)
    vcmp_ref[...] = jnp.where(key, pltpu.roll(y, HEAD_DIM, axis=1), _ones_lane(y.shape)).T.astype(BF16)


def _rope_tables(pos):
    inv = ROPE_THETA ** (-jnp.arange(ROPE_HALF, dtype=F32) * 2.0 / ROPE_DIM)
    ang = pos.astype(F32)[:, None] * inv[None, :]
    cos, sin = jnp.cos(ang), jnp.sin(ang)
    n = pos.shape[0]
    zero = jnp.zeros((n, ROPE_HALF), F32)
    rest0 = jnp.zeros((n, HEAD_DIM - ROPE_DIM), F32)
    c = jnp.concatenate([cos, cos, rest0 + 1.0], axis=1)
    s_up = jnp.concatenate([-sin, zero, rest0], axis=1)
    s_dn = jnp.concatenate([zero, sin, rest0], axis=1)
    return c, s_up, s_dn


def _identity_tables(n):
    return jnp.ones((n, HEAD_DIM), F32), jnp.zeros((n, HEAD_DIM), F32), jnp.zeros((n, HEAD_DIM), F32)


def _nsa_prep(nkv, kc2, vc2, g_kc, g_ks, g_kw, pos_k, k_w1, k_w2, pos_v, v_w1, v_w2, b, t):
    nb = t // CMP_STRIDE
    assert HEAD_DIM + t // SEL_LEN <= LANES
    flat = CMP_STRIDE * HEAD_DIM
    rope_t = _rope_tables(jnp.arange(t))
    iden_t = _identity_tables(t)
    tabs = [jnp.concatenate([r, i, r, i], axis=1) for r, i in zip(rope_t, iden_t)]
    rope_c = _rope_tables(jnp.arange(nb) * CMP_STRIDE + CMP_LEN - 1)
    iden_c = _identity_tables(nb)
    tabs_c = [jnp.concatenate([r, i], axis=1) for r, i in zip(rope_c, iden_c)]
    one = jnp.ones((HEAD_DIM,), F32)
    gkv = jnp.concatenate([g_ks, one, g_kw, one]).reshape(1, W_MIX)
    nmask = jnp.concatenate([one, 0 * one, one, 0 * one]).reshape(1, W_MIX)
    gc = jnp.concatenate([g_kc, one]).reshape(1, LANES)
    zpad = jnp.zeros_like(k_w2)
    kw2 = jnp.concatenate([k_w2, zpad], axis=1).astype(BF16)
    vw2 = jnp.concatenate([zpad, v_w2], axis=1).astype(BF16)
    const = _const_spec
    seq = lambda width: pl.BlockSpec((t, width), lambda i: (i, 0))
    cmp_in = pl.BlockSpec((nb, flat), lambda i: (i, 0))
    cmp_out = pl.BlockSpec((nb, LANES), lambda i: (i, 0))
    return _Part(
        _nsa_prep_kernel,
        [nkv, kc2, vc2, pos_k.reshape(1, 2 * flat), pos_v.reshape(1, 2 * flat),
         k_w1.astype(BF16), v_w1.astype(BF16), kw2, vw2, gkv, nmask, _group_mean_matrix(W_MIX), *tabs, gc, *tabs_c],
        [seq(W_MIX), cmp_in, cmp_in, const((1, 2 * flat)), const((1, 2 * flat)),
         const(k_w1.shape), const(v_w1.shape), const(kw2.shape), const(vw2.shape),
         const((1, W_MIX)), const((1, W_MIX)), const((W_MIX, W_MIX)),
         const((t, W_MIX)), const((t, W_MIX)), const((t, W_MIX)),
         const((1, LANES)), const((nb, LANES)), const((nb, LANES)), const((nb, LANES))],
        [seq(LANES)] * 4 + [cmp_out, pl.BlockSpec((LANES, nb), lambda i: (i, 0))],
        [jax.ShapeDtypeStruct((b * t, LANES), BF16)] * 4
        + [jax.ShapeDtypeStruct((b * nb, LANES), BF16), jax.ShapeDtypeStruct((b * LANES, nb), BF16)])


_NSA_ROW_BLOCKED = (0, 1, 5, 6, 7, 15)


def _nsa_kernel(*refs, n_top, tq):
    q_ref, ks_ref = refs[0], refs[11]
    per_step = q_ref.shape[0] // tq

    def variant(step):
        for s in range(per_step):
            sub = [r.at[pl.ds(s * tq, tq)] if i in _NSA_ROW_BLOCKED else r for i, r in enumerate(refs)]
            _nsa_tile(*sub, n_top=n_top, qi=step * per_step + s)

    lax.switch(pl.program_id(1), [functools.partial(variant, k) for k in range(ks_ref.shape[0] // q_ref.shape[0])])


def _nsa_tile(q_ref, misc_ref, gb_ref, gq_ref, gmean_ref, c_ref, su_ref, sd_ref, ovt_ref,
              kcmp_ref, vcmp_ref, ks_ref, vs_ref, kw_ref, vw_ref, o_ref, *, n_top, qi):
    tq = q_ref.shape[0]
    t0 = qi * tq
    nb = kcmp_ref.shape[0]
    nsel = ks_ref.shape[0] // SEL_LEN
    rows = N_HEADS * tq

    qn = _rope(_group_rms(q_ref[...], gmean_ref[...]) * gq_ref[...], c_ref[...], su_ref[...], sd_ref[...]) * (ATTN_SCALE * LOG2E)
    q_cols = [_head_column(qn, h) for h in range(N_HEADS)]
    lane_q = lax.broadcasted_iota(jnp.int32, (tq, LANES), 1)

    q0 = jnp.concatenate([jnp.where(lane_q < HEAD_DIM, col, 0.0) for col in q_cols], axis=0).astype(BF16)
    cmp_end = lax.broadcasted_iota(jnp.int32, (nb, rows), 0) * CMP_STRIDE + (CMP_LEN - 1)
    qpos_c = (lax.broadcasted_iota(jnp.int32, (nb, rows), 1) & (tq - 1)) + t0
    m_cmp = cmp_end <= qpos_c
    s = jnp.where(m_cmp, _dot_nt(kcmp_ref[...], q0), NEG_INF)
    e = jnp.where(m_cmp, jnp.exp2(s - jnp.max(s, axis=0, keepdims=True)), 0.0)
    denom = jnp.sum(e, axis=0, keepdims=True)
    inv = 1.0 / jnp.where(denom > 0.0, denom, 1.0)
    p_cmp = e * inv
    o_cmp = (_dot(vcmp_ref[...], e.astype(BF16)) * inv).T

    nsp = -(-nsel // 8) * 8
    blk = lax.broadcasted_iota(jnp.int32, (nsp, tq), 0)
    cur = (lax.broadcasted_iota(jnp.int32, (nsp, tq), 1) + t0) // SEL_LEN
    valid = blk <= cur
    n_reachable = (t0 + tq - 1) // SEL_LEN + 1
    if n_reachable <= n_top:
        keep = valid
    else:
        p_sum = p_cmp[:, 0:tq] + p_cmp[:, tq:2 * tq] + p_cmp[:, 2 * tq:3 * tq] + p_cmp[:, 3 * tq:4 * tq]
        imp = _dot(ovt_ref[0:nsp, :], p_sum, precision=HIGHEST)
        forced = ((blk == 0) | (blk == cur) | (blk == cur - 1)).astype(F32)
        imp = jnp.where(valid, imp + SEL_FORCE * forced, NEG_INF)
        cur_row = (lax.broadcasted_iota(jnp.int32, (1, tq), 1) + t0) // SEL_LEN
        outranked = jnp.zeros((nsp, tq), F32)
        for j in range(min(nsel, n_reachable)):
            rival = imp[j:j + 1, :]
            wins = (rival > imp) | ((rival == imp) & (blk > j))
            outranked = outranked + jnp.where(wins & (cur_row >= j), 1.0, 0.0)
        keep = (outranked < n_top) & valid
    bias_t = jnp.where(keep | (blk >= nsel), 0.0, NEG_INF)
    if nsp < LANES:
        bias_t = jnp.concatenate([bias_t, jnp.zeros((LANES - nsp, tq), F32)], axis=0)
    bias = jnp.concatenate([bias_t[:, c * LANES:(c + 1) * LANES].T for c in range(tq // LANES)], axis=0)
    bias = pltpu.roll(bias, HEAD_DIM, axis=1)
    q = jnp.concatenate([jnp.where(lane_q < HEAD_DIM, col, bias) for col in q_cols], axis=0).astype(BF16)

    trow = lax.broadcasted_iota(jnp.int32, (rows, tq), 0) & (tq - 1)
    tcol = lax.broadcasted_iota(jnp.int32, (rows, tq), 1)
    causal_bias = jnp.where(tcol <= trow, 0.0, NEG_INF)

    def tile(j):
        return slice(j * tq, (j + 1) * tq)

    def sel_step(kj, extra_bias, st):
        m, acc = st
        s = _dot_nt(q, ks_ref[tile(kj), :])
        if extra_bias is not None:
            s = s + extra_bias
        m_new = jnp.maximum(m, jnp.max(s, axis=-1, keepdims=True))
        acc = jnp.exp2(m - m_new) * acc + _dot(jnp.exp2(s - m_new).astype(BF16), vs_ref[tile(kj), :])
        return m_new, acc

    st = (jnp.full((rows, 1), NEG_INF, F32), jnp.zeros((rows, LANES), F32))
    for kj in range(qi):
        st = sel_step(kj, None, st)
    _, acc = sel_step(qi, causal_bias, st)
    o_sel = acc / acc[:, ONE_LANE:ONE_LANE + 1]

    wt = min(NSA_WINDOW_TILE, tq)
    span = WINDOW // wt
    wrow = lax.broadcasted_iota(jnp.int32, (N_HEADS * wt, wt), 0) & (wt - 1)
    wcol = lax.broadcasted_iota(jnp.int32, (N_HEADS * wt, wt), 1)
    o_win = []
    for sub in range(tq // wt):
        q_sub = jnp.concatenate([q[h * tq + sub * wt:h * tq + (sub + 1) * wt] for h in range(N_HEADS)], axis=0)
        diag = qi * (tq // wt) + sub
        parts = []
        for d in range(min(span, diag), -1, -1):
            ksl = slice((diag - d) * wt, (diag - d + 1) * wt)
            s = _dot_nt(q_sub, kw_ref[ksl, :])
            if d == 0:
                s = s + jnp.where(wcol <= wrow, 0.0, NEG_INF)
            elif d == span:
                s = s + jnp.where(wcol > wrow, 0.0, NEG_INF)
            parts.append((s, ksl))
        m = functools.reduce(jnp.maximum, [jnp.max(s, axis=-1, keepdims=True) for s, _ in parts])
        acc = sum(_dot(jnp.exp2(s - m).astype(BF16), vw_ref[ksl, :]) for s, ksl in parts)
        o_win.append(acc / acc[:, ONE_LANE:ONE_LANE + 1])

    gate = jax.nn.sigmoid(misc_ref[...] + gb_ref[...])
    outs = []
    for h in range(N_HEADS):
        hr = slice(h * tq, (h + 1) * tq)
        g0 = _MISC_GATE + 3 * h
        o_win_h = jnp.concatenate([part[h * wt:(h + 1) * wt] for part in o_win], axis=0)
        outs.append(gate[:, g0:g0 + 1] * o_cmp[hr] + gate[:, g0 + 1:g0 + 2] * o_sel[hr] + gate[:, g0 + 2:g0 + 3] * o_win_h)
    for hp in range(N_HEADS // 2):
        o_ref[:, hp * LANES:(hp + 1) * LANES] = _pair_heads(outs[2 * hp], outs[2 * hp + 1]).astype(o_ref.dtype)


def _cmp_to_sel_overlap_t(nb, nsel):
    nc = nb - 1
    cs = np.arange(nc) * CMP_STRIDE
    ss = np.arange(nsel) * SEL_LEN
    ov = np.clip(np.minimum(cs[:, None] + CMP_LEN, ss[None, :] + SEL_LEN) - np.maximum(cs[:, None], ss[None, :]), 0, None)
    out = np.zeros((LANES, nb), np.float32)
    out[:nsel, :nc] = (ov / CMP_LEN).T
    return jnp.asarray(out)


def _nsa(nq, misc, gate_b, g_q, ks, vs, kw, vw, kcmp, vcmp, b, t):
    tq = min(NSA_TILE, t)
    nb = t // CMP_STRIDE
    nsel = t // SEL_LEN
    wt = min(NSA_WINDOW_TILE, tq)
    assert HEAD_DIM + nsel <= LANES and nb % 8 == 0 and tq & (tq - 1) == 0 and tq % LANES == 0
    assert WINDOW % wt == 0 and tq % wt == 0 and wt & (wt - 1) == 0
    gb = jnp.zeros((1, LANES), F32).at[0, _MISC_GATE:_MISC_GATE + 3 * N_HEADS].set(gate_b)
    gq = jnp.tile(g_q, N_HEADS).reshape(1, W_MIX)
    tabs = [jnp.tile(x, (1, N_HEADS)) for x in _rope_tables(jnp.arange(t))]
    rows = tq * min(NSA_TILES_PER_STEP, t // tq)
    assert t % rows == 0
    steps = t // rows
    const = lambda shape: pl.BlockSpec(shape, lambda i, j: (0,) * len(shape))
    qtile = lambda width: pl.BlockSpec((rows, width), lambda i, j: (i * steps + j, 0))
    ptile = pl.BlockSpec((rows, W_MIX), lambda i, j: (j, 0))
    seq = pl.BlockSpec((t, LANES), lambda i, j: (i, 0))
    cmp = pl.BlockSpec((nb, LANES), lambda i, j: (i, 0))
    return pl.pallas_call(
        functools.partial(_nsa_kernel, n_top=min(SEL_TOP, nsel), tq=tq),
        grid=(b, steps),
        in_specs=[qtile(W_MIX), qtile(LANES), const((1, LANES)), const((1, W_MIX)), const((W_MIX, W_MIX)),
                  ptile, ptile, ptile, const((LANES, nb)), cmp, pl.BlockSpec((LANES, nb), lambda i, j: (i, 0)),
                  seq, seq, seq, seq],
        out_specs=qtile(W_MIX),
        out_shape=jax.ShapeDtypeStruct((b * t, W_MIX), BF16),
        compiler_params=_params("parallel", "arbitrary"),
        name="nsa",
    )(nq, misc, gb, gq, _group_mean_matrix(W_MIX), *tabs, _cmp_to_sel_overlap_t(nb, nsel),
      kcmp, vcmp, ks, vs, kw, vw)


def kernel(x, ffn1_norm, ffn1_w1, ffn1_w3, ffn1_w2, mix_norm, w_in, w_out, fox_f_bias, fox_q_norm, fox_k_norm, gmlp_v_norm, gmlp_w_s, gmlp_b_s, nsa_q_norm, nsa_kc_norm, nsa_ks_norm, nsa_kw_norm, nsa_cmp_pos_k, nsa_cmp_k_w1, nsa_cmp_k_w2, nsa_cmp_pos_v, nsa_cmp_v_w1, nsa_cmp_v_w2, nsa_gate_bias, pool_w, pool_scale, ffn2_norm, ffn2_w1, ffn2_w3, ffn2_w2):
    b, t, d = x.shape
    assert t % GMLP_CHUNK == 0 and t >= CMP_LEN
    n = b * t
    xf = x.reshape(n, d)
    for l in range(ffn1_norm.shape[0]):
        xf = _ffn(xf, ffn1_norm[l], ffn1_w1[l], ffn1_w3[l], ffn1_w2[l])
        fqkv, guv, nq, nkv, pz, kc, vc, misc = _inproj(xf, mix_norm[l], w_in[l])
        ((o_a,),) = _run_parts([_fox(fqkv, misc, fox_f_bias[l], fox_q_norm[l], fox_k_norm[l], b, t)], b, "fox")
        (o_b,), (o_d,), (ks, vs, kw, vw, kcmp, vcmp) = _run_parts([
            _gmlp(guv, gmlp_v_norm[l], gmlp_w_s[l], gmlp_b_s[l], b, t),
            _pool(pz, pool_w[l], pool_scale[l], b, t),
            _nsa_prep(nkv, kc, vc, nsa_kc_norm[l], nsa_ks_norm[l], nsa_kw_norm[l],
                      nsa_cmp_pos_k[l], nsa_cmp_k_w1[l], nsa_cmp_k_w2[l],
                      nsa_cmp_pos_v[l], nsa_cmp_v_w1[l], nsa_cmp_v_w2[l], b, t)], b, "seq_mixers")
        o_c = _nsa(nq, misc, nsa_gate_bias[l], nsa_q_norm[l], ks, vs, kw, vw, kcmp, vcmp, b, t)
        xf = _ffn(xf, ffn2_norm[l], ffn2_w1[l], ffn2_w3[l], ffn2_w2[l], mixers=(o_a, o_b, o_c, o_d), w_out=w_out[l])
    return xf.reshape(b, t, d)
```

```python
import functools
from typing import Callable, NamedTuple

import numpy as np
import jax
import jax.numpy as jnp
from jax import lax
from jax.experimental import pallas as pl
from jax.experimental.pallas import tpu as pltpu

F32 = jnp.float32
BF16 = jnp.bfloat16
HIGHEST = lax.Precision.HIGHEST

HEAD_DIM = 64
N_HEADS = 4
W_MIX = N_HEADS * HEAD_DIM
ROPE_THETA = 500000.0
ROPE_DIM = HEAD_DIM // 4
ROPE_HALF = ROPE_DIM // 2
GMLP_CHUNK = 128
CMP_LEN = 32
CMP_STRIDE = 16
SEL_LEN = 64
SEL_TOP = 16
WINDOW = 512
POOL_SIZES = (2, 4, 8, 16)
FFN_RES_WEIGHT = 0.5
EPS = 1e-6
NEG_INF = -1e30
SEL_FORCE = 1e3
ATTN_SCALE = HEAD_DIM ** -0.5
LOG2E = 1.4426950408889634

LANES = 128
VMEM_LIMIT = 52 * 1024 * 1024
FFN_TOKENS = 1024
FFN_MIX_TOKENS = 1024
FFN_HIDDEN = 256
FOX_PREP_ROWS = 512
FOX_TILE = 1024
NSA_TILE = 512
NSA_WINDOW_TILE = 512
NSA_TILES_PER_STEP = 4
ONE_LANE = HEAD_DIM


def _params(*sem):
    return pltpu.CompilerParams(dimension_semantics=sem, vmem_limit_bytes=VMEM_LIMIT)


def _dot(a, b, **kw):
    return jnp.dot(a, b, preferred_element_type=F32, **kw)


def _dot_nt(a, b):
    return lax.dot_general(a, b, (((1,), (1,)), ((), ())), preferred_element_type=F32)


def _rms(x):
    return x * lax.rsqrt(jnp.mean(x * x, axis=-1, keepdims=True) + EPS)


def _group_rms(x, gmean):
    ms = _dot((x * x).astype(BF16), gmean)
    return x * lax.rsqrt(ms + EPS)


def _rope(x, c, s_up, s_dn):
    w = x.shape[-1]
    return x * c + pltpu.roll(x, w - ROPE_HALF, axis=1) * s_up + pltpu.roll(x, ROPE_HALF, axis=1) * s_dn


_GELU_C = 0.7978845608028654


def _gelu(x):
    half = 0.5 * x
    return half + half * jnp.tanh(x * (_GELU_C + (_GELU_C * 0.044715) * (x * x)))


def _log_sigmoid(x):
    return jnp.minimum(x, 0.0) - jnp.log(1.0 + jnp.exp(-jnp.abs(x)))


def _head_column(x, h):
    col = x[:, (h // 2) * LANES:(h // 2 + 1) * LANES]
    return pltpu.roll(col, HEAD_DIM, axis=1) if h % 2 else col


def _pair_heads(even, odd):
    lane = lax.broadcasted_iota(jnp.int32, even.shape, 1)
    return jnp.where(lane < HEAD_DIM, even, pltpu.roll(odd, HEAD_DIM, axis=1))


def _ones_lane(shape):
    return jnp.where(lax.broadcasted_iota(jnp.int32, shape, 1) == ONE_LANE, 1.0, 0.0)


def _ffn_kernel(x_ref, g_ref, w1_ref, w3_ref, w2_ref, *rest):
    *mix_refs, o_ref = rest
    x = x_ref[...]
    if mix_refs:
        *mixers, wout_ref = mix_refs
        for k, ref in enumerate(mixers):
            x = x + _dot(ref[...], wout_ref[k * W_MIX:(k + 1) * W_MIX, :])
    h = (_rms(x) * g_ref[...]).astype(BF16)
    acc = jnp.zeros_like(x)
    for c in range(w1_ref.shape[1] // FFN_HIDDEN):
        cs = slice(c * FFN_HIDDEN, (c + 1) * FFN_HIDDEN)
        a = _dot(h, w1_ref[:, cs])
        b = _dot(h, w3_ref[:, cs])
        acc = acc + _dot((a * jax.nn.sigmoid(a) * b).astype(BF16), w2_ref[cs, :])
    o_ref[...] = x + FFN_RES_WEIGHT * acc


def _ffn(x, g, w1, w3, w2, mixers=(), w_out=None):
    n, d = x.shape
    tm = min(FFN_MIX_TOKENS if mixers else FFN_TOKENS, n)
    resident = lambda shape: pl.BlockSpec(shape, lambda i: (0, 0), pipeline_mode=pl.Buffered(1))
    mix_specs = [pl.BlockSpec((tm, W_MIX), lambda i: (i, 0)) for _ in mixers] + ([resident(w_out.shape)] if mixers else [])
    mix_args = list(mixers) + ([w_out.astype(BF16)] if mixers else [])
    return pl.pallas_call(
        _ffn_kernel,
        grid=(n // tm,),
        in_specs=[pl.BlockSpec((tm, d), lambda i: (i, 0)), resident((1, d)),
                  resident(w1.shape), resident(w3.shape), resident(w2.shape)] + mix_specs,
        out_specs=pl.BlockSpec((tm, d), lambda i: (i, 0)),
        out_shape=jax.ShapeDtypeStruct((n, d), F32),
        compiler_params=_params("parallel"),
        name="ffn_mix" if mixers else "ffn",
    )(x, g.reshape(1, d), w1.astype(BF16), w3.astype(BF16), w2.astype(BF16), *mix_args)


_IN_GROUPS = (768, 512, 256, 256, 256, 128, 128)
_MISC_FORGET = 0
_MISC_GATE = 4


def _inproj_kernel(x_ref, g_ref, w_ref, fqkv_ref, guv_ref, nq_ref, nkv_ref, pz_ref, kc_ref, vc_ref, misc_ref, kcvc_s):
    h = (_rms(x_ref[...]) * g_ref[...]).astype(BF16)
    off = 0
    for ref, width in zip((fqkv_ref, guv_ref, nq_ref, nkv_ref, pz_ref), _IN_GROUPS[:5]):
        ref[...] = _dot(h, w_ref[:, off:off + width])
        off += width
    tail = _dot(h, w_ref[:, off:off + 2 * LANES])
    misc_ref[...] = tail[:, LANES:]
    kcvc_s[...] = tail[:, :LANES]
    groups = x_ref.shape[0] // CMP_STRIDE
    lane = lax.broadcasted_iota(jnp.int32, (groups, LANES), 1)
    for j in range(0, CMP_STRIDE, 2):
        even = kcvc_s[pl.ds(j, groups, stride=CMP_STRIDE), :]
        odd = kcvc_s[pl.ds(j + 1, groups, stride=CMP_STRIDE), :]
        col = slice((j // 2) * LANES, (j // 2 + 1) * LANES)
        kc_ref[:, col] = jnp.where(lane < HEAD_DIM, even, pltpu.roll(odd, HEAD_DIM, axis=1))
        vc_ref[:, col] = jnp.where(lane < HEAD_DIM, pltpu.roll(even, HEAD_DIM, axis=1), odd)


def _relayout_w_in(w_in):
    o = np.cumsum((0, 256, 256, 256, 4, 256, 256, 256, 64, 64, 64, 64, 64, 64, 12, 256))
    fq, ff, gu, nq, nkc, nks, ng, pz, end = o[0], o[3], o[4], o[6], o[7], o[9], o[13], o[14], o[15]
    d = w_in.shape[0]
    pad = jnp.zeros((d, LANES - 16), w_in.dtype)
    return jnp.concatenate([
        w_in[:, fq:ff], w_in[:, gu:nq], w_in[:, nq:nkc], w_in[:, nks:ng], w_in[:, pz:end],
        w_in[:, nkc:nks], w_in[:, ff:gu], w_in[:, ng:pz], pad], axis=1)


def _inproj(x, g, w_in):
    n, d = x.shape
    tm = min(FFN_TOKENS, n)
    assert tm % (8 * CMP_STRIDE) == 0
    w = _relayout_w_in(w_in).astype(BF16)
    flat = CMP_STRIDE * HEAD_DIM
    outs = [(tm, n, wd) for wd in (768, 512, 256, 256, 256)] + [(tm // CMP_STRIDE, n // CMP_STRIDE, flat)] * 2 + [(tm, n, LANES)]
    return pl.pallas_call(
        _inproj_kernel,
        grid=(n // tm,),
        in_specs=[
            pl.BlockSpec((tm, d), lambda i: (i, 0)),
            pl.BlockSpec((1, d), lambda i: (0, 0)),
            pl.BlockSpec(w.shape, lambda i: (0, 0)),
        ],
        out_specs=[pl.BlockSpec((rows, wd), lambda i: (i, 0)) for rows, _, wd in outs],
        out_shape=[jax.ShapeDtypeStruct((total, wd), F32) for _, total, wd in outs],
        scratch_shapes=[pltpu.VMEM((tm, LANES), F32)],
        compiler_params=_params("parallel"),
        name="inproj",
    )(x, g.reshape(1, d), w)


def _bf16_pieces(x):
    p1 = x.astype(BF16).astype(F32)
    r1 = x - p1
    p2 = r1.astype(BF16).astype(F32)
    p3 = (r1 - p2).astype(BF16).astype(F32)
    return p1, p2, p3


_N_PIECES = 3
_N_EXTRA = _N_PIECES * N_HEADS
_PACK_ONE = _N_EXTRA


def _fox_placement():
    pq = np.zeros((LANES, LANES), np.float32)
    pk = np.zeros((LANES, LANES), np.float32)
    for h in range(N_HEADS):
        for p in range(_N_PIECES):
            pq[p * N_HEADS + h, HEAD_DIM + _N_PIECES * h + p] = 1.0
            pk[p * N_HEADS + h, HEAD_DIM + _N_EXTRA + _N_PIECES * h + p] = -1.0
    pq[_PACK_ONE, HEAD_DIM + _N_EXTRA:HEAD_DIM + 2 * _N_EXTRA] = 1.0
    pk[_PACK_ONE, HEAD_DIM:HEAD_DIM + _N_EXTRA] = 1.0
    return jnp.asarray(np.concatenate([pq, pk], axis=1), BF16)


def _fox_kernel(qkv_ref, misc_ref, fbt_ref, gq_ref, gk_ref, gmean_ref, pqk_ref, o_ref, q_s, k_s, v_s, o_s):
    t = qkv_ref.shape[0]
    pr = min(FOX_PREP_ROWS, t)
    gmean = gmean_ref[...]
    upper = (lax.broadcasted_iota(jnp.int32, (pr, pr), 0) <= lax.broadcasted_iota(jnp.int32, (pr, pr), 1)).astype(BF16)
    row8 = lax.broadcasted_iota(jnp.int32, (8, pr), 0)
    lane = lax.broadcasted_iota(jnp.int32, (pr, LANES), 1)
    extra = lane - HEAD_DIM
    own = [((extra >= _N_PIECES * h) & (extra < _N_PIECES * (h + 1)))
           | ((extra >= _N_EXTRA + _N_PIECES * h) & (extra < _N_EXTRA + _N_PIECES * (h + 1))) for h in range(N_HEADS)]
    ones_v = jnp.where(extra == 0, 1.0, 0.0)
    carry = jnp.zeros((8, pr), F32)
    for r in range(t // pr):
        sl = slice(r * pr, (r + 1) * pr)
        logits_t = misc_ref[sl, :].T[0:8, :]
        log_f = _log_sigmoid(logits_t + fbt_ref[...]) * LOG2E
        c = sum(_dot(piece.astype(BF16), upper) for piece in _bf16_pieces(log_f)) + carry
        carry = jnp.broadcast_to(c[:, pr - 1:pr], (8, pr))
        p1, p2, p3 = _bf16_pieces(c)
        lo = jnp.where(row8 < N_HEADS, p1, pltpu.roll(p2, N_HEADS, axis=0))
        hi = jnp.where(row8 < N_HEADS, p3, jnp.where(row8 == _PACK_ONE - 8, 1.0, 0.0))
        packed = jnp.concatenate([lo, hi, jnp.zeros((LANES - 16, pr), F32)], axis=0).T.astype(BF16)
        q_extra = _dot(packed, pqk_ref[:, :LANES])
        k_extra = _dot(packed, pqk_ref[:, LANES:])
        qn = _group_rms(qkv_ref[sl, 0:W_MIX], gmean) * gq_ref[...] * (ATTN_SCALE * LOG2E)
        kn = _group_rms(qkv_ref[sl, W_MIX:2 * W_MIX], gmean) * gk_ref[...]
        v = qkv_ref[sl, 2 * W_MIX:3 * W_MIX]
        for h in range(N_HEADS):
            q_s[h, sl, :] = jnp.where(extra < 0, _head_column(qn, h), q_extra).astype(BF16)
            k_s[h, sl, :] = jnp.where(extra < 0, _head_column(kn, h), jnp.where(own[h], k_extra, 0.0)).astype(BF16)
            v_aug = jnp.where(extra < 0, _head_column(v, h), ones_v)
            v_s[h, :, sl] = v_aug.T.astype(BF16)

    tq = min(FOX_TILE, t)
    causal = lax.broadcasted_iota(jnp.int32, (tq, tq), 0) <= lax.broadcasted_iota(jnp.int32, (tq, tq), 1)

    def one_head(h):
        for qi in range(t // tq):
            dsl = slice(qi * tq, (qi + 1) * tq)
            q = q_s[h, dsl, :]
            s_d = jnp.where(causal, _dot_nt(k_s[h, dsl, :], q), NEG_INF)
            m = jnp.max(s_d, axis=0, keepdims=True)
            if qi:
                s_o = _dot_nt(k_s[h, 0:qi * tq, :], q)
                m = jnp.maximum(m, jnp.max(s_o, axis=0, keepdims=True))
            acc = _dot(v_s[h, :, dsl], jnp.exp2(s_d - m).astype(BF16))
            if qi:
                acc = acc + _dot(v_s[h, :, 0:qi * tq], jnp.exp2(s_o - m).astype(BF16))
            o_s[h, dsl, :] = (acc / acc[ONE_LANE:ONE_LANE + 1, :]).T

    for h in range(N_HEADS):
        one_head(h)
    for hp in range(N_HEADS // 2):
        o_ref[:, hp * LANES:(hp + 1) * LANES] = _pair_heads(o_s[2 * hp], o_s[2 * hp + 1]).astype(o_ref.dtype)


def _group_mean_matrix(width):
    g = np.kron(np.eye(width // HEAD_DIM, dtype=np.float32), np.full((HEAD_DIM, HEAD_DIM), 1.0 / HEAD_DIM, np.float32))
    return jnp.asarray(g, BF16)


def _fox(fqkv, misc, f_bias, g_q, g_k, b, t):
    assert t % min(FOX_TILE, t) == 0 and t % min(FOX_PREP_ROWS, t) == 0
    pr = min(FOX_PREP_ROWS, t)
    assert _MISC_FORGET == 0 and N_HEADS <= 4 and pr % LANES == 0
    fbt = jnp.broadcast_to(jnp.zeros((8,), F32).at[:N_HEADS].set(f_bias)[:, None], (8, pr))
    gq = jnp.tile(g_q, N_HEADS).reshape(1, W_MIX)
    gk = jnp.tile(g_k, N_HEADS).reshape(1, W_MIX)
    const = _const_spec
    head_scratch = lambda dtype: pltpu.VMEM((N_HEADS, t, LANES), dtype)
    return _Part(
        _fox_kernel,
        [fqkv, misc, fbt, gq, gk, _group_mean_matrix(W_MIX), _fox_placement()],
        [pl.BlockSpec((t, 3 * W_MIX), lambda i: (i, 0)), pl.BlockSpec((t, LANES), lambda i: (i, 0)),
         const((8, pr)), const((1, W_MIX)), const((1, W_MIX)), const((W_MIX, W_MIX)), const((LANES, 2 * LANES))],
        [pl.BlockSpec((t, W_MIX), lambda i: (i, 0))],
        [jax.ShapeDtypeStruct((b * t, W_MIX), BF16)],
        (head_scratch(BF16), head_scratch(BF16), pltpu.VMEM((N_HEADS, LANES, t), BF16), head_scratch(F32)))


def _gmlp_kernel(uv_ref, gv_ref, gmean_ref, w_ref, bias_ref, o_ref):
    c = GMLP_CHUNK
    rows = lax.broadcasted_iota(jnp.int32, (c, c), 0)
    cols = lax.broadcasted_iota(jnp.int32, (c, c), 1)
    lane_group = lax.broadcasted_iota(jnp.int32, (c, W_MIX), 1) // HEAD_DIM
    w_cat = jnp.concatenate([jnp.where(cols <= rows, w_ref[g], 0.0).astype(BF16) for g in range(N_HEADS)], axis=1)
    v_all = (_group_rms(_gelu(uv_ref[:, W_MIX:2 * W_MIX]), gmean_ref[...]) * gv_ref[...]).astype(BF16)
    for r in range(uv_ref.shape[0] // c):
        sl = slice(r * c, (r + 1) * c)
        u = _gelu(uv_ref[sl, 0:W_MIX])
        v = v_all[sl]
        v_stack = jnp.concatenate([jnp.where(lane_group == g, v, jnp.zeros_like(v)) for g in range(N_HEADS)], axis=0)
        o_ref[sl, :] = (u * (bias_ref[...] + _dot(w_cat, v_stack))).astype(o_ref.dtype)


class _Part(NamedTuple):
    body: Callable
    args: list
    in_specs: list
    out_specs: list
    out_shape: list
    scratch_shapes: tuple = ()


def _const_spec(shape):
    return pl.BlockSpec(shape, lambda i: (0,) * len(shape))


def _run_parts(parts, b, name):
    n_in = [len(p.in_specs) for p in parts]
    n_out = [len(p.out_specs) for p in parts]
    n_scr = [len(p.scratch_shapes) for p in parts]

    def body(*refs):
        ins, outs, scr = refs[:sum(n_in)], refs[sum(n_in):sum(n_in) + sum(n_out)], refs[sum(n_in) + sum(n_out):]
        for p, i0, o0, s0 in zip(parts, np.cumsum([0] + n_in), np.cumsum([0] + n_out), np.cumsum([0] + n_scr)):
            p.body(*ins[i0:i0 + len(p.in_specs)], *outs[o0:o0 + len(p.out_specs)], *scr[s0:s0 + len(p.scratch_shapes)])

    outs = pl.pallas_call(
        body,
        grid=(b,),
        in_specs=[s for p in parts for s in p.in_specs],
        out_specs=[s for p in parts for s in p.out_specs],
        out_shape=[s for p in parts for s in p.out_shape],
        scratch_shapes=[s for p in parts for s in p.scratch_shapes],
        compiler_params=_params("parallel"),
        name=name,
    )(*[a for p in parts for a in p.args])
    return [outs[o0:o0 + k] for o0, k in zip(np.cumsum([0] + n_out), n_out)]


def _gmlp(guv, g_v, w_s, b_s, b, t):
    bias = jnp.repeat(b_s.T, HEAD_DIM, axis=1)
    return _Part(
        _gmlp_kernel,
        [guv, g_v.reshape(1, W_MIX), _group_mean_matrix(W_MIX), w_s, bias],
        [pl.BlockSpec((t, 2 * W_MIX), lambda i: (i, 0)), _const_spec((1, W_MIX)), _const_spec((W_MIX, W_MIX)),
         _const_spec(w_s.shape), _const_spec(bias.shape)],
        [pl.BlockSpec((t, W_MIX), lambda i: (i, 0))],
        [jax.ShapeDtypeStruct((b * t, W_MIX), BF16)])


def _pool_kernel(z_ref, inv_cnt_ref, w_ref, scale_ref, o_ref):
    t = z_ref.shape[0]
    z = z_ref[...]
    row = lax.broadcasted_iota(jnp.int32, (t, W_MIX), 0)
    lane_group = lax.broadcasted_iota(jnp.int32, (t, W_MIX), 1) // HEAD_DIM
    sums = {}
    s = z
    k = 1
    while k < max(POOL_SIZES):
        s = s + jnp.where(row >= k, pltpu.roll(s, k, axis=0), 0.0)
        k *= 2
        sums[k] = s
    win_sum = sums[POOL_SIZES[-1]]
    for g in range(len(POOL_SIZES) - 2, -1, -1):
        win_sum = jnp.where(lane_group == g, sums[POOL_SIZES[g]], win_sum)
    pooled = win_sum * inv_cnt_ref[...] - z
    o_ref[...] = (_dot(pooled.astype(BF16), w_ref[...]) * scale_ref[...]).astype(o_ref.dtype)


def _pool(pz, w_p, scale, b, t):
    w_bd = jax.scipy.linalg.block_diag(*[w_p[g] for g in range(N_HEADS)]).astype(BF16)
    win = jnp.repeat(jnp.array(POOL_SIZES, jnp.int32), HEAD_DIM)
    inv_cnt = 1.0 / jnp.minimum(jnp.arange(t)[:, None] + 1, win[None, :]).astype(F32)
    return _Part(
        _pool_kernel,
        [pz, inv_cnt, w_bd, scale.reshape(1, W_MIX)],
        [pl.BlockSpec((t, W_MIX), lambda i: (i, 0)), _const_spec((t, W_MIX)), _const_spec((W_MIX, W_MIX)),
         _const_spec((1, W_MIX))],
        [pl.BlockSpec((t, W_MIX), lambda i: (i, 0))],
        [jax.ShapeDtypeStruct((b * t, W_MIX), BF16)])


def _nsa_prep_kernel(kv_ref, kc_ref, vc_ref, posk_ref, posv_ref, kw1_ref, vw1_ref, kw2_ref, vw2_ref,
                     gkv_ref, nmask_ref, gmean_ref, c_ref, su_ref, sd_ref, gc_ref, cc_ref, csu_ref, csd_ref,
                     ks_ref, vs_ref, kw_ref, vw_ref, kcmp_ref, vcmp_ref):
    t = kv_ref.shape[0]
    x = kv_ref[...]
    normed = _group_rms(x, gmean_ref[...]) * gkv_ref[...]
    y = _rope(jnp.where(nmask_ref[...] > 0.5, normed, x), c_ref[...], su_ref[...], sd_ref[...])
    lane = lax.broadcasted_iota(jnp.int32, (t, LANES), 1)
    block = lax.broadcasted_iota(jnp.int32, (t, LANES), 0) // SEL_LEN
    key = lane < HEAD_DIM
    ks_v, kw_v = y[:, 0:LANES], y[:, LANES:2 * LANES]
    ks_ref[...] = jnp.where(key, ks_v, jnp.where(lane - HEAD_DIM == block, 1.0, 0.0)).astype(BF16)
    vs_ref[...] = jnp.where(key, pltpu.roll(ks_v, HEAD_DIM, axis=1), _ones_lane((t, LANES))).astype(BF16)
    kw_ref[...] = jnp.where(key, kw_v, 0.0).astype(BF16)
    vw_ref[...] = jnp.where(key, pltpu.roll(kw_v, HEAD_DIM, axis=1), _ones_lane((t, LANES))).astype(BF16)

    half = kc_ref.shape[1]
    nb = kc_ref.shape[0]

    def hidden(x_ref, pos_ref, w1_ref):
        top = _dot((x_ref[...] + pos_ref[:, 0:half]).astype(BF16), w1_ref[0:half, :])
        bot = _dot((x_ref[...] + pos_ref[:, half:2 * half]).astype(BF16), w1_ref[half:2 * half, :])
        return _gelu(top + pltpu.roll(bot, nb - 1, axis=0)).astype(BF16)

    kv_cmp = _dot(hidden(kc_ref, posk_ref, kw1_ref), kw2_ref[...]) + _dot(hidden(vc_ref, posv_ref, vw1_ref), vw2_ref[...])
    key = lax.broadcasted_iota(jnp.int32, kv_cmp.shape, 1) < HEAD_DIM
    normed = _group_rms(kv_cmp, gmean_ref[0:LANES, 0:LANES]) * gc_ref[...]
    y = _rope(jnp.where(key, normed, kv_cmp), cc_ref[...], csu_ref[...], csd_ref[...])
    kcmp_ref[...] = jnp.where(key, y, 0.0).astype(BF16)
    vcmp_ref[...] = jnp.where(key, pltpu.roll(y, HEAD_DIM, axis=1), _ones_lane(y.shape)).T.astype(BF16)


def _rope_tables(pos):
    inv = ROPE_THETA ** (-jnp.arange(ROPE_HALF, dtype=F32) * 2.0 / ROPE_DIM)
    ang = pos.astype(F32)[:, None] * inv[None, :]
    cos, sin = jnp.cos(ang), jnp.sin(ang)
    n = pos.shape[0]
    zero = jnp.zeros((n, ROPE_HALF), F32)
    rest0 = jnp.zeros((n, HEAD_DIM - ROPE_DIM), F32)
    c = jnp.concatenate([cos, cos, rest0 + 1.0], axis=1)
    s_up = jnp.concatenate([-sin, zero, rest0], axis=1)
    s_dn = jnp.concatenate([zero, sin, rest0], axis=1)
    return c, s_up, s_dn


def _identity_tables(n):
    return jnp.ones((n, HEAD_DIM), F32), jnp.zeros((n, HEAD_DIM), F32), jnp.zeros((n, HEAD_DIM), F32)


def _nsa_prep(nkv, kc2, vc2, g_kc, g_ks, g_kw, pos_k, k_w1, k_w2, pos_v, v_w1, v_w2, b, t):
    nb = t // CMP_STRIDE
    assert HEAD_DIM + t // SEL_LEN <= LANES
    flat = CMP_STRIDE * HEAD_DIM
    rope_t = _rope_tables(jnp.arange(t))
    iden_t = _identity_tables(t)
    tabs = [jnp.concatenate([r, i, r, i], axis=1) for r, i in zip(rope_t, iden_t)]
    rope_c = _rope_tables(jnp.arange(nb) * CMP_STRIDE + CMP_LEN - 1)
    iden_c = _identity_tables(nb)
    tabs_c = [jnp.concatenate([r, i], axis=1) for r, i in zip(rope_c, iden_c)]
    one = jnp.ones((HEAD_DIM,), F32)
    gkv = jnp.concatenate([g_ks, one, g_kw, one]).reshape(1, W_MIX)
    nmask = jnp.concatenate([one, 0 * one, one, 0 * one]).reshape(1, W_MIX)
    gc = jnp.concatenate([g_kc, one]).reshape(1, LANES)
    zpad = jnp.zeros_like(k_w2)
    kw2 = jnp.concatenate([k_w2, zpad], axis=1).astype(BF16)
    vw2 = jnp.concatenate([zpad, v_w2], axis=1).astype(BF16)
    const = _const_spec
    seq = lambda width: pl.BlockSpec((t, width), lambda i: (i, 0))
    cmp_in = pl.BlockSpec((nb, flat), lambda i: (i, 0))
    cmp_out = pl.BlockSpec((nb, LANES), lambda i: (i, 0))
    return _Part(
        _nsa_prep_kernel,
        [nkv, kc2, vc2, pos_k.reshape(1, 2 * flat), pos_v.reshape(1, 2 * flat),
         k_w1.astype(BF16), v_w1.astype(BF16), kw2, vw2, gkv, nmask, _group_mean_matrix(W_MIX), *tabs, gc, *tabs_c],
        [seq(W_MIX), cmp_in, cmp_in, const((1, 2 * flat)), const((1, 2 * flat)),
         const(k_w1.shape), const(v_w1.shape), const(kw2.shape), const(vw2.shape),
         const((1, W_MIX)), const((1, W_MIX)), const((W_MIX, W_MIX)),
         const((t, W_MIX)), const((t, W_MIX)), const((t, W_MIX)),
         const((1, LANES)), const((nb, LANES)), const((nb, LANES)), const((nb, LANES))],
        [seq(LANES)] * 4 + [cmp_out, pl.BlockSpec((LANES, nb), lambda i: (i, 0))],
        [jax.ShapeDtypeStruct((b * t, LANES), BF16)] * 4
        + [jax.ShapeDtypeStruct((b * nb, LANES), BF16), jax.ShapeDtypeStruct((b * LANES, nb), BF16)])


_NSA_ROW_BLOCKED = (0, 1, 5, 6, 7, 15)


def _nsa_kernel(*refs, n_top, tq):
    q_ref, ks_ref = refs[0], refs[11]
    per_step = q_ref.shape[0] // tq

    def variant(step):
        for s in range(per_step):
            sub = [r.at[pl.ds(s * tq, tq)] if i in _NSA_ROW_BLOCKED else r for i, r in enumerate(refs)]
            _nsa_tile(*sub, n_top=n_top, qi=step * per_step + s)

    lax.switch(pl.program_id(1), [functools.partial(variant, k) for k in range(ks_ref.shape[0] // q_ref.shape[0])])


def _nsa_tile(q_ref, misc_ref, gb_ref, gq_ref, gmean_ref, c_ref, su_ref, sd_ref, ovt_ref,
              kcmp_ref, vcmp_ref, ks_ref, vs_ref, kw_ref, vw_ref, o_ref, *, n_top, qi):
    tq = q_ref.shape[0]
    t0 = qi * tq
    nb = kcmp_ref.shape[0]
    nsel = ks_ref.shape[0] // SEL_LEN
    rows = N_HEADS * tq

    qn = _rope(_group_rms(q_ref[...], gmean_ref[...]) * gq_ref[...], c_ref[...], su_ref[...], sd_ref[...]) * (ATTN_SCALE * LOG2E)
    q_cols = [_head_column(qn, h) for h in range(N_HEADS)]
    lane_q = lax.broadcasted_iota(jnp.int32, (tq, LANES), 1)

    q0 = jnp.concatenate([jnp.where(lane_q < HEAD_DIM, col, 0.0) for col in q_cols], axis=0).astype(BF16)
    cmp_end = lax.broadcasted_iota(jnp.int32, (nb, rows), 0) * CMP_STRIDE + (CMP_LEN - 1)
    qpos_c = (lax.broadcasted_iota(jnp.int32, (nb, rows), 1) & (tq - 1)) + t0
    m_cmp = cmp_end <= qpos_c
    s = jnp.where(m_cmp, _dot_nt(kcmp_ref[...], q0), NEG_INF)
    e = jnp.where(m_cmp, jnp.exp2(s - jnp.max(s, axis=0, keepdims=True)), 0.0)
    denom = jnp.sum(e, axis=0, keepdims=True)
    inv = 1.0 / jnp.where(denom > 0.0, denom, 1.0)
    p_cmp = e * inv
    o_cmp = (_dot(vcmp_ref[...], e.astype(BF16)) * inv).T

    nsp = -(-nsel // 8) * 8
    blk = lax.broadcasted_iota(jnp.int32, (nsp, tq), 0)
    cur = (lax.broadcasted_iota(jnp.int32, (nsp, tq), 1) + t0) // SEL_LEN
    valid = blk <= cur
    n_reachable = (t0 + tq - 1) // SEL_LEN + 1
    if n_reachable <= n_top:
        keep = valid
    else:
        p_sum = p_cmp[:, 0:tq] + p_cmp[:, tq:2 * tq] + p_cmp[:, 2 * tq:3 * tq] + p_cmp[:, 3 * tq:4 * tq]
        imp = _dot(ovt_ref[0:nsp, :], p_sum, precision=HIGHEST)
        forced = ((blk == 0) | (blk == cur) | (blk == cur - 1)).astype(F32)
        imp = jnp.where(valid, imp + SEL_FORCE * forced, NEG_INF)
        cur_row = (lax.broadcasted_iota(jnp.int32, (1, tq), 1) + t0) // SEL_LEN
        outranked = jnp.zeros((nsp, tq), F32)
        for j in range(min(nsel, n_reachable)):
            rival = imp[j:j + 1, :]
            wins = (rival > imp) | ((rival == imp) & (blk > j))
            outranked = outranked + jnp.where(wins & (cur_row >= j), 1.0, 0.0)
        keep = (outranked < n_top) & valid
    bias_t = jnp.where(keep | (blk >= nsel), 0.0, NEG_INF)
    if nsp < LANES:
        bias_t = jnp.concatenate([bias_t, jnp.zeros((LANES - nsp, tq), F32)], axis=0)
    bias = jnp.concatenate([bias_t[:, c * LANES:(c + 1) * LANES].T for c in range(tq // LANES)], axis=0)
    bias = pltpu.roll(bias, HEAD_DIM, axis=1)
    q = jnp.concatenate([jnp.where(lane_q < HEAD_DIM, col, bias) for col in q_cols], axis=0).astype(BF16)

    trow = lax.broadcasted_iota(jnp.int32, (rows, tq), 0) & (tq - 1)
    tcol = lax.broadcasted_iota(jnp.int32, (rows, tq), 1)
    causal_bias = jnp.where(tcol <= trow, 0.0, NEG_INF)

    def tile(j):
        return slice(j * tq, (j + 1) * tq)

    def sel_step(kj, extra_bias, st):
        m, acc = st
        s = _dot_nt(q, ks_ref[tile(kj), :])
        if extra_bias is not None:
            s = s + extra_bias
        m_new = jnp.maximum(m, jnp.max(s, axis=-1, keepdims=True))
        acc = jnp.exp2(m - m_new) * acc + _dot(jnp.exp2(s - m_new).astype(BF16), vs_ref[tile(kj), :])
        return m_new, acc

    st = (jnp.full((rows, 1), NEG_INF, F32), jnp.zeros((rows, LANES), F32))
    for kj in range(qi):
        st = sel_step(kj, None, st)
    _, acc = sel_step(qi, causal_bias, st)
    o_sel = acc / acc[:, ONE_LANE:ONE_LANE + 1]

    wt = min(NSA_WINDOW_TILE, tq)
    span = WINDOW // wt
    wrow = lax.broadcasted_iota(jnp.int32, (N_HEADS * wt, wt), 0) & (wt - 1)
    wcol = lax.broadcasted_iota(jnp.int32, (N_HEADS * wt, wt), 1)
    o_win = []
    for sub in range(tq // wt):
        q_sub = jnp.concatenate([q[h * tq + sub * wt:h * tq + (sub + 1) * wt] for h in range(N_HEADS)], axis=0)
        diag = qi * (tq // wt) + sub
        parts = []
        for d in range(min(span, diag), -1, -1):
            ksl = slice((diag - d) * wt, (diag - d + 1) * wt)
            s = _dot_nt(q_sub, kw_ref[ksl, :])
            if d == 0:
                s = s + jnp.where(wcol <= wrow, 0.0, NEG_INF)
            elif d == span:
                s = s + jnp.where(wcol > wrow, 0.0, NEG_INF)
            parts.append((s, ksl))
        m = functools.reduce(jnp.maximum, [jnp.max(s, axis=-1, keepdims=True) for s, _ in parts])
        acc = sum(_dot(jnp.exp2(s - m).astype(BF16), vw_ref[ksl, :]) for s, ksl in parts)
        o_win.append(acc / acc[:, ONE_LANE:ONE_LANE + 1])

    gate = jax.nn.sigmoid(misc_ref[...] + gb_ref[...])
    outs = []
    for h in range(N_HEADS):
        hr = slice(h * tq, (h + 1) * tq)
        g0 = _MISC_GATE + 3 * h
        o_win_h = jnp.concatenate([part[h * wt:(h + 1) * wt] for part in o_win], axis=0)
        outs.append(gate[:, g0:g0 + 1] * o_cmp[hr] + gate[:, g0 + 1:g0 + 2] * o_sel[hr] + gate[:, g0 + 2:g0 + 3] * o_win_h)
    for hp in range(N_HEADS // 2):
        o_ref[:, hp * LANES:(hp + 1) * LANES] = _pair_heads(outs[2 * hp], outs[2 * hp + 1]).astype(o_ref.dtype)


def _cmp_to_sel_overlap_t(nb, nsel):
    nc = nb - 1
    cs = np.arange(nc) * CMP_STRIDE
    ss = np.arange(nsel) * SEL_LEN
    ov = np.clip(np.minimum(cs[:, None] + CMP_LEN, ss[None, :] + SEL_LEN) - np.maximum(cs[:, None], ss[None, :]), 0, None)
    out = np.zeros((LANES, nb), np.float32)
    out[:nsel, :nc] = (ov / CMP_LEN).T
    return jnp.asarray(out)


def _nsa(nq, misc, gate_b, g_q, ks, vs, kw, vw, kcmp, vcmp, b, t):
    tq = min(NSA_TILE, t)
    nb = t // CMP_STRIDE
    nsel = t // SEL_LEN
    wt = min(NSA_WINDOW_TILE, tq)
    assert HEAD_DIM + nsel <= LANES and nb % 8 == 0 and tq & (tq - 1) == 0 and tq % LANES == 0
    assert WINDOW % wt == 0 and tq % wt == 0 and wt & (wt - 1) == 0
    gb = jnp.zeros((1, LANES), F32).at[0, _MISC_GATE:_MISC_GATE + 3 * N_HEADS].set(gate_b)
    gq = jnp.tile(g_q, N_HEADS).reshape(1, W_MIX)
    tabs = [jnp.tile(x, (1, N_HEADS)) for x in _rope_tables(jnp.arange(t))]
    rows = tq * min(NSA_TILES_PER_STEP, t // tq)
    assert t % rows == 0
    steps = t // rows
    const = lambda shape: pl.BlockSpec(shape, lambda i, j: (0,) * len(shape))
    qtile = lambda width: pl.BlockSpec((rows, width), lambda i, j: (i * steps + j, 0))
    ptile = pl.BlockSpec((rows, W_MIX), lambda i, j: (j, 0))
    seq = pl.BlockSpec((t, LANES), lambda i, j: (i, 0))
    cmp = pl.BlockSpec((nb, LANES), lambda i, j: (i, 0))
    return pl.pallas_call(
        functools.partial(_nsa_kernel, n_top=min(SEL_TOP, nsel), tq=tq),
        grid=(b, steps),
        in_specs=[qtile(W_MIX), qtile(LANES), const((1, LANES)), const((1, W_MIX)), const((W_MIX, W_MIX)),
                  ptile, ptile, ptile, const((LANES, nb)), cmp, pl.BlockSpec((LANES, nb), lambda i, j: (i, 0)),
                  seq, seq, seq, seq],
        out_specs=qtile(W_MIX),
        out_shape=jax.ShapeDtypeStruct((b * t, W_MIX), BF16),
        compiler_params=_params("parallel", "arbitrary"),
        name="nsa",
    )(nq, misc, gb, gq, _group_mean_matrix(W_MIX), *tabs, _cmp_to_sel_overlap_t(nb, nsel),
      kcmp, vcmp, ks, vs, kw, vw)


def kernel(x, ffn1_norm, ffn1_w1, ffn1_w3, ffn1_w2, mix_norm, w_in, w_out, fox_f_bias, fox_q_norm, fox_k_norm, gmlp_v_norm, gmlp_w_s, gmlp_b_s, nsa_q_norm, nsa_kc_norm, nsa_ks_norm, nsa_kw_norm, nsa_cmp_pos_k, nsa_cmp_k_w1, nsa_cmp_k_w2, nsa_cmp_pos_v, nsa_cmp_v_w1, nsa_cmp_v_w2, nsa_gate_bias, pool_w, pool_scale, ffn2_norm, ffn2_w1, ffn2_w3, ffn2_w2):
    b, t, d = x.shape
    assert t % GMLP_CHUNK == 0 and t >= CMP_LEN
    n = b * t
    xf = x.reshape(n, d)
    for l in range(ffn1_norm.shape[0]):
        xf = _ffn(xf, ffn1_norm[l], ffn1_w1[l], ffn1_w3[l], ffn1_w2[l])
        fqkv, guv, nq, nkv, pz, kc, vc, misc = _inproj(xf, mix_norm[l], w_in[l])
        ((o_a,),) = _run_parts([_fox(fqkv, misc, fox_f_bias[l], fox_q_norm[l], fox_k_norm[l], b, t)], b, "fox")
        (o_b,), (o_d,), (ks, vs, kw, vw, kcmp, vcmp) = _run_parts([
            _gmlp(guv, gmlp_v_norm[l], gmlp_w_s[l], gmlp_b_s[l], b, t),
            _pool(pz, pool_w[l], pool_scale[l], b, t),
            _nsa_prep(nkv, kc, vc, nsa_kc_norm[l], nsa_ks_norm[l], nsa_kw_norm[l],
                      nsa_cmp_pos_k[l], nsa_cmp_k_w1[l], nsa_cmp_k_w2[l],
                      nsa_cmp_pos_v[l], nsa_cmp_v_w1[l], nsa_cmp_v_w2[l], b, t)], b, "seq_mixers")
        o_c = _nsa(nq, misc, nsa_gate_bias[l], nsa_q_norm[l], ks, vs, kw, vw, kcmp, vcmp, b, t)
        xf = _ffn(xf, ffn2_norm[l], ffn2_w1[l], ffn2_w3[l], ffn2_w2[l], mixers=(o_a, o_b, o_c, o_d), w_out=w_out[l])
    return xf.reshape(b, t, d)
```

```python
import functools
from typing import Callable, NamedTuple

import numpy as np
import jax
import jax.numpy as jnp
from jax import lax
from jax.experimental import pallas as pl
from jax.experimental.pallas import tpu as pltpu

F32 = jnp.float32
BF16 = jnp.bfloat16
HIGHEST = lax.Precision.HIGHEST

HEAD_DIM = 64
N_HEADS = 4
W_MIX = N_HEADS * HEAD_DIM
ROPE_THETA = 500000.0
ROPE_DIM = HEAD_DIM // 4
ROPE_HALF = ROPE_DIM // 2
GMLP_CHUNK = 128
CMP_LEN = 32
CMP_STRIDE = 16
SEL_LEN = 64
SEL_TOP = 16
WINDOW = 512
POOL_SIZES = (2, 4, 8, 16)
FFN_RES_WEIGHT = 0.5
EPS = 1e-6
NEG_INF = -1e30
SEL_FORCE = 1e3
ATTN_SCALE = HEAD_DIM ** -0.5
LOG2E = 1.4426950408889634

LANES = 128
VMEM_LIMIT = 52 * 1024 * 1024
FFN_TOKENS = 1024
FFN_MIX_TOKENS = 1024
FFN_HIDDEN = 256
FOX_PREP_ROWS = 512
FOX_TILE = 1024
NSA_TILE = 256
NSA_WINDOW_TILE = 256
NSA_TILES_PER_STEP = 8
ONE_LANE = HEAD_DIM


def _params(*sem):
    return pltpu.CompilerParams(dimension_semantics=sem, vmem_limit_bytes=VMEM_LIMIT)


def _dot(a, b, **kw):
    return jnp.dot(a, b, preferred_element_type=F32, **kw)


def _dot_nt(a, b):
    return lax.dot_general(a, b, (((1,), (1,)), ((), ())), preferred_element_type=F32)


def _rms(x):
    return x * lax.rsqrt(jnp.mean(x * x, axis=-1, keepdims=True) + EPS)


def _group_rms(x, gmean):
    ms = _dot((x * x).astype(BF16), gmean)
    return x * lax.rsqrt(ms + EPS)


def _rope(x, c, s_up, s_dn):
    w = x.shape[-1]
    return x * c + pltpu.roll(x, w - ROPE_HALF, axis=1) * s_up + pltpu.roll(x, ROPE_HALF, axis=1) * s_dn


_GELU_C = 0.7978845608028654


def _gelu(x):
    half = 0.5 * x
    return half + half * jnp.tanh(x * (_GELU_C + (_GELU_C * 0.044715) * (x * x)))


def _log_sigmoid(x):
    return jnp.minimum(x, 0.0) - jnp.log(1.0 + jnp.exp(-jnp.abs(x)))


def _head_column(x, h):
    col = x[:, (h // 2) * LANES:(h // 2 + 1) * LANES]
    return pltpu.roll(col, HEAD_DIM, axis=1) if h % 2 else col


def _pair_heads(even, odd):
    lane = lax.broadcasted_iota(jnp.int32, even.shape, 1)
    return jnp.where(lane < HEAD_DIM, even, pltpu.roll(odd, HEAD_DIM, axis=1))


def _ones_lane(shape):
    return jnp.where(lax.broadcasted_iota(jnp.int32, shape, 1) == ONE_LANE, 1.0, 0.0)


def _ffn_kernel(x_ref, g_ref, w1_ref, w3_ref, w2_ref, *rest):
    *mix_refs, o_ref = rest
    x = x_ref[...]
    if mix_refs:
        *mixers, wout_ref = mix_refs
        for k, ref in enumerate(mixers):
            x = x + _dot(ref[...], wout_ref[k * W_MIX:(k + 1) * W_MIX, :])
    h = (_rms(x) * g_ref[...]).astype(BF16)
    acc = jnp.zeros_like(x)
    for c in range(w1_ref.shape[1] // FFN_HIDDEN):
        cs = slice(c * FFN_HIDDEN, (c + 1) * FFN_HIDDEN)
        a = _dot(h, w1_ref[:, cs])
        b = _dot(h, w3_ref[:, cs])
        acc = acc + _dot((a * jax.nn.sigmoid(a) * b).astype(BF16), w2_ref[cs, :])
    o_ref[...] = x + FFN_RES_WEIGHT * acc


def _ffn(x, g, w1, w3, w2, mixers=(), w_out=None):
    n, d = x.shape
    tm = min(FFN_MIX_TOKENS if mixers else FFN_TOKENS, n)
    resident = lambda shape: pl.BlockSpec(shape, lambda i: (0, 0), pipeline_mode=pl.Buffered(1))
    mix_specs = [pl.BlockSpec((tm, W_MIX), lambda i: (i, 0)) for _ in mixers] + ([resident(w_out.shape)] if mixers else [])
    mix_args = list(mixers) + ([w_out.astype(BF16)] if mixers else [])
    return pl.pallas_call(
        _ffn_kernel,
        grid=(n // tm,),
        in_specs=[pl.BlockSpec((tm, d), lambda i: (i, 0)), resident((1, d)),
                  resident(w1.shape), resident(w3.shape), resident(w2.shape)] + mix_specs,
        out_specs=pl.BlockSpec((tm, d), lambda i: (i, 0)),
        out_shape=jax.ShapeDtypeStruct((n, d), F32),
        compiler_params=_params("parallel"),
        name="ffn_mix" if mixers else "ffn",
    )(x, g.reshape(1, d), w1.astype(BF16), w3.astype(BF16), w2.astype(BF16), *mix_args)


_IN_GROUPS = (768, 512, 256, 256, 256, 128, 128)
_MISC_FORGET = 0
_MISC_GATE = 4


def _inproj_kernel(x_ref, g_ref, w_ref, fqkv_ref, guv_ref, nq_ref, nkv_ref, pz_ref, kc_ref, vc_ref, misc_ref, kcvc_s):
    h = (_rms(x_ref[...]) * g_ref[...]).astype(BF16)
    off = 0
    for ref, width in zip((fqkv_ref, guv_ref, nq_ref, nkv_ref, pz_ref), _IN_GROUPS[:5]):
        ref[...] = _dot(h, w_ref[:, off:off + width])
        off += width
    tail = _dot(h, w_ref[:, off:off + 2 * LANES])
    misc_ref[...] = tail[:, LANES:]
    kcvc_s[...] = tail[:, :LANES]
    groups = x_ref.shape[0] // CMP_STRIDE
    lane = lax.broadcasted_iota(jnp.int32, (groups, LANES), 1)
    for j in range(0, CMP_STRIDE, 2):
        even = kcvc_s[pl.ds(j, groups, stride=CMP_STRIDE), :]
        odd = kcvc_s[pl.ds(j + 1, groups, stride=CMP_STRIDE), :]
        col = slice((j // 2) * LANES, (j // 2 + 1) * LANES)
        kc_ref[:, col] = jnp.where(lane < HEAD_DIM, even, pltpu.roll(odd, HEAD_DIM, axis=1))
        vc_ref[:, col] = jnp.where(lane < HEAD_DIM, pltpu.roll(even, HEAD_DIM, axis=1), odd)


def _relayout_w_in(w_in):
    o = np.cumsum((0, 256, 256, 256, 4, 256, 256, 256, 64, 64, 64, 64, 64, 64, 12, 256))
    fq, ff, gu, nq, nkc, nks, ng, pz, end = o[0], o[3], o[4], o[6], o[7], o[9], o[13], o[14], o[15]
    d = w_in.shape[0]
    pad = jnp.zeros((d, LANES - 16), w_in.dtype)
    return jnp.concatenate([
        w_in[:, fq:ff], w_in[:, gu:nq], w_in[:, nq:nkc], w_in[:, nks:ng], w_in[:, pz:end],
        w_in[:, nkc:nks], w_in[:, ff:gu], w_in[:, ng:pz], pad], axis=1)


def _inproj(x, g, w_in):
    n, d = x.shape
    tm = min(FFN_TOKENS, n)
    assert tm % (8 * CMP_STRIDE) == 0
    w = _relayout_w_in(w_in).astype(BF16)
    flat = CMP_STRIDE * HEAD_DIM
    outs = [(tm, n, wd) for wd in (768, 512, 256, 256, 256)] + [(tm // CMP_STRIDE, n // CMP_STRIDE, flat)] * 2 + [(tm, n, LANES)]
    return pl.pallas_call(
        _inproj_kernel,
        grid=(n // tm,),
        in_specs=[
            pl.BlockSpec((tm, d), lambda i: (i, 0)),
            pl.BlockSpec((1, d), lambda i: (0, 0)),
            pl.BlockSpec(w.shape, lambda i: (0, 0)),
        ],
        out_specs=[pl.BlockSpec((rows, wd), lambda i: (i, 0)) for rows, _, wd in outs],
        out_shape=[jax.ShapeDtypeStruct((total, wd), F32) for _, total, wd in outs],
        scratch_shapes=[pltpu.VMEM((tm, LANES), F32)],
        compiler_params=_params("parallel"),
        name="inproj",
    )(x, g.reshape(1, d), w)


def _bf16_pieces(x):
    p1 = x.astype(BF16).astype(F32)
    r1 = x - p1
    p2 = r1.astype(BF16).astype(F32)
    p3 = (r1 - p2).astype(BF16).astype(F32)
    return p1, p2, p3


_N_PIECES = 3
_N_EXTRA = _N_PIECES * N_HEADS
_PACK_ONE = _N_EXTRA


def _fox_placement():
    pq = np.zeros((LANES, LANES), np.float32)
    pk = np.zeros((LANES, LANES), np.float32)
    for h in range(N_HEADS):
        for p in range(_N_PIECES):
            pq[p * N_HEADS + h, HEAD_DIM + _N_PIECES * h + p] = 1.0
            pk[p * N_HEADS + h, HEAD_DIM + _N_EXTRA + _N_PIECES * h + p] = -1.0
    pq[_PACK_ONE, HEAD_DIM + _N_EXTRA:HEAD_DIM + 2 * _N_EXTRA] = 1.0
    pk[_PACK_ONE, HEAD_DIM:HEAD_DIM + _N_EXTRA] = 1.0
    return jnp.asarray(np.concatenate([pq, pk], axis=1), BF16)


def _fox_kernel(qkv_ref, misc_ref, fbt_ref, gq_ref, gk_ref, gmean_ref, pqk_ref, o_ref, q_s, k_s, v_s, o_s):
    t = qkv_ref.shape[0]
    pr = min(FOX_PREP_ROWS, t)
    gmean = gmean_ref[...]
    upper = (lax.broadcasted_iota(jnp.int32, (pr, pr), 0) <= lax.broadcasted_iota(jnp.int32, (pr, pr), 1)).astype(BF16)
    row8 = lax.broadcasted_iota(jnp.int32, (8, pr), 0)
    lane = lax.broadcasted_iota(jnp.int32, (pr, LANES), 1)
    extra = lane - HEAD_DIM
    own = [((extra >= _N_PIECES * h) & (extra < _N_PIECES * (h + 1)))
           | ((extra >= _N_EXTRA + _N_PIECES * h) & (extra < _N_EXTRA + _N_PIECES * (h + 1))) for h in range(N_HEADS)]
    ones_v = jnp.where(extra == 0, 1.0, 0.0)
    carry = jnp.zeros((8, pr), F32)
    for r in range(t // pr):
        sl = slice(r * pr, (r + 1) * pr)
        logits_t = misc_ref[sl, :].T[0:8, :]
        log_f = _log_sigmoid(logits_t + fbt_ref[...]) * LOG2E
        c = sum(_dot(piece.astype(BF16), upper) for piece in _bf16_pieces(log_f)) + carry
        carry = jnp.broadcast_to(c[:, pr - 1:pr], (8, pr))
        p1, p2, p3 = _bf16_pieces(c)
        lo = jnp.where(row8 < N_HEADS, p1, pltpu.roll(p2, N_HEADS, axis=0))
        hi = jnp.where(row8 < N_HEADS, p3, jnp.where(row8 == _PACK_ONE - 8, 1.0, 0.0))
        packed = jnp.concatenate([lo, hi, jnp.zeros((LANES - 16, pr), F32)], axis=0).T.astype(BF16)
        q_extra = _dot(packed, pqk_ref[:, :LANES])
        k_extra = _dot(packed, pqk_ref[:, LANES:])
        qn = _group_rms(qkv_ref[sl, 0:W_MIX], gmean) * gq_ref[...] * (ATTN_SCALE * LOG2E)
        kn = _group_rms(qkv_ref[sl, W_MIX:2 * W_MIX], gmean) * gk_ref[...]
        v = qkv_ref[sl, 2 * W_MIX:3 * W_MIX]
        for h in range(N_HEADS):
            q_s[h, sl, :] = jnp.where(extra < 0, _head_column(qn, h), q_extra).astype(BF16)
            k_s[h, sl, :] = jnp.where(extra < 0, _head_column(kn, h), jnp.where(own[h], k_extra, 0.0)).astype(BF16)
            v_aug = jnp.where(extra < 0, _head_column(v, h), ones_v)
            v_s[h, :, sl] = v_aug.T.astype(BF16)

    tq = min(FOX_TILE, t)
    causal = lax.broadcasted_iota(jnp.int32, (tq, tq), 0) <= lax.broadcasted_iota(jnp.int32, (tq, tq), 1)

    def one_head(h):
        for qi in range(t // tq):
            dsl = slice(qi * tq, (qi + 1) * tq)
            q = q_s[h, dsl, :]
            s_d = jnp.where(causal, _dot_nt(k_s[h, dsl, :], q), NEG_INF)
            m = jnp.max(s_d, axis=0, keepdims=True)
            if qi:
                s_o = _dot_nt(k_s[h, 0:qi * tq, :], q)
                m = jnp.maximum(m, jnp.max(s_o, axis=0, keepdims=True))
            acc = _dot(v_s[h, :, dsl], jnp.exp2(s_d - m).astype(BF16))
            if qi:
                acc = acc + _dot(v_s[h, :, 0:qi * tq], jnp.exp2(s_o - m).astype(BF16))
            o_s[h, dsl, :] = (acc / acc[ONE_LANE:ONE_LANE + 1, :]).T

    for h in range(N_HEADS):
        one_head(h)
    for hp in range(N_HEADS // 2):
        o_ref[:, hp * LANES:(hp + 1) * LANES] = _pair_heads(o_s[2 * hp], o_s[2 * hp + 1]).astype(o_ref.dtype)


def _group_mean_matrix(width):
    g = np.kron(np.eye(width // HEAD_DIM, dtype=np.float32), np.full((HEAD_DIM, HEAD_DIM), 1.0 / HEAD_DIM, np.float32))
    return jnp.asarray(g, BF16)


def _fox(fqkv, misc, f_bias, g_q, g_k, b, t):
    assert t % min(FOX_TILE, t) == 0 and t % min(FOX_PREP_ROWS, t) == 0
    pr = min(FOX_PREP_ROWS, t)
    assert _MISC_FORGET == 0 and N_HEADS <= 4 and pr % LANES == 0
    fbt = jnp.broadcast_to(jnp.zeros((8,), F32).at[:N_HEADS].set(f_bias)[:, None], (8, pr))
    gq = jnp.tile(g_q, N_HEADS).reshape(1, W_MIX)
    gk = jnp.tile(g_k, N_HEADS).reshape(1, W_MIX)
    const = _const_spec
    head_scratch = lambda dtype: pltpu.VMEM((N_HEADS, t, LANES), dtype)
    return _Part(
        _fox_kernel,
        [fqkv, misc, fbt, gq, gk, _group_mean_matrix(W_MIX), _fox_placement()],
        [pl.BlockSpec((t, 3 * W_MIX), lambda i: (i, 0)), pl.BlockSpec((t, LANES), lambda i: (i, 0)),
         const((8, pr)), const((1, W_MIX)), const((1, W_MIX)), const((W_MIX, W_MIX)), const((LANES, 2 * LANES))],
        [pl.BlockSpec((t, W_MIX), lambda i: (i, 0))],
        [jax.ShapeDtypeStruct((b * t, W_MIX), BF16)],
        (head_scratch(BF16), head_scratch(BF16), pltpu.VMEM((N_HEADS, LANES, t), BF16), head_scratch(F32)))


def _gmlp_kernel(uv_ref, gv_ref, gmean_ref, w_ref, bias_ref, o_ref):
    c = GMLP_CHUNK
    rows = lax.broadcasted_iota(jnp.int32, (c, c), 0)
    cols = lax.broadcasted_iota(jnp.int32, (c, c), 1)
    lane_group = lax.broadcasted_iota(jnp.int32, (c, W_MIX), 1) // HEAD_DIM
    w_cat = jnp.concatenate([jnp.where(cols <= rows, w_ref[g], 0.0).astype(BF16) for g in range(N_HEADS)], axis=1)
    v_all = (_group_rms(_gelu(uv_ref[:, W_MIX:2 * W_MIX]), gmean_ref[...]) * gv_ref[...]).astype(BF16)
    for r in range(uv_ref.shape[0] // c):
        sl = slice(r * c, (r + 1) * c)
        u = _gelu(uv_ref[sl, 0:W_MIX])
        v = v_all[sl]
        v_stack = jnp.concatenate([jnp.where(lane_group == g, v, jnp.zeros_like(v)) for g in range(N_HEADS)], axis=0)
        o_ref[sl, :] = (u * (bias_ref[...] + _dot(w_cat, v_stack))).astype(o_ref.dtype)


class _Part(NamedTuple):
    body: Callable
    args: list
    in_specs: list
    out_specs: list
    out_shape: list
    scratch_shapes: tuple = ()


def _const_spec(shape):
    return pl.BlockSpec(shape, lambda i: (0,) * len(shape))


def _run_parts(parts, b, name):
    n_in = [len(p.in_specs) for p in parts]
    n_out = [len(p.out_specs) for p in parts]
    n_scr = [len(p.scratch_shapes) for p in parts]

    def body(*refs):
        ins, outs, scr = refs[:sum(n_in)], refs[sum(n_in):sum(n_in) + sum(n_out)], refs[sum(n_in) + sum(n_out):]
        for p, i0, o0, s0 in zip(parts, np.cumsum([0] + n_in), np.cumsum([0] + n_out), np.cumsum([0] + n_scr)):
            p.body(*ins[i0:i0 + len(p.in_specs)], *outs[o0:o0 + len(p.out_specs)], *scr[s0:s0 + len(p.scratch_shapes)])

    outs = pl.pallas_call(
        body,
        grid=(b,),
        in_specs=[s for p in parts for s in p.in_specs],
        out_specs=[s for p in parts for s in p.out_specs],
        out_shape=[s for p in parts for s in p.out_shape],
        scratch_shapes=[s for p in parts for s in p.scratch_shapes],
        compiler_params=_params("parallel"),
        name=name,
    )(*[a for p in parts for a in p.args])
    return [outs[o0:o0 + k] for o0, k in zip(np.cumsum([0] + n_out), n_out)]


def _gmlp(guv, g_v, w_s, b_s, b, t):
    bias = jnp.repeat(b_s.T, HEAD_DIM, axis=1)
    return _Part(
        _gmlp_kernel,
        [guv, g_v.reshape(1, W_MIX), _group_mean_matrix(W_MIX), w_s, bias],
        [pl.BlockSpec((t, 2 * W_MIX), lambda i: (i, 0)), _const_spec((1, W_MIX)), _const_spec((W_MIX, W_MIX)),
         _const_spec(w_s.shape), _const_spec(bias.shape)],
        [pl.BlockSpec((t, W_MIX), lambda i: (i, 0))],
        [jax.ShapeDtypeStruct((b * t, W_MIX), BF16)])


def _pool_kernel(z_ref, inv_cnt_ref, w_ref, scale_ref, o_ref):
    t = z_ref.shape[0]
    z = z_ref[...]
    row = lax.broadcasted_iota(jnp.int32, (t, W_MIX), 0)
    lane_group = lax.broadcasted_iota(jnp.int32, (t, W_MIX), 1) // HEAD_DIM
    sums = {}
    s = z
    k = 1
    while k < max(POOL_SIZES):
        s = s + jnp.where(row >= k, pltpu.roll(s, k, axis=0), 0.0)
        k *= 2
        sums[k] = s
    win_sum = sums[POOL_SIZES[-1]]
    for g in range(len(POOL_SIZES) - 2, -1, -1):
        win_sum = jnp.where(lane_group == g, sums[POOL_SIZES[g]], win_sum)
    pooled = win_sum * inv_cnt_ref[...] - z
    o_ref[...] = (_dot(pooled.astype(BF16), w_ref[...]) * scale_ref[...]).astype(o_ref.dtype)


def _pool(pz, w_p, scale, b, t):
    w_bd = jax.scipy.linalg.block_diag(*[w_p[g] for g in range(N_HEADS)]).astype(BF16)
    win = jnp.repeat(jnp.array(POOL_SIZES, jnp.int32), HEAD_DIM)
    inv_cnt = 1.0 / jnp.minimum(jnp.arange(t)[:, None] + 1, win[None, :]).astype(F32)
    return _Part(
        _pool_kernel,
        [pz, inv_cnt, w_bd, scale.reshape(1, W_MIX)],
        [pl.BlockSpec((t, W_MIX), lambda i: (i, 0)), _const_spec((t, W_MIX)), _const_spec((W_MIX, W_MIX)),
         _const_spec((1, W_MIX))],
        [pl.BlockSpec((t, W_MIX), lambda i: (i, 0))],
        [jax.ShapeDtypeStruct((b * t, W_MIX), BF16)])


def _nsa_prep_kernel(kv_ref, kc_ref, vc_ref, posk_ref, posv_ref, kw1_ref, vw1_ref, kw2_ref, vw2_ref,
                     gkv_ref, nmask_ref, gmean_ref, c_ref, su_ref, sd_ref, gc_ref, cc_ref, csu_ref, csd_ref,
                     ks_ref, vs_ref, kw_ref, vw_ref, kcmp_ref, vcmp_ref):
    t = kv_ref.shape[0]
    x = kv_ref[...]
    normed = _group_rms(x, gmean_ref[...]) * gkv_ref[...]
    y = _rope(jnp.where(nmask_ref[...] > 0.5, normed, x), c_ref[...], su_ref[...], sd_ref[...])
    lane = lax.broadcasted_iota(jnp.int32, (t, LANES), 1)
    block = lax.broadcasted_iota(jnp.int32, (t, LANES), 0) // SEL_LEN
    key = lane < HEAD_DIM
    ks_v, kw_v = y[:, 0:LANES], y[:, LANES:2 * LANES]
    ks_ref[...] = jnp.where(key, ks_v, jnp.where(lane - HEAD_DIM == block, 1.0, 0.0)).astype(BF16)
    vs_ref[...] = jnp.where(key, pltpu.roll(ks_v, HEAD_DIM, axis=1), _ones_lane((t, LANES))).astype(BF16)
    kw_ref[...] = jnp.where(key, kw_v, 0.0).astype(BF16)
    vw_ref[...] = jnp.where(key, pltpu.roll(kw_v, HEAD_DIM, axis=1), _ones_lane((t, LANES))).astype(BF16)

    half = kc_ref.shape[1]
    nb = kc_ref.shape[0]

    def hidden(x_ref, pos_ref, w1_ref):
        top = _dot((x_ref[...] + pos_ref[:, 0:half]).astype(BF16), w1_ref[0:half, :])
        bot = _dot((x_ref[...] + pos_ref[:, half:2 * half]).astype(BF16), w1_ref[half:2 * half, :])
        return _gelu(top + pltpu.roll(bot, nb - 1, axis=0)).astype(BF16)

    kv_cmp = _dot(hidden(kc_ref, posk_ref, kw1_ref), kw2_ref[...]) + _dot(hidden(vc_ref, posv_ref, vw1_ref), vw2_ref[...])
    key = lax.broadcasted_iota(jnp.int32, kv_cmp.shape, 1) < HEAD_DIM
    normed = _group_rms(kv_cmp, gmean_ref[0:LANES, 0:LANES]) * gc_ref[...]
    y = _rope(jnp.where(key, normed, kv_cmp), cc_ref[...], csu_ref[...], csd_ref[...])
    kcmp_ref[...] = jnp.where(key, y, 0.0).astype(BF16)
    vcmp_ref[...] = jnp.where(key, pltpu.roll(y, HEAD_DIM, axis=1), _ones_lane(y.shape)).T.astype(BF16)


def _rope_tables(pos):
    inv = ROPE_THETA ** (-jnp.arange(ROPE_HALF, dtype=F32) * 2.0 / ROPE_DIM)
    ang = pos.astype(F32)[:, None] * inv[None, :]
    cos, sin = jnp.cos(ang), jnp.sin(ang)
    n = pos.shape[0]
    zero = jnp.zeros((n, ROPE_HALF), F32)
    rest0 = jnp.zeros((n, HEAD_DIM - ROPE_DIM), F32)
    c = jnp.concatenate([cos, cos, rest0 + 1.0], axis=1)
    s_up = jnp.concatenate([-sin, zero, rest0], axis=1)
    s_dn = jnp.concatenate([zero, sin, rest0], axis=1)
    return c, s_up, s_dn


def _identity_tables(n):
    return jnp.ones((n, HEAD_DIM), F32), jnp.zeros((n, HEAD_DIM), F32), jnp.zeros((n, HEAD_DIM), F32)


def _nsa_prep(nkv, kc2, vc2, g_kc, g_ks, g_kw, pos_k, k_w1, k_w2, pos_v, v_w1, v_w2, b, t):
    nb = t // CMP_STRIDE
    assert HEAD_DIM + t // SEL_LEN <= LANES
    flat = CMP_STRIDE * HEAD_DIM
    rope_t = _rope_tables(jnp.arange(t))
    iden_t = _identity_tables(t)
    tabs = [jnp.concatenate([r, i, r, i], axis=1) for r, i in zip(rope_t, iden_t)]
    rope_c = _rope_tables(jnp.arange(nb) * CMP_STRIDE + CMP_LEN - 1)
    iden_c = _identity_tables(nb)
    tabs_c = [jnp.concatenate([r, i], axis=1) for r, i in zip(rope_c, iden_c)]
    one = jnp.ones((HEAD_DIM,), F32)
    gkv = jnp.concatenate([g_ks, one, g_kw, one]).reshape(1, W_MIX)
    nmask = jnp.concatenate([one, 0 * one, one, 0 * one]).reshape(1, W_MIX)
    gc = jnp.concatenate([g_kc, one]).reshape(1, LANES)
    zpad = jnp.zeros_like(k_w2)
    kw2 = jnp.concatenate([k_w2, zpad], axis=1).astype(BF16)
    vw2 = jnp.concatenate([zpad, v_w2], axis=1).astype(BF16)
    const = _const_spec
    seq = lambda width: pl.BlockSpec((t, width), lambda i: (i, 0))
    cmp_in = pl.BlockSpec((nb, flat), lambda i: (i, 0))
    cmp_out = pl.BlockSpec((nb, LANES), lambda i: (i, 0))
    return _Part(
        _nsa_prep_kernel,
        [nkv, kc2, vc2, pos_k.reshape(1, 2 * flat), pos_v.reshape(1, 2 * flat),
         k_w1.astype(BF16), v_w1.astype(BF16), kw2, vw2, gkv, nmask, _group_mean_matrix(W_MIX), *tabs, gc, *tabs_c],
        [seq(W_MIX), cmp_in, cmp_in, const((1, 2 * flat)), const((1, 2 * flat)),
         const(k_w1.shape), const(v_w1.shape), const(kw2.shape), const(vw2.shape),
         const((1, W_MIX)), const((1, W_MIX)), const((W_MIX, W_MIX)),
         const((t, W_MIX)), const((t, W_MIX)), const((t, W_MIX)),
         const((1, LANES)), const((nb, LANES)), const((nb, LANES)), const((nb, LANES))],
        [seq(LANES)] * 4 + [cmp_out, pl.BlockSpec((LANES, nb), lambda i: (i, 0))],
        [jax.ShapeDtypeStruct((b * t, LANES), BF16)] * 4
        + [jax.ShapeDtypeStruct((b * nb, LANES), BF16), jax.ShapeDtypeStruct((b * LANES, nb), BF16)])


_NSA_ROW_BLOCKED = (0, 1, 5, 6, 7, 15)


def _nsa_kernel(*refs, n_top, tq):
    q_ref, ks_ref = refs[0], refs[11]
    per_step = q_ref.shape[0] // tq

    def variant(step):
        for s in range(per_step):
            sub = [r.at[pl.ds(s * tq, tq)] if i in _NSA_ROW_BLOCKED else r for i, r in enumerate(refs)]
            _nsa_tile(*sub, n_top=n_top, qi=step * per_step + s)

    lax.switch(pl.program_id(1), [functools.partial(variant, k) for k in range(ks_ref.shape[0] // q_ref.shape[0])])


def _nsa_tile(q_ref, misc_ref, gb_ref, gq_ref, gmean_ref, c_ref, su_ref, sd_ref, ovt_ref,
              kcmp_ref, vcmp_ref, ks_ref, vs_ref, kw_ref, vw_ref, o_ref, *, n_top, qi):
    tq = q_ref.shape[0]
    t0 = qi * tq
    nb = kcmp_ref.shape[0]
    nsel = ks_ref.shape[0] // SEL_LEN
    rows = N_HEADS * tq

    qn = _rope(_group_rms(q_ref[...], gmean_ref[...]) * gq_ref[...], c_ref[...], su_ref[...], sd_ref[...]) * (ATTN_SCALE * LOG2E)
    q_cols = [_head_column(qn, h) for h in range(N_HEADS)]
    lane_q = lax.broadcasted_iota(jnp.int32, (tq, LANES), 1)

    q0 = jnp.concatenate([jnp.where(lane_q < HEAD_DIM, col, 0.0) for col in q_cols], axis=0).astype(BF16)
    cmp_end = lax.broadcasted_iota(jnp.int32, (nb, rows), 0) * CMP_STRIDE + (CMP_LEN - 1)
    qpos_c = (lax.broadcasted_iota(jnp.int32, (nb, rows), 1) & (tq - 1)) + t0
    m_cmp = cmp_end <= qpos_c
    s = jnp.where(m_cmp, _dot_nt(kcmp_ref[...], q0), NEG_INF)
    e = jnp.where(m_cmp, jnp.exp2(s - jnp.max(s, axis=0, keepdims=True)), 0.0)
    denom = jnp.sum(e, axis=0, keepdims=True)
    inv = 1.0 / jnp.where(denom > 0.0, denom, 1.0)
    p_cmp = e * inv
    o_cmp = (_dot(vcmp_ref[...], e.astype(BF16)) * inv).T

    nsp = -(-nsel // 8) * 8
    blk = lax.broadcasted_iota(jnp.int32, (nsp, tq), 0)
    cur = (lax.broadcasted_iota(jnp.int32, (nsp, tq), 1) + t0) // SEL_LEN
    valid = blk <= cur
    n_reachable = (t0 + tq - 1) // SEL_LEN + 1
    if n_reachable <= n_top:
        keep = valid
    else:
        p_sum = p_cmp[:, 0:tq] + p_cmp[:, tq:2 * tq] + p_cmp[:, 2 * tq:3 * tq] + p_cmp[:, 3 * tq:4 * tq]
        imp = _dot(ovt_ref[0:nsp, :], p_sum, precision=HIGHEST)
        forced = ((blk == 0) | (blk == cur) | (blk == cur - 1)).astype(F32)
        imp = jnp.where(valid, imp + SEL_FORCE * forced, NEG_INF)
        cur_row = (lax.broadcasted_iota(jnp.int32, (1, tq), 1) + t0) // SEL_LEN
        outranked = jnp.zeros((nsp, tq), F32)
        for j in range(min(nsel, n_reachable)):
            rival = imp[j:j + 1, :]
            wins = (rival > imp) | ((rival == imp) & (blk > j))
            outranked = outranked + jnp.where(wins & (cur_row >= j), 1.0, 0.0)
        keep = (outranked < n_top) & valid
    bias_t = jnp.where(keep | (blk >= nsel), 0.0, NEG_INF)
    if nsp < LANES:
        bias_t = jnp.concatenate([bias_t, jnp.zeros((LANES - nsp, tq), F32)], axis=0)
    bias = jnp.concatenate([bias_t[:, c * LANES:(c + 1) * LANES].T for c in range(tq // LANES)], axis=0)
    bias = pltpu.roll(bias, HEAD_DIM, axis=1)
    q = jnp.concatenate([jnp.where(lane_q < HEAD_DIM, col, bias) for col in q_cols], axis=0).astype(BF16)

    trow = lax.broadcasted_iota(jnp.int32, (rows, tq), 0) & (tq - 1)
    tcol = lax.broadcasted_iota(jnp.int32, (rows, tq), 1)
    causal_bias = jnp.where(tcol <= trow, 0.0, NEG_INF)

    def tile(j):
        return slice(j * tq, (j + 1) * tq)

    def sel_step(kj, extra_bias, st):
        m, acc = st
        s = _dot_nt(q, ks_ref[tile(kj), :])
        if extra_bias is not None:
            s = s + extra_bias
        m_new = jnp.maximum(m, jnp.max(s, axis=-1, keepdims=True))
        acc = jnp.exp2(m - m_new) * acc + _dot(jnp.exp2(s - m_new).astype(BF16), vs_ref[tile(kj), :])
        return m_new, acc

    st = (jnp.full((rows, 1), NEG_INF, F32), jnp.zeros((rows, LANES), F32))
    for kj in range(qi):
        st = sel_step(kj, None, st)
    _, acc = sel_step(qi, causal_bias, st)
    o_sel = acc / acc[:, ONE_LANE:ONE_LANE + 1]

    wt = min(NSA_WINDOW_TILE, tq)
    span = WINDOW // wt
    wrow = lax.broadcasted_iota(jnp.int32, (N_HEADS * wt, wt), 0) & (wt - 1)
    wcol = lax.broadcasted_iota(jnp.int32, (N_HEADS * wt, wt), 1)
    o_win = []
    for sub in range(tq // wt):
        q_sub = jnp.concatenate([q[h * tq + sub * wt:h * tq + (sub + 1) * wt] for h in range(N_HEADS)], axis=0)
        diag = qi * (tq // wt) + sub
        parts = []
        for d in range(min(span, diag), -1, -1):
            ksl = slice((diag - d) * wt, (diag - d + 1) * wt)
            s = _dot_nt(q_sub, kw_ref[ksl, :])
            if d == 0:
                s = s + jnp.where(wcol <= wrow, 0.0, NEG_INF)
            elif d == span:
                s = s + jnp.where(wcol > wrow, 0.0, NEG_INF)
            parts.append((s, ksl))
        m = functools.reduce(jnp.maximum, [jnp.max(s, axis=-1, keepdims=True) for s, _ in parts])
        acc = sum(_dot(jnp.exp2(s - m).astype(BF16), vw_ref[ksl, :]) for s, ksl in parts)
        o_win.append(acc / acc[:, ONE_LANE:ONE_LANE + 1])

    gate = jax.nn.sigmoid(misc_ref[...] + gb_ref[...])
    outs = []
    for h in range(N_HEADS):
        hr = slice(h * tq, (h + 1) * tq)
        g0 = _MISC_GATE + 3 * h
        o_win_h = jnp.concatenate([part[h * wt:(h + 1) * wt] for part in o_win], axis=0)
        outs.append(gate[:, g0:g0 + 1] * o_cmp[hr] + gate[:, g0 + 1:g0 + 2] * o_sel[hr] + gate[:, g0 + 2:g0 + 3] * o_win_h)
    for hp in range(N_HEADS // 2):
        o_ref[:, hp * LANES:(hp + 1) * LANES] = _pair_heads(outs[2 * hp], outs[2 * hp + 1]).astype(o_ref.dtype)


def _cmp_to_sel_overlap_t(nb, nsel):
    nc = nb - 1
    cs = np.arange(nc) * CMP_STRIDE
    ss = np.arange(nsel) * SEL_LEN
    ov = np.clip(np.minimum(cs[:, None] + CMP_LEN, ss[None, :] + SEL_LEN) - np.maximum(cs[:, None], ss[None, :]), 0, None)
    out = np.zeros((LANES, nb), np.float32)
    out[:nsel, :nc] = (ov / CMP_LEN).T
    return jnp.asarray(out)


def _nsa(nq, misc, gate_b, g_q, ks, vs, kw, vw, kcmp, vcmp, b, t):
    tq = min(NSA_TILE, t)
    nb = t // CMP_STRIDE
    nsel = t // SEL_LEN
    wt = min(NSA_WINDOW_TILE, tq)
    assert HEAD_DIM + nsel <= LANES and nb % 8 == 0 and tq & (tq - 1) == 0 and tq % LANES == 0
    assert WINDOW % wt == 0 and tq % wt == 0 and wt & (wt - 1) == 0
    gb = jnp.zeros((1, LANES), F32).at[0, _MISC_GATE:_MISC_GATE + 3 * N_HEADS].set(gate_b)
    gq = jnp.tile(g_q, N_HEADS).reshape(1, W_MIX)
    tabs = [jnp.tile(x, (1, N_HEADS)) for x in _rope_tables(jnp.arange(t))]
    rows = tq * min(NSA_TILES_PER_STEP, t // tq)
    assert t % rows == 0
    steps = t // rows
    const = lambda shape: pl.BlockSpec(shape, lambda i, j: (0,) * len(shape))
    qtile = lambda width: pl.BlockSpec((rows, width), lambda i, j: (i * steps + j, 0))
    ptile = pl.BlockSpec((rows, W_MIX), lambda i, j: (j, 0))
    seq = pl.BlockSpec((t, LANES), lambda i, j: (i, 0))
    cmp = pl.BlockSpec((nb, LANES), lambda i, j: (i, 0))
    return pl.pallas_call(
        functools.partial(_nsa_kernel, n_top=min(SEL_TOP, nsel), tq=tq),
        grid=(b, steps),
        in_specs=[qtile(W_MIX), qtile(LANES), const((1, LANES)), const((1, W_MIX)), const((W_MIX, W_MIX)),
                  ptile, ptile, ptile, const((LANES, nb)), cmp, pl.BlockSpec((LANES, nb), lambda i, j: (i, 0)),
                  seq, seq, seq, seq],
        out_specs=qtile(W_MIX),
        out_shape=jax.ShapeDtypeStruct((b * t, W_MIX), BF16),
        compiler_params=_params("parallel", "arbitrary"),
        name="nsa",
    )(nq, misc, gb, gq, _group_mean_matrix(W_MIX), *tabs, _cmp_to_sel_overlap_t(nb, nsel),
      kcmp, vcmp, ks, vs, kw, vw)


def kernel(x, ffn1_norm, ffn1_w1, ffn1_w3, ffn1_w2, mix_norm, w_in, w_out, fox_f_bias, fox_q_norm, fox_k_norm, gmlp_v_norm, gmlp_w_s, gmlp_b_s, nsa_q_norm, nsa_kc_norm, nsa_ks_norm, nsa_kw_norm, nsa_cmp_pos_k, nsa_cmp_k_w1, nsa_cmp_k_w2, nsa_cmp_pos_v, nsa_cmp_v_w1, nsa_cmp_v_w2, nsa_gate_bias, pool_w, pool_scale, ffn2_norm, ffn2_w1, ffn2_w3, ffn2_w2):
    b, t, d = x.shape
    assert t % GMLP_CHUNK == 0 and t >= CMP_LEN
    n = b * t
    xf = x.reshape(n, d)
    for l in range(ffn1_norm.shape[0]):
        xf = _ffn(xf, ffn1_norm[l], ffn1_w1[l], ffn1_w3[l], ffn1_w2[l])
        fqkv, guv, nq, nkv, pz, kc, vc, misc = _inproj(xf, mix_norm[l], w_in[l])
        ((o_a,),) = _run_parts([_fox(fqkv, misc, fox_f_bias[l], fox_q_norm[l], fox_k_norm[l], b, t)], b, "fox")
        (o_b,), (o_d,), (ks, vs, kw, vw, kcmp, vcmp) = _run_parts([
            _gmlp(guv, gmlp_v_norm[l], gmlp_w_s[l], gmlp_b_s[l], b, t),
            _pool(pz, pool_w[l], pool_scale[l], b, t),
            _nsa_prep(nkv, kc, vc, nsa_kc_norm[l], nsa_ks_norm[l], nsa_kw_norm[l],
                      nsa_cmp_pos_k[l], nsa_cmp_k_w1[l], nsa_cmp_k_w2[l],
                      nsa_cmp_pos_v[l], nsa_cmp_v_w1[l], nsa_cmp_v_w2[l], b, t)], b, "seq_mixers")
        o_c = _nsa(nq, misc, nsa_gate_bias[l], nsa_q_norm[l], ks, vs, kw, vw, kcmp, vcmp, b, t)
        xf = _ffn(xf, ffn2_norm[l], ffn2_w1[l], ffn2_w3[l], ffn2_w2[l], mixers=(o_a, o_b, o_c, o_d), w_out=w_out[l])
    return xf.reshape(b, t, d)
```
